```python
import math
import jax
import jax.numpy as jnp
from jax import lax
import numpy as np

D_MODEL = 2048
BATCH = 4
SEQ = 4096
DEPTH = 2

GRID_W = 64
CTX_LEN = 256
EPS = 1e-6
D_FF = ((8 * D_MODEL + 3 * 256 - 1) // (3 * 256)) * 256

S5_WIDTH = D_MODEL // 2
S5_GROUP = 16
S5_GROUPS = S5_WIDTH // S5_GROUP
S5_STATE = 64

DN_DK = 128
DN_DV = 128
DN_HEADS = (D_MODEL // 2) // DN_DV
DN_WIDTH = DN_HEADS * DN_DV
DN_CHUNK = 64
SHORT_CONV = 4

LRU_WIDTH = D_MODEL // 2
LRU_BLOCK = 128
LRU_BLOCKS = LRU_WIDTH // LRU_BLOCK
LRU_C = 8.0
LRU_CONV = 4

RET_DK = 256
RET_DV = 512
RET_HEADS = (D_MODEL // 2) // RET_DK
RET_QK = RET_HEADS * RET_DK
RET_V = RET_HEADS * RET_DV
RET_CHUNK = 128
ROPE_BASE = 10000.0

EVEN_IN_SIZES = (S5_WIDTH, 3 * DN_WIDTH, DN_WIDTH, 2 * DN_HEADS, 2 * DN_HEADS)
EVEN_IN = sum(EVEN_IN_SIZES)
EVEN_OUT = S5_WIDTH + DN_WIDTH
ODD_IN_SIZES = (LRU_WIDTH, LRU_WIDTH, RET_QK, RET_QK, RET_V, RET_V)
ODD_IN = sum(ODD_IN_SIZES)
ODD_OUT = LRU_WIDTH + RET_V

kernel_name = 'hybrid_s5_deltanet_rglru_retention_dit'


def _identity(t):
    return t


def _flip(t):
    return jnp.flip(t, axis=1)


def _split(p, sizes):
    return jnp.split(p, np.cumsum(sizes)[:-1].tolist(), axis=-1)


def rmsnorm(x, g=None):
    xf = x.astype(jnp.float32)
    y = xf * lax.rsqrt(jnp.mean(xf * xf, axis=-1, keepdims=True) + EPS)
    if g is not None:
        y = y * g
    return y.astype(x.dtype)


def l2norm(x):
    xf = x.astype(jnp.float32)
    return (xf * lax.rsqrt(jnp.sum(xf * xf, axis=-1, keepdims=True) + EPS)).astype(x.dtype)


def adaln(x, g, shift, scale):
    return rmsnorm(x, g) * (1.0 + scale) + shift


def swiglu(h, w1, w3, w2):
    return (jax.nn.silu(h @ w1) * (h @ w3)) @ w2


def centred_depthwise_conv(x, w, b=None):
    k = w.shape[-1]
    y = lax.conv_general_dilated(
        x, jnp.transpose(w)[:, None, :].astype(x.dtype), window_strides=(1,),
        padding=[(k // 2, k - 1 - k // 2)], dimension_numbers=('NWC', 'WIO', 'NWC'),
        feature_group_count=x.shape[-1])
    if b is not None:
        y = y + b
    return y


def _linear_combine(e1, e2):
    a1, b1 = e1
    a2, b2 = e2
    return a1 * a2, a2 * b1 + b2


def _cmul(ar, ai, br, bi):
    return ar * br - ai * bi, ar * bi + ai * br


def _complex_linear_combine(e1, e2):
    a1r, a1i, b1r, b1i = e1
    a2r, a2i, b2r, b2i = e2
    ar, ai = _cmul(a2r, a2i, a1r, a1i)
    br, bi = _cmul(a2r, a2i, b1r, b1i)
    return ar, ai, br + b2r, bi + b2i


def s5_discretise(lam_re, lam_im, log_step, b_re, b_im):
    lam_re = jnp.minimum(lam_re, -1e-4)
    dt = jnp.exp(log_step)[:, None]
    mag = jnp.exp(lam_re * dt)
    abar_re, abar_im = mag * jnp.cos(lam_im * dt), mag * jnp.sin(lam_im * dt)
    nr, ni = abar_re - 1.0, abar_im
    den = lam_re * lam_re + lam_im * lam_im
    coef_re = (nr * lam_re + ni * lam_im) / den
    coef_im = (ni * lam_re - nr * lam_im) / den
    bbar_re, bbar_im = _cmul(coef_re[..., None], coef_im[..., None], b_re, b_im)
    return abar_re, abar_im, bbar_re, bbar_im


def s5_scan(abar_re, abar_im, bu_re, bu_im, h0_re, h0_im):
    shape = (1, bu_re.shape[1]) + abar_re.shape
    a_re = jnp.broadcast_to(abar_re, shape)
    a_im = jnp.broadcast_to(abar_im, shape)
    acr, aci, hr, hi = lax.associative_scan(_complex_linear_combine, (a_re, a_im, bu_re, bu_im), axis=1)
    cr, ci = _cmul(acr, aci, h0_re[:, None], h0_im[:, None])
    return hr + cr, hi + ci


def s5_mixer(uc, ul, lam_re, lam_im, log_step, b_re, b_im, c_re, c_im, d_skip, glu_w, glu_b, need_ctx):
    def grouped(u):
        return u.reshape(u.shape[0], u.shape[1], S5_GROUPS, S5_GROUP)

    def drive(u, bbr, bbi):
        return jnp.einsum('blgs,gps->blgp', u, bbr), jnp.einsum('blgs,gps->blgp', u, bbi)

    def readout(hr, hi, cr, ci):
        y = jnp.einsum('blgp,gsp->blgs', hr, cr) - jnp.einsum('blgp,gsp->blgs', hi, ci)
        return y.reshape(y.shape[0], y.shape[1], S5_WIDTH)

    ugc, ugl = grouped(uc), grouped(ul)
    h0 = jnp.zeros((ul.shape[0], S5_GROUPS, S5_STATE), ul.dtype)
    yl = d_skip * ul
    yc = d_skip * uc if need_ctx else None
    for d in range(2):
        f = _identity if d == 0 else _flip
        abr, abi, bbr, bbi = s5_discretise(lam_re[d], lam_im[d], log_step[d], b_re[d], b_im[d])
        hcr, hci = s5_scan(abr, abi, *drive(f(ugc), bbr, bbi), h0, h0)
        hlr, hli = s5_scan(abr, abi, *drive(f(ugl), bbr, bbi), hcr[:, -1], hci[:, -1])
        yl = yl + f(readout(hlr, hli, c_re[d], c_im[d]))
        if need_ctx:
            yc = yc + f(readout(hcr, hci, c_re[d], c_im[d]))

    def glu(y):
        y = jax.nn.gelu(y)
        return y * jax.nn.sigmoid(y @ glu_w + glu_b)

    return (glu(yc) if need_ctx else None), glu(yl)


def gated_delta_chunks(q, k, v, g, beta, s0, with_out):
    bsz, n_tok, heads, dk = q.shape
    dv = v.shape[-1]
    c = DN_CHUNK
    n = n_tok // c

    def chunk(t):
        return jnp.moveaxis(t.reshape((bsz, n, c) + t.shape[2:]), 2, 3)

    q, k, v, beta = chunk(q), chunk(k), chunk(v), chunk(beta)
    gcum = jnp.cumsum(chunk(g).astype(jnp.float32), axis=-1)
    causal = jnp.tril(jnp.ones((c, c), dtype=bool))
    strict = jnp.tril(jnp.ones((c, c), dtype=bool), -1)
    gam = jnp.exp(jnp.where(causal, gcum[..., :, None] - gcum[..., None, :], -jnp.inf))
    kb = k * beta[..., None]
    m = jnp.where(strict, jnp.einsum('bnhid,bnhjd->bnhij', kb, k) * gam, 0.0)
    eye = jnp.eye(c, dtype=m.dtype)
    p = -m
    tinv = eye + p
    for _ in range(int(math.log2(c)) - 1):
        p = p @ p
        tinv = tinv @ (eye + p)
    u = tinv @ (v * beta[..., None])
    w = tinv @ (kb * jnp.exp(gcum)[..., None])
    g_last = gcum[..., -1]
    k_dec = k * jnp.exp(g_last[..., None] - gcum)[..., None]

    def step(s, inp):
        u_c, w_c, kd_c, gl_c = inp
        v_new = u_c - jnp.einsum('bhck,bhkv->bhcv', w_c, s)
        s_next = s * jnp.exp(gl_c)[..., None, None] + jnp.einsum('bhck,bhcv->bhkv', kd_c, v_new)
        return s_next, s

    def seq_first(t):
        return jnp.moveaxis(t, 1, 0)

    s_final, s_start = lax.scan(step, s0, (seq_first(u), seq_first(w), seq_first(k_dec), seq_first(g_last)))
    if not with_out:
        return None, s_final
    s_start = jnp.moveaxis(s_start, 0, 1)
    v_new = u - w @ s_start
    attn = jnp.einsum('bnhid,bnhjd->bnhij', q, k) * gam
    o = (q * jnp.exp(gcum)[..., None]) @ s_start + attn @ v_new
    return jnp.moveaxis(o, 3, 2).reshape(bsz, n_tok, heads, dv), s_final


def dn_prep(qkv, a_raw, b_raw, conv_w, a_log, dt_bias):
    bsz, n_tok, _ = qkv.shape
    qkv = jax.nn.silu(centred_depthwise_conv(qkv, conv_w))
    q, k, v = jnp.split(qkv, 3, axis=-1)
    q = l2norm(q.reshape(bsz, n_tok, DN_HEADS, DN_DK)) * (DN_DK ** -0.5)
    k = l2norm(k.reshape(bsz, n_tok, DN_HEADS, DN_DK))
    v = v.reshape(bsz, n_tok, DN_HEADS, DN_DV)
    a_raw = a_raw.reshape(bsz, n_tok, 2, DN_HEADS)
    b_raw = b_raw.reshape(bsz, n_tok, 2, DN_HEADS)
    g = -jnp.exp(a_log) * jax.nn.softplus(a_raw + dt_bias)
    beta = jax.nn.sigmoid(b_raw)
    return q, k, v, g, beta


def deltanet_mixer(qkv_c, gate_c, a_c, b_c, qkv_l, gate_l, a_l, b_l, conv_w, a_log, dt_bias, norm_g, need_ctx):
    qc, kc, vc, gc, bc = dn_prep(qkv_c, a_c, b_c, conv_w, a_log, dt_bias)
    ql, kl, vl, gl, bl = dn_prep(qkv_l, a_l, b_l, conv_w, a_log, dt_bias)
    s0 = jnp.zeros((ql.shape[0], DN_HEADS, DN_DK, DN_DV), jnp.float32)
    oc, ol = 0.0, 0.0
    for d in range(2):
        f = _identity if d == 0 else _flip
        oc_d, s_ctx = gated_delta_chunks(f(qc), f(kc), f(vc), f(gc[:, :, d]), f(bc[:, :, d]), s0, need_ctx)
        ol_d, _ = gated_delta_chunks(f(ql), f(kl), f(vl), f(gl[:, :, d]), f(bl[:, :, d]), s_ctx, True)
        ol = ol + f(ol_d)
        if need_ctx:
            oc = oc + f(oc_d)

    def finish(o, gate):
        bsz, n_tok = gate.shape[:2]
        gate = gate.reshape(bsz, n_tok, DN_HEADS, DN_DV)
        return (rmsnorm(o, norm_g) * jax.nn.silu(gate)).reshape(bsz, n_tok, DN_WIDTH)

    return (finish(oc, gate_c) if need_ctx else None), finish(ol, gate_l)


def rglru_scan(x, wa, ba, wx, bx, lam, h0):
    bsz, n_tok, _ = x.shape
    xf = x.astype(jnp.float32)
    xb = xf.reshape(bsz, n_tok, LRU_BLOCKS, LRU_BLOCK)
    r = jax.nn.sigmoid(jnp.einsum('blnd,nde->blne', xb, wa).reshape(bsz, n_tok, LRU_WIDTH) + ba)
    i = jax.nn.sigmoid(jnp.einsum('blnd,nde->blne', xb, wx).reshape(bsz, n_tok, LRU_WIDTH) + bx)
    log_a = -LRU_C * r * jax.nn.softplus(-lam.astype(jnp.float32))
    a = jnp.exp(log_a)
    b = jnp.sqrt(jnp.maximum(-jnp.expm1(2.0 * log_a), 0.0)) * (i * xf)
    a_cum, h = lax.associative_scan(_linear_combine, (a, b), axis=1)
    return h + a_cum * h0[:, None]


def rglru_mixer(xc, gc, xl, gl, conv_w, conv_b, wa, ba, wx, bx, lam, need_ctx):
    xc = centred_depthwise_conv(xc, conv_w, conv_b)
    xl = centred_depthwise_conv(xl, conv_w, conv_b)
    h0 = jnp.zeros((xl.shape[0], LRU_WIDTH), jnp.float32)
    yc, yl = 0.0, 0.0
    for d in range(2):
        f = _identity if d == 0 else _flip
        hc = rglru_scan(f(xc), wa[d], ba[d], wx[d], bx[d], lam[d], h0)
        hl = rglru_scan(f(xl), wa[d], ba[d], wx[d], bx[d], lam[d], hc[:, -1])
        yl = yl + f(hl)
        if need_ctx:
            yc = yc + f(hc)
    out_l = yl * jax.nn.gelu(gl)
    out_c = yc * jax.nn.gelu(gc) if need_ctx else None
    return out_c, out_l


def axial_rope_tables(n_tokens):
    rows = n_tokens // GRID_W
    r, col = jnp.meshgrid(jnp.arange(rows), jnp.arange(GRID_W), indexing='ij')
    n_freq = RET_DK // 4
    inv = ROPE_BASE ** (-jnp.arange(n_freq, dtype=jnp.float32) / n_freq)
    ang = jnp.concatenate([r.reshape(-1, 1) * inv, col.reshape(-1, 1) * inv], axis=-1)
    return jnp.cos(ang), jnp.sin(ang)


def apply_rope(t, cos, sin):
    half = t.shape[-1] // 2
    t1, t2 = t[..., :half], t[..., half:]
    cos, sin = cos[None, :, None, :], sin[None, :, None, :]
    return jnp.concatenate([t1 * cos - t2 * sin, t1 * sin + t2 * cos], axis=-1).astype(t.dtype)


def retention_chunks(q, k, v, log_gamma, s0, with_out):
    bsz, n_tok, heads, _ = q.shape
    dv = v.shape[-1]
    c = RET_CHUNK
    n = n_tok // c

    def chunk(t):
        return t.reshape(bsz, n, c, heads, t.shape[-1]).transpose(1, 0, 3, 2, 4)

    idx = jnp.arange(c, dtype=jnp.float32)
    lg = log_gamma.astype(jnp.float32)
    causal = jnp.tril(jnp.ones((c, c), dtype=bool))
    rel = idx[:, None] - idx[None, :]
    dec = jnp.exp(jnp.where(causal, rel[None] * lg[:, None, None], -jnp.inf))
    xi = jnp.exp((idx[None] + 1.0) * lg[:, None])
    zeta = jnp.exp((c - 1.0 - idx[None]) * lg[:, None])
    g_chunk = jnp.exp(c * lg)

    def step(s, inp):
        qc, kc, vc = inp
        s_next = s * g_chunk[None, :, None, None] + jnp.einsum('bhcd,bhcv->bhdv', kc * zeta[None, :, :, None], vc)
        if not with_out:
            return s_next, None
        inner = jnp.einsum('bhid,bhjd->bhij', qc, kc) * dec[None]
        o = jnp.einsum('bhij,bhjv->bhiv', inner, vc) + jnp.einsum('bhcd,bhdv->bhcv', qc * xi[None, :, :, None], s)
        return s_next, o

    s_final, o = lax.scan(step, s0, (chunk(q), chunk(k), chunk(v)))
    if not with_out:
        return None, s_final
    return o.transpose(1, 0, 3, 2, 4).reshape(bsz, n_tok, heads, dv), s_final


def retention_mixer(qc, kc, vc, rc, ql, kl, vl, rl, theta, cos, sin, need_ctx):
    def heads(t, dh):
        return t.reshape(t.shape[0], t.shape[1], RET_HEADS, dh)

    scale = RET_DK ** -0.5
    qc, kc, vc = heads(qc, RET_DK), heads(kc, RET_DK) * scale, heads(vc, RET_DV)
    ql = apply_rope(heads(ql, RET_DK), cos, sin)
    kl = apply_rope(heads(kl, RET_DK), cos, sin) * scale
    vl = heads(vl, RET_DV)
    log_gamma = -jnp.exp(theta.astype(jnp.float32))
    s0 = jnp.zeros((ql.shape[0], RET_HEADS, RET_DK, RET_DV), jnp.float32)
    oc, ol = 0.0, 0.0
    for d in range(2):
        f = _identity if d == 0 else _flip
        oc_d, s_ctx = retention_chunks(f(qc), f(kc), f(vc), log_gamma[d], s0, need_ctx)
        ol_d, _ = retention_chunks(f(ql), f(kl), f(vl), log_gamma[d], s_ctx, True)
        ol = ol + f(ol_d)
        if need_ctx:
            oc = oc + f(oc_d)

    def finish(o, gate):
        return (rmsnorm(o) * jax.nn.silu(heads(gate, RET_DV))).reshape(gate.shape[0], gate.shape[1], RET_V)

    return (finish(oc, rc) if need_ctx else None), finish(ol, rl)


def even_mixer(hc, hl, w_in, w_out, lam_re, lam_im, log_step, b_re, b_im, c_re, c_im, d_skip, glu_w, glu_b,
               dn_conv_w, dn_a_log, dn_dt_bias, dn_norm_g, need_ctx):
    uc, qkv_c, gate_c, a_c, b_c = _split(hc @ w_in, EVEN_IN_SIZES)
    ul, qkv_l, gate_l, a_l, b_l = _split(hl @ w_in, EVEN_IN_SIZES)
    s5_c, s5_l = s5_mixer(uc, ul, lam_re, lam_im, log_step, b_re, b_im, c_re, c_im, d_skip, glu_w, glu_b, need_ctx)
    dn_c, dn_l = deltanet_mixer(qkv_c, gate_c, a_c, b_c, qkv_l, gate_l, a_l, b_l,
                                dn_conv_w, dn_a_log, dn_dt_bias, dn_norm_g, need_ctx)
    yl = jnp.concatenate([s5_l, dn_l], axis=-1) @ w_out
    yc = jnp.concatenate([s5_c, dn_c], axis=-1) @ w_out if need_ctx else None
    return yc, yl


def odd_mixer(hc, hl, w_in, w_out, conv_w, conv_b, wa, ba, wx, bx, lam, ret_theta, cos, sin, need_ctx):
    xc, gc, qc, kc, vc, rc = _split(hc @ w_in, ODD_IN_SIZES)
    xl, gl, ql, kl, vl, rl = _split(hl @ w_in, ODD_IN_SIZES)
    lru_c, lru_l = rglru_mixer(xc, gc, xl, gl, conv_w, conv_b, wa, ba, wx, bx, lam, need_ctx)
    ret_c, ret_l = retention_mixer(qc, kc, vc, rc, ql, kl, vl, rl, ret_theta, cos, sin, need_ctx)
    yl = jnp.concatenate([lru_l, ret_l], axis=-1) @ w_out
    yc = jnp.concatenate([lru_c, ret_c], axis=-1) @ w_out if need_ctx else None
    return yc, yl


def setup_inputs(seed: int = 0) -> dict:
    key = jax.random.key(seed)
    keys = jax.random.split(key, 48)
    counter = iter(range(48))
    f32 = jnp.float32

    def nk():
        return keys[next(counter)]

    def normal(shape, scale):
        return jax.random.normal(nk(), shape, f32) * scale

    def uniform(shape, lo, hi):
        return jax.random.uniform(nk(), shape, f32, lo, hi)

    D = D_MODEL
    NE = (DEPTH + 1) // 2
    NO = DEPTH // 2
    n_idx = jnp.arange(S5_STATE, dtype=f32)
    dt_dn = jnp.exp(uniform((NE, 2, DN_HEADS), math.log(1e-3), math.log(1e-1)))
    a_lru = uniform((NO, 2, LRU_WIDTH), 0.9, 0.999) ** (1.0 / LRU_C)
    h_idx = jnp.arange(RET_HEADS, dtype=f32)
    ret_base = jnp.log(-jnp.log1p(-jnp.exp2(-5.0 - h_idx)))
    return {
        'x': normal((BATCH, SEQ, D), 1.0),
        'c': normal((BATCH, D), 1.0),
        'ctx': normal((BATCH, CTX_LEN, D), 1.0),
        'c_ctx': normal((D,), 1.0),
        'mod_w': normal((DEPTH, D, 6 * D), 0.5 * D ** -0.5),
        'mod_b': normal((DEPTH, 6 * D), 0.02),
        'norm1_g': 1.0 + normal((DEPTH, D), 0.02),
        'norm2_g': 1.0 + normal((DEPTH, D), 0.02),
        'ffn_w1': normal((DEPTH, D, D_FF), D ** -0.5),
        'ffn_w3': normal((DEPTH, D, D_FF), D ** -0.5),
        'ffn_w2': normal((DEPTH, D_FF, D), D_FF ** -0.5),
        'final_g': 1.0 + normal((D,), 0.02),
        'even_w_in': normal((NE, D, EVEN_IN), D ** -0.5),
        'even_w_out': normal((NE, EVEN_OUT, D), EVEN_OUT ** -0.5),
        's5_lam_re': -0.5 + normal((NE, 2, S5_GROUPS, S5_STATE), 0.01),
        's5_lam_im': math.pi * n_idx + normal((NE, 2, S5_GROUPS, S5_STATE), 0.01),
        's5_log_step': uniform((NE, 2, S5_GROUPS), math.log(1e-3), math.log(1e-1)),
        's5_b_re': normal((NE, 2, S5_GROUPS, S5_STATE, S5_GROUP), (2 * S5_GROUP) ** -0.5),
        's5_b_im': normal((NE, 2, S5_GROUPS, S5_STATE, S5_GROUP), (2 * S5_GROUP) ** -0.5),
        's5_c_re': normal((NE, 2, S5_GROUPS, S5_GROUP, S5_STATE), (2 * S5_STATE) ** -0.5),
        's5_c_im': normal((NE, 2, S5_GROUPS, S5_GROUP, S5_STATE), (2 * S5_STATE) ** -0.5),
        's5_d': normal((NE, S5_WIDTH), 0.5),
        's5_glu_w': normal((NE, S5_WIDTH, S5_WIDTH), S5_WIDTH ** -0.5),
        's5_glu_b': normal((NE, S5_WIDTH), 0.02),
        'dn_conv_w': normal((NE, 3 * DN_WIDTH, SHORT_CONV), SHORT_CONV ** -0.5),
        'dn_a_log': jnp.log(uniform((NE, 2, DN_HEADS), 1.0, 16.0)),
        'dn_dt_bias': dt_dn + jnp.log(-jnp.expm1(-dt_dn)),
        'dn_norm_g': 1.0 + normal((NE, DN_DV), 0.02),
        'odd_w_in': normal((NO, D, ODD_IN), D ** -0.5),
        'odd_w_out': normal((NO, ODD_OUT, D), ODD_OUT ** -0.5),
        'lru_conv_w': normal((NO, LRU_WIDTH, LRU_CONV), LRU_CONV ** -0.5),
        'lru_conv_b': normal((NO, LRU_WIDTH), 0.02),
        'lru_wa': normal((NO, 2, LRU_BLOCKS, LRU_BLOCK, LRU_BLOCK), LRU_BLOCK ** -0.5),
        'lru_ba': normal((NO, 2, LRU_WIDTH), 0.02),
        'lru_wx': normal((NO, 2, LRU_BLOCKS, LRU_BLOCK, LRU_BLOCK), LRU_BLOCK ** -0.5),
        'lru_bx': normal((NO, 2, LRU_WIDTH), 0.02),
        'lru_lam': jnp.log(a_lru) - jnp.log1p(-a_lru),
        'ret_theta': ret_base + normal((NO, 2, RET_HEADS), 0.01),
    }


def reference(x, c, ctx, c_ctx, mod_w, mod_b, norm1_g, norm2_g, ffn_w1, ffn_w3, ffn_w2, final_g,
              even_w_in, even_w_out, s5_lam_re, s5_lam_im, s5_log_step, s5_b_re, s5_b_im, s5_c_re, s5_c_im,
              s5_d, s5_glu_w, s5_glu_b, dn_conv_w, dn_a_log, dn_dt_bias, dn_norm_g,
              odd_w_in, odd_w_out, lru_conv_w, lru_conv_b, lru_wa, lru_ba, lru_wx, lru_bx, lru_lam, ret_theta):
    cos, sin = axial_rope_tables(x.shape[1])
    sc = jax.nn.silu(c)
    scc = jax.nn.silu(c_ctx)
    xl, xc = x, ctx
    for i in range(DEPTH):
        need_ctx = i < DEPTH - 1
        j = i // 2
        mod_l = [m[:, None, :] for m in jnp.split(sc @ mod_w[i] + mod_b[i], 6, axis=-1)]
        mod_c = jnp.split(scc @ mod_w[i] + mod_b[i], 6, axis=-1)
        hl = adaln(xl, norm1_g[i], mod_l[0], mod_l[1])
        hc = adaln(xc, norm1_g[i], mod_c[0], mod_c[1])
        if i % 2 == 0:
            yc, yl = even_mixer(hc, hl, even_w_in[j], even_w_out[j], s5_lam_re[j], s5_lam_im[j], s5_log_step[j],
                                s5_b_re[j], s5_b_im[j], s5_c_re[j], s5_c_im[j], s5_d[j], s5_glu_w[j], s5_glu_b[j],
                                dn_conv_w[j], dn_a_log[j], dn_dt_bias[j], dn_norm_g[j], need_ctx)
        else:
            yc, yl = odd_mixer(hc, hl, odd_w_in[j], odd_w_out[j], lru_conv_w[j], lru_conv_b[j], lru_wa[j], lru_ba[j],
                               lru_wx[j], lru_bx[j], lru_lam[j], ret_theta[j], cos, sin, need_ctx)
        xl = xl + mod_l[2] * yl
        hl = adaln(xl, norm2_g[i], mod_l[3], mod_l[4])
        xl = xl + mod_l[5] * swiglu(hl, ffn_w1[i], ffn_w3[i], ffn_w2[i])
        if need_ctx:
            xc = xc + mod_c[2] * yc
            hc = adaln(xc, norm2_g[i], mod_c[3], mod_c[4])
            xc = xc + mod_c[5] * swiglu(hc, ffn_w1[i], ffn_w3[i], ffn_w2[i])
    return rmsnorm(xl, final_g)
```

```python
import functools
import math

import jax
import jax.numpy as jnp
from jax import lax
from jax.experimental import pallas as pl
from jax.experimental.pallas import tpu as pltpu

F32 = jnp.float32
BF16 = jnp.bfloat16

EPS = 1e-6
GRID_W = 64
ROPE_BASE = 10000.0

S5_GROUP = 16
S5_STATE = 64
S5_SLAB_GROUPS = 8
S5_SLAB_STATES = S5_SLAB_GROUPS * S5_STATE

DN_DK = 128
DN_CHUNK = 64
SHORT_CONV = 4

LRU_BLOCK = 128
LRU_C = 8.0

RET_DK = 256
RET_DV = 512
RET_CHUNK = 128

LANES = 128
SUBLANES = 8
VMEM_LIMIT = 56 * 1024 * 1024


def _params(*sem):
    return pltpu.CompilerParams(dimension_semantics=sem, vmem_limit_bytes=VMEM_LIMIT)


def _silu(x):
    return x * jax.nn.sigmoid(x)


def _gelu(x):
    return 0.5 * x * (1.0 + jnp.tanh(0.7978845608028654 * (x + 0.044715 * (x * x * x))))


def _softplus(x):
    return jnp.maximum(x, 0.0) + jnp.log1p(jnp.exp(-jnp.abs(x)))


def _expm1(x):
    u = jnp.exp(x)
    safe = jnp.where(u == 1.0, 2.0, u)
    return jnp.where(u == 1.0, x, (u - 1.0) * x / jnp.where(u == 0.0, x, jnp.log(safe)))


def _adaln(x, g, shift, scale):
    y = x * lax.rsqrt(jnp.mean(x * x, axis=-1, keepdims=True) + EPS)
    return (y * g) * (1.0 + scale) + shift


def _dot(a, b):
    return jnp.dot(a, b, preferred_element_type=F32)


def _dot_nt(a, b):
    return lax.dot_general(a, b, (((1,), (1,)), ((), ())), preferred_element_type=F32)


def _mod_kernel(s_ref, w_ref, b_ref, o_ref):
    s = _silu(s_ref[...])
    o_ref[0] = _dot(s.astype(BF16), w_ref[0].astype(BF16)) + b_ref[0]


def _mod_call(rows, mod_w, mod_b):
    depth, d, n = mod_w.shape
    tn = 1024
    return pl.pallas_call(
        _mod_kernel,
        grid=(depth, n // tn),
        in_specs=[pl.BlockSpec((SUBLANES, d), lambda i, j: (0, 0)),
                  pl.BlockSpec((1, d, tn), lambda i, j: (i, 0, j)),
                  pl.BlockSpec((1, 1, tn), lambda i, j: (i, 0, j))],
        out_specs=pl.BlockSpec((1, SUBLANES, tn), lambda i, j: (i, 0, j)),
        out_shape=jax.ShapeDtypeStruct((depth, SUBLANES, n), F32),
        compiler_params=_params("parallel", "parallel"),
        name="mod_proj",
    )(rows, mod_w, mod_b.reshape(depth, 1, n))


def _inproj_kernel(x_ref, g_ref, sh_ref, sc_ref, w_ref, *rest, has_small):
    if has_small:
        ws_ref, o_ref, os_ref, h_ref = rest
    else:
        o_ref, h_ref = rest

    @pl.when(pl.program_id(2) == 0)
    def _():
        h = _adaln(x_ref[0], g_ref[...], sh_ref[0], sc_ref[0]).astype(BF16)
        h_ref[...] = h
        if has_small:
            os_ref[0] = _dot(h, ws_ref[...])

    o_ref[0] = _dot(h_ref[...], w_ref[...])


def _inproj_call(x, g, shift, scale, w, w_small=None):
    b, l, d = x.shape
    n = w.shape[1]
    tm = min(l, 1024)
    tn = 512
    has_small = w_small is not None
    in_specs = [pl.BlockSpec((1, tm, d), lambda bi, i, j: (bi, i, 0)),
                pl.BlockSpec((1, d), lambda bi, i, j: (0, 0)),
                pl.BlockSpec((1, 1, d), lambda bi, i, j: (bi, 0, 0)),
                pl.BlockSpec((1, 1, d), lambda bi, i, j: (bi, 0, 0)),
                pl.BlockSpec((d, tn), lambda bi, i, j: (0, j))]
    out_specs = [pl.BlockSpec((1, tm, tn), lambda bi, i, j: (bi, i, j))]
    out_shape = [jax.ShapeDtypeStruct((b, l, n), F32)]
    args = [x, g.reshape(1, d), shift, scale, w]
    if has_small:
        in_specs.append(pl.BlockSpec((d, LANES), lambda bi, i, j: (0, 0)))
        out_specs.append(pl.BlockSpec((1, tm, LANES), lambda bi, i, j: (bi, i, 0)))
        out_shape.append(jax.ShapeDtypeStruct((b, l, LANES), F32))
        args.append(w_small)
    outs = pl.pallas_call(
        functools.partial(_inproj_kernel, has_small=has_small),
        grid=(b, l // tm, n // tn),
        in_specs=in_specs, out_specs=out_specs, out_shape=out_shape,
        scratch_shapes=[pltpu.VMEM((tm, d), BF16)],
        compiler_params=_params("parallel", "parallel", "arbitrary"),
        name="adaln_inproj",
    )(*args)
    return outs if has_small else outs[0]


def _outproj_kernel(x_ref, gate_ref, a1_ref, a2_ref, w1_ref, w2_ref, o_ref):
    y = _dot(a1_ref[0], w1_ref[...]) + _dot(a2_ref[0], w2_ref[...])
    o_ref[0] = x_ref[0] + gate_ref[0] * y


def _outproj_call(x, gate, a1, a2, w1, w2):
    b, l, d = x.shape
    k1, k2 = a1.shape[2], a2.shape[2]
    tm = min(l, 1024)
    tn = 512
    return pl.pallas_call(
        _outproj_kernel,
        grid=(b, l // tm, d // tn),
        in_specs=[pl.BlockSpec((1, tm, tn), lambda bi, i, j: (bi, i, j)),
                  pl.BlockSpec((1, 1, tn), lambda bi, i, j: (bi, 0, j)),
                  pl.BlockSpec((1, tm, k1), lambda bi, i, j: (bi, i, 0)),
                  pl.BlockSpec((1, tm, k2), lambda bi, i, j: (bi, i, 0)),
                  pl.BlockSpec((k1, tn), lambda bi, i, j: (0, j)),
                  pl.BlockSpec((k2, tn), lambda bi, i, j: (0, j))],
        out_specs=pl.BlockSpec((1, tm, tn), lambda bi, i, j: (bi, i, j)),
        out_shape=jax.ShapeDtypeStruct((b, l, d), F32),
        compiler_params=_params("parallel", "parallel", "arbitrary"),
        name="outproj_residual",
    )(x, gate, a1, a2, w1, w2)


def _ffn_kernel(x_ref, g_ref, sh_ref, sc_ref, gate_ref, w1_ref, w3_ref, w2_ref, fg_ref, o_ref,
                h_ref, acc_ref, *, final_norm):
    f = pl.program_id(2)

    @pl.when(f == 0)
    def _():
        h_ref[...] = _adaln(x_ref[0], g_ref[...], sh_ref[0], sc_ref[0]).astype(BF16)
        acc_ref[...] = jnp.zeros_like(acc_ref)

    h = h_ref[...]
    a = _dot(h, w1_ref[...])
    b = _dot(h, w3_ref[...])
    acc_ref[...] += _dot((_silu(a) * b).astype(BF16), w2_ref[...])

    @pl.when(f == pl.num_programs(2) - 1)
    def _():
        y = x_ref[0] + gate_ref[0] * acc_ref[...]
        if final_norm:
            y = y * lax.rsqrt(jnp.mean(y * y, axis=-1, keepdims=True) + EPS) * fg_ref[...]
        o_ref[0] = y


def _ffn_call(x, g, shift, scale, gate, w1, w3, w2, final_g, final_norm):
    b, l, d = x.shape
    dff = w1.shape[1]
    tm = min(l, 512)
    tf = 512
    vec = pl.BlockSpec((1, 1, d), lambda bi, i, f: (bi, 0, 0))
    return pl.pallas_call(
        functools.partial(_ffn_kernel, final_norm=final_norm),
        grid=(b, l // tm, dff // tf),
        in_specs=[pl.BlockSpec((1, tm, d), lambda bi, i, f: (bi, i, 0)),
                  pl.BlockSpec((1, d), lambda bi, i, f: (0, 0)),
                  vec, vec, vec,
                  pl.BlockSpec((d, tf), lambda bi, i, f: (0, f)),
                  pl.BlockSpec((d, tf), lambda bi, i, f: (0, f)),
                  pl.BlockSpec((tf, d), lambda bi, i, f: (f, 0)),
                  pl.BlockSpec((1, d), lambda bi, i, f: (0, 0))],
        out_specs=pl.BlockSpec((1, tm, d), lambda bi, i, f: (bi, i, 0)),
        out_shape=jax.ShapeDtypeStruct((b, l, d), F32),
        scratch_shapes=[pltpu.VMEM((tm, d), BF16), pltpu.VMEM((tm, d), F32)],
        compiler_params=_params("parallel", "parallel", "arbitrary"),
        name="ffn_swiglu",
    )(x, g.reshape(1, d), shift, scale, gate, w1, w3, w2, final_g.reshape(1, d))


def _s5_disc_kernel(lre_ref, lim_ref, ls_ref, bre_ref, bim_ref, are_ref, aim_ref, bbre_ref, bbim_ref):
    lre = jnp.minimum(lre_ref[...], -1e-4)
    lim = lim_ref[...]
    dt = jnp.exp(ls_ref[...])
    mag = jnp.exp(lre * dt)
    ar = mag * jnp.cos(lim * dt)
    ai = mag * jnp.sin(lim * dt)
    nr, ni = ar - 1.0, ai
    den = lre * lre + lim * lim
    cr = (nr * lre + ni * lim) / den
    ci = (ni * lre - nr * lim) / den
    bre, bim = bre_ref[...], bim_ref[...]
    are_ref[...] = ar
    aim_ref[...] = ai
    bbre_ref[...] = cr * bre - ci * bim
    bbim_ref[...] = cr * bim + ci * bre


def _s5_weights(lam_re, lam_im, log_step, b_re, b_im, c_re, c_im):
    nd, g, p = lam_re.shape
    s = b_re.shape[-1]
    rows = nd * g * s
    rep = lambda t: jnp.repeat(t.reshape(nd * g, p), s, axis=0)
    ls = jnp.broadcast_to(log_step.reshape(nd * g, 1), (nd * g, p))
    tb = lambda t: jnp.transpose(t, (0, 1, 3, 2)).reshape(rows, p)
    shp = jax.ShapeDtypeStruct((rows, p), F32)
    are, aim, bbre, bbim = pl.pallas_call(
        _s5_disc_kernel, out_shape=[shp, shp, shp, shp], name="s5_discretise",
    )(rep(lam_re), rep(lam_im), rep(ls), tb(b_re), tb(b_im))
    nslab = g // S5_SLAB_GROUPS
    eye = jnp.eye(S5_SLAB_GROUPS, dtype=F32)
    abar = jnp.concatenate([are[::s].reshape(nd, nslab, 1, S5_SLAB_STATES),
                            aim[::s].reshape(nd, nslab, 1, S5_SLAB_STATES)], axis=-1)

    def drive(t):
        t = t.reshape(nd, nslab, S5_SLAB_GROUPS, s, p)
        return jnp.einsum('dagsp,gh->dagshp', t, eye).reshape(nd, nslab, S5_SLAB_GROUPS * s, S5_SLAB_STATES)

    def readout(t):
        t = t.reshape(nd, nslab, S5_SLAB_GROUPS, s, p)
        return jnp.einsum('dagsp,gh->dagphs', t, eye).reshape(nd, nslab, S5_SLAB_STATES, S5_SLAB_GROUPS * s)

    wd = jnp.concatenate([drive(bbre), drive(bbim)], axis=-1).astype(BF16)
    wo = jnp.concatenate([readout(c_re), readout(-c_im)], axis=-2).astype(BF16)
    return abar, wd, wo


def _s5_kernel(ul_ref, uc_ref, wd_ref, wo_ref, a_ref, dsk_ref, yl_ref, yc_ref, hbuf, tab, *, tb):
    n_lat, n_ctx = ul_ref.shape[1], uc_ref.shape[1]
    hs = S5_SLAB_STATES
    row = lax.broadcasted_iota(jnp.int32, (SUBLANES, hs), 0)

    for d in range(2):
        a1r = jnp.broadcast_to(a_ref[d, 0, :, 0:hs], (SUBLANES, hs))
        a1i = jnp.broadcast_to(a_ref[d, 0, :, hs:2 * hs], (SUBLANES, hs))
        pw = [(a1r, a1i)]
        for _ in range(SUBLANES - 1):
            pr, pi = pw[-1]
            pw.append((pr * a1r - pi * a1i, pr * a1i + pi * a1r))
        for idx, k in enumerate((1, 2, 4)):
            keep = (row >= k) if d == 0 else (row <= SUBLANES - 1 - k)
            tab[d, 2 * idx] = jnp.where(keep, pw[k - 1][0], 0.0)
            tab[d, 2 * idx + 1] = jnp.where(keep, pw[k - 1][1], 0.0)
        cr = jnp.zeros((SUBLANES, hs), F32)
        ci = jnp.zeros((SUBLANES, hs), F32)
        for r in range(SUBLANES):
            e = r + 1 if d == 0 else SUBLANES - r
            cr = jnp.where(row == r, pw[e - 1][0], cr)
            ci = jnp.where(row == r, pw[e - 1][1], ci)
        tab[d, 6] = cr
        tab[d, 7] = ci

    def tile_body(d, ntiles, t, carry):
        cr, ci = carry
        ti = t if d == 0 else ntiles - 1 - t
        r0 = pl.multiple_of(ti * SUBLANES, SUBLANES)
        xr = hbuf[pl.ds(r0, SUBLANES), 0:hs]
        xi = hbuf[pl.ds(r0, SUBLANES), hs:2 * hs]
        for idx, k in enumerate((1, 2, 4)):
            sh = k if d == 0 else SUBLANES - k
            sr = pltpu.roll(xr, sh, 0)
            si = pltpu.roll(xi, sh, 0)
            akr, aki = tab[d, 2 * idx], tab[d, 2 * idx + 1]
            xr, xi = xr + akr * sr - aki * si, xi + akr * si + aki * sr
        pr, pi = tab[d, 6], tab[d, 7]
        hr = xr + pr * cr - pi * ci
        hi = xi + pr * ci + pi * cr
        hbuf[pl.ds(r0, SUBLANES), 0:hs] = hr
        hbuf[pl.ds(r0, SUBLANES), hs:2 * hs] = hi
        last = SUBLANES - 1 if d == 0 else 0
        return (jnp.broadcast_to(hr[last:last + 1], (SUBLANES, hs)),
                jnp.broadcast_to(hi[last:last + 1], (SUBLANES, hs)))

    def run_segment(d, u_ref, y_ref, n_rows, carry):
        tbs = min(tb, n_rows)
        nblk = n_rows // tbs

        def blk_body(i, carry):
            bi = i if d == 0 else nblk - 1 - i
            r0 = pl.multiple_of(bi * tbs, tbs)
            ub = u_ref[0, pl.ds(r0, tbs), :]
            hbuf[0:tbs, :] = _dot(ub.astype(BF16), wd_ref[d, 0])
            carry = lax.fori_loop(0, tbs // SUBLANES,
                                  functools.partial(tile_body, d, tbs // SUBLANES), carry)
            yb = _dot(hbuf[0:tbs, :].astype(BF16), wo_ref[d, 0])
            if d == 0:
                y_ref[0, pl.ds(r0, tbs), :] = dsk_ref[...] * ub + yb
            else:
                y_ref[0, pl.ds(r0, tbs), :] += yb
            return carry

        return lax.fori_loop(0, nblk, blk_body, carry)

    for d in range(2):
        carry = (jnp.zeros((SUBLANES, hs), F32), jnp.zeros((SUBLANES, hs), F32))
        carry = run_segment(d, uc_ref, yc_ref, n_ctx, carry)
        run_segment(d, ul_ref, yl_ref, n_lat, carry)


def _s5_scan_call(main_l, main_c, abar, wd, wo, d_skip):
    b, l, _ = main_l.shape
    lc = main_c.shape[1]
    width = d_skip.shape[0]
    nslab = width // LANES
    tb = 256
    return pl.pallas_call(
        functools.partial(_s5_kernel, tb=tb),
        grid=(b, nslab),
        in_specs=[pl.BlockSpec((1, l, LANES), lambda bi, s: (bi, 0, s)),
                  pl.BlockSpec((1, lc, LANES), lambda bi, s: (bi, 0, s)),
                  pl.BlockSpec((2, 1, LANES, 2 * S5_SLAB_STATES), lambda bi, s: (0, s, 0, 0)),
                  pl.BlockSpec((2, 1, 2 * S5_SLAB_STATES, LANES), lambda bi, s: (0, s, 0, 0)),
                  pl.BlockSpec((2, 1, 1, 2 * S5_SLAB_STATES), lambda bi, s: (0, s, 0, 0)),
                  pl.BlockSpec((1, LANES), lambda bi, s: (0, s))],
        out_specs=[pl.BlockSpec((1, l, LANES), lambda bi, s: (bi, 0, s)),
                   pl.BlockSpec((1, lc, LANES), lambda bi, s: (bi, 0, s))],
        out_shape=[jax.ShapeDtypeStruct((b, l, width), F32),
                   jax.ShapeDtypeStruct((b, lc, width), F32)],
        scratch_shapes=[pltpu.VMEM((tb, 2 * S5_SLAB_STATES), F32),
                        pltpu.VMEM((2, 8, SUBLANES, S5_SLAB_STATES), F32)],
        compiler_params=_params("parallel", "parallel"),
        name="s5_scan",
    )(main_l, main_c, wd, wo, abar, d_skip.reshape(1, width))


def _s5_glu_kernel(y_ref, w_ref, b_ref, o_ref):
    g = _gelu(y_ref[0])
    o_ref[0] = (g * jax.nn.sigmoid(_dot(g.astype(BF16), w_ref[...]) + b_ref[...])).astype(BF16)


def _s5_glu_call(y, glu_w, glu_b):
    b, l, w = y.shape
    tm = min(l, 1024)
    return pl.pallas_call(
        _s5_glu_kernel,
        grid=(b, l // tm),
        in_specs=[pl.BlockSpec((1, tm, w), lambda bi, i: (bi, i, 0)),
                  pl.BlockSpec((w, w), lambda bi, i: (0, 0)),
                  pl.BlockSpec((1, w), lambda bi, i: (0, 0))],
        out_specs=pl.BlockSpec((1, tm, w), lambda bi, i: (bi, i, 0)),
        out_shape=jax.ShapeDtypeStruct((b, l, w), BF16),
        compiler_params=_params("parallel", "parallel"),
        name="s5_glu",
    )(y, glu_w, glu_b.reshape(1, w))


def _conv_rows(src_ref, n_rows, xpad, cw, bias, emit, tbs):
    zeros = jnp.zeros((SUBLANES, LANES), F32)
    xpad[0:SUBLANES, :] = zeros
    xpad[SUBLANES + n_rows:2 * SUBLANES + n_rows, :] = zeros
    tbs = min(tbs, n_rows)
    for r0 in range(0, n_rows, tbs):
        xpad[SUBLANES + r0:SUBLANES + r0 + tbs, :] = src_ref[0, r0:r0 + tbs, :]
    for r0 in range(0, n_rows, tbs):
        acc = None
        for j in range(SHORT_CONV):
            off = SUBLANES + r0 + j - SHORT_CONV // 2
            term = cw[j:j + 1, :] * xpad[off:off + tbs, :]
            acc = term if acc is None else acc + term
        if bias is not None:
            acc = acc + bias
        emit(r0, acc)


def _lru_kernel(xl_ref, xc_ref, gl_ref, cw_ref, cb_ref, wa_ref, wx_ref, ba_ref, bx_ref, lam_ref, o_ref,
                xpad, xconv, abuf, bbuf, hsum, *, tb):
    n_lat, n_ctx = xl_ref.shape[1], xc_ref.shape[1]
    cw = cw_ref[...]
    cb = cb_ref[...]

    def put(off):
        def emit(r0, y):
            xconv[off + r0:off + r0 + y.shape[0], :] = y
        return emit

    _conv_rows(xc_ref, n_ctx, xpad, cw, cb, put(0), tb)
    _conv_rows(xl_ref, n_lat, xpad, cw, cb, put(n_ctx), tb)

    row = lax.broadcasted_iota(jnp.int32, (SUBLANES, LANES), 0)

    def tile_body(d, ntiles, t, carry):
        ti = t if d == 0 else ntiles - 1 - t
        r0 = pl.multiple_of(ti * SUBLANES, SUBLANES)
        at = abuf[pl.ds(r0, SUBLANES), :]
        bt = bbuf[pl.ds(r0, SUBLANES), :]
        for k in (1, 2, 4):
            sh = k if d == 0 else SUBLANES - k
            keep = (row >= k) if d == 0 else (row <= SUBLANES - 1 - k)
            sa = jnp.where(keep, pltpu.roll(at, sh, 0), 1.0)
            sb = jnp.where(keep, pltpu.roll(bt, sh, 0), 0.0)
            bt = bt + at * sb
            at = at * sa
        h = bt + at * carry
        bbuf[pl.ds(r0, SUBLANES), :] = h
        last = SUBLANES - 1 if d == 0 else 0
        return jnp.broadcast_to(h[last:last + 1], (SUBLANES, LANES))

    for d in range(2):
        sp = _softplus(-lam_ref[d])
        wa, wx = wa_ref[d, 0], wx_ref[d, 0]
        ba, bx = ba_ref[d], bx_ref[d]

        def run_segment(off, n_rows, is_lat, carry, d=d, sp=sp, wa=wa, wx=wx, ba=ba, bx=bx):
            tbs = min(tb, n_rows)
            nblk = n_rows // tbs

            def blk_body(i, carry):
                bi = i if d == 0 else nblk - 1 - i
                lr0 = pl.multiple_of(bi * tbs, tbs)
                xc = xconv[pl.ds(off + lr0, tbs), :]
                xb = xc.astype(BF16)
                r = jax.nn.sigmoid(_dot(xb, wa) + ba)
                ig = jax.nn.sigmoid(_dot(xb, wx) + bx)
                log_a = -LRU_C * r * sp
                abuf[0:tbs, :] = jnp.exp(log_a)
                bbuf[0:tbs, :] = jnp.sqrt(jnp.maximum(-_expm1(2.0 * log_a), 0.0)) * (ig * xc)
                carry = lax.fori_loop(0, tbs // SUBLANES,
                                      functools.partial(tile_body, d, tbs // SUBLANES), carry)
                if is_lat:
                    if d == 0:
                        hsum[pl.ds(lr0, tbs), :] = bbuf[0:tbs, :]
                    else:
                        y = (hsum[pl.ds(lr0, tbs), :] + bbuf[0:tbs, :]) * _gelu(gl_ref[0, pl.ds(lr0, tbs), :])
                        o_ref[0, pl.ds(lr0, tbs), :] = y.astype(BF16)
                return carry

            return lax.fori_loop(0, nblk, blk_body, carry)

        carry = run_segment(0, n_ctx, False, jnp.zeros((SUBLANES, LANES), F32))
        run_segment(n_ctx, n_lat, True, carry)


def _lru_call(main_l, main_c, conv_w, conv_b, wa, ba, wx, bx, lam):
    b, l, _ = main_l.shape
    lc = main_c.shape[1]
    width = conv_b.shape[0]
    nb = width // LRU_BLOCK
    tb = 256
    vec = pl.BlockSpec((2, 1, LANES), lambda bi, n: (0, 0, n))
    wspec = pl.BlockSpec((2, 1, LRU_BLOCK, LRU_BLOCK), lambda bi, n: (0, n, 0, 0))
    return pl.pallas_call(
        functools.partial(_lru_kernel, tb=tb),
        grid=(b, nb),
        in_specs=[pl.BlockSpec((1, l, LANES), lambda bi, n: (bi, 0, n)),
                  pl.BlockSpec((1, lc, LANES), lambda bi, n: (bi, 0, n)),
                  pl.BlockSpec((1, l, LANES), lambda bi, n, nb=nb: (bi, 0, nb + n)),
                  pl.BlockSpec((SHORT_CONV, LANES), lambda bi, n: (0, n)),
                  pl.BlockSpec((1, LANES), lambda bi, n: (0, n)),
                  wspec, wspec, vec, vec, vec],
        out_specs=pl.BlockSpec((1, l, LANES), lambda bi, n: (bi, 0, n)),
        out_shape=jax.ShapeDtypeStruct((b, l, width), BF16),
        scratch_shapes=[pltpu.VMEM((l + 2 * SUBLANES, LANES), F32),
                        pltpu.VMEM((lc + l, LANES), F32),
                        pltpu.VMEM((tb, LANES), F32),
                        pltpu.VMEM((tb, LANES), F32),
                        pltpu.VMEM((l, LANES), F32)],
        compiler_params=_params("parallel", "parallel"),
        name="rglru",
    )(main_l, main_c, main_l, conv_w.T, conv_b.reshape(1, width), wa, wx,
      ba.reshape(2, 1, width), bx.reshape(2, 1, width), lam.reshape(2, 1, width))


def _dn_gate_kernel(ab_ref, alog_ref, dtb_ref, o_ref, *, heads):
    x = ab_ref[0]
    tm = x.shape[0]
    lane = lax.broadcasted_iota(jnp.int32, x.shape, 1)
    g = -jnp.exp(alog_ref[...]) * _softplus(x + dtb_ref[...])
    ii = lax.broadcasted_iota(jnp.int32, (tm, tm), 0)
    jj = lax.broadcasted_iota(jnp.int32, (tm, tm), 1)
    same = (ii // DN_CHUNK) == (jj // DN_CHUNK)
    lower = jnp.where(same & (jj <= ii), 1.0, 0.0).astype(F32)
    upper = jnp.where(same & (jj >= ii), 1.0, 0.0).astype(F32)
    pre = jnp.dot(lower, g, preferred_element_type=F32, precision=lax.Precision.HIGHEST)
    suf = jnp.dot(upper, g, preferred_element_type=F32, precision=lax.Precision.HIGHEST)
    gc = jnp.where(lane < heads, pre, suf)
    o_ref[0] = jnp.where(lane < 2 * heads, gc, jax.nn.sigmoid(x))


def _dn_gate_call(ab, a_log, dt_bias):
    b, l, _ = ab.shape
    tm = min(l, 256)
    pad = lambda t: jnp.pad(t.reshape(1, -1), ((0, 0), (0, LANES - t.size)))
    return pl.pallas_call(
        functools.partial(_dn_gate_kernel, heads=a_log.shape[1]),
        grid=(b, l // tm),
        in_specs=[pl.BlockSpec((1, tm, LANES), lambda bi, i: (bi, i, 0)),
                  pl.BlockSpec((1, LANES), lambda bi, i: (0, 0)),
                  pl.BlockSpec((1, LANES), lambda bi, i: (0, 0))],
        out_specs=pl.BlockSpec((1, tm, LANES), lambda bi, i: (bi, i, 0)),
        out_shape=jax.ShapeDtypeStruct((b, l, LANES), F32),
        compiler_params=_params("parallel", "parallel"),
        name="deltanet_gates",
    )(ab, pad(a_log), pad(dt_bias))


def _dn_gate_layouts(gates, heads):
    b, l, _ = gates.shape
    t = gates[:, :, :4 * heads].reshape(b, l, 2, 2, heads)
    col = jnp.transpose(t, (0, 4, 1, 2, 3)).reshape(b, heads, l, 4)
    row = jnp.transpose(col.reshape(b, heads, l // DN_CHUNK, DN_CHUNK, 4), (0, 1, 2, 4, 3))
    return col, row


def _dn_kernel(ql_ref, kl_ref, vl_ref, gl_ref, qc_ref, kc_ref, vc_ref, gc_ref,
               cwq_ref, cwk_ref, cwv_ref, coll_ref, rowl_ref, colc_ref, rowc_ref, ng_ref,
               ol_ref, oc_ref,
               xpad, qs, ks, vs, oacc, ubuf, wbuf, qebuf, kdt, attnb, egl):
    n_lat, n_ctx = ql_ref.shape[1], qc_ref.shape[1]
    c = DN_CHUNK
    tbs = 256

    def prep(src_ref, n_rows, off, cw_ref, dst, mode):
        def emit(r0, y):
            y = _silu(y)
            if mode != "v":
                y = y * lax.rsqrt(jnp.sum(y * y, axis=-1, keepdims=True) + EPS)
            if mode == "q":
                y = y * (DN_DK ** -0.5)
            dst[off + r0:off + r0 + y.shape[0], :] = y
        _conv_rows(src_ref, n_rows, xpad, cw_ref[...], None, emit, tbs)

    for src_c, src_l, cw_ref, dst, mode in ((qc_ref, ql_ref, cwq_ref, qs, "q"),
                                            (kc_ref, kl_ref, cwk_ref, ks, "k"),
                                            (vc_ref, vl_ref, cwv_ref, vs, "v")):
        prep(src_c, n_ctx, 0, cw_ref, dst, mode)
        prep(src_l, n_lat, n_ctx, cw_ref, dst, mode)

    ii = lax.broadcasted_iota(jnp.int32, (c, c), 0)
    jj = lax.broadcasted_iota(jnp.int32, (c, c), 1)
    eye = jnp.where(ii == jj, 1.0, 0.0).astype(F32)
    n_sq = int(math.log2(c)) - 1

    segments = ((0, n_ctx, colc_ref, rowc_ref), (n_ctx, n_lat, coll_ref, rowl_ref))

    for d in range(2):
        causal = (ii >= jj) if d == 0 else (ii <= jj)
        strict = (ii > jj) if d == 0 else (ii < jj)

        def phase1(ci, off, col_ref, row_ref, d=d, causal=causal, strict=strict):
            lr0 = pl.multiple_of(ci * c, c)
            r0 = pl.multiple_of(off + lr0, c)
            cg = off // c + ci
            q = qs[pl.ds(r0, c), :]
            k = ks[pl.ds(r0, c), :]
            v = vs[pl.ds(r0, c), :]
            gcb = col_ref[0, 0, pl.ds(lr0, c), :]
            gc_col = gcb[:, d:d + 1]
            beta = gcb[:, 2 + d:3 + d]
            gc_row = row_ref[0, 0, ci][d:d + 1, :]
            diff = gc_col - gc_row
            gam = jnp.where(causal, jnp.exp(jnp.where(causal, diff, 0.0)), 0.0)
            kb = k * beta
            kbf = k.astype(BF16)
            m = jnp.where(strict, _dot_nt(kb.astype(BF16), kbf) * gam, 0.0)
            p = -m
            tinv = eye + p
            for _ in range(n_sq):
                pb = p.astype(BF16)
                p = _dot(pb, pb)
                tinv = tinv + _dot(tinv.astype(BF16), p.astype(BF16))
            tb16 = tinv.astype(BF16)
            egc = jnp.exp(gc_col)
            u = _dot(tb16, (v * beta).astype(BF16))
            w = _dot(tb16, (kb * egc).astype(BF16))
            g_last = gc_col[c - 1:c, :] if d == 0 else gc_col[0:1, :]
            kdec = k * jnp.exp(g_last - gc_col)
            attn = _dot_nt(q.astype(BF16), kbf) * gam
            ubuf[pl.ds(r0, c), :] = u
            wbuf[pl.ds(r0, c), :] = w.astype(BF16)
            qebuf[pl.ds(r0, c), :] = (q * egc).astype(BF16)
            kdt[cg] = kdec.T.astype(BF16)
            attnb[pl.ds(r0, c), :] = attn.astype(BF16)
            egl[pl.ds(cg, 1), :] = jnp.broadcast_to(jnp.exp(g_last), (1, LANES))

        for off, n_rows, col_ref, row_ref in segments:
            def p1_body(ci, carry, off=off, col_ref=col_ref, row_ref=row_ref):
                phase1(ci, off, col_ref, row_ref)
                return carry
            lax.fori_loop(0, n_rows // c, p1_body, 0)

        def phase2(ci, state, off, d=d):
            r0 = pl.multiple_of(off + ci * c, c)
            cg = off // c + ci
            sb = state.astype(BF16)
            vnew = ubuf[pl.ds(r0, c), :] - _dot(wbuf[pl.ds(r0, c), :], sb)
            vnb = vnew.astype(BF16)
            o = _dot(qebuf[pl.ds(r0, c), :], sb) + _dot(attnb[pl.ds(r0, c), :], vnb)
            if d == 0:
                oacc[pl.ds(r0, c), :] = o
            else:
                oacc[pl.ds(r0, c), :] += o
            return state * egl[pl.ds(cg, 1), :] + _dot(kdt[cg], vnb)

        state = jnp.zeros((DN_DK, LANES), F32)
        for off, n_rows, _, _ in segments:
            nch = n_rows // c

            def p2_body(i, state, off=off, nch=nch, d=d):
                ci = i if d == 0 else nch - 1 - i
                return phase2(ci, state, off)
            state = lax.fori_loop(0, nch, p2_body, state)

    ng = ng_ref[...]
    for off, n_rows, g_ref, o_ref in ((0, n_ctx, gc_ref, oc_ref), (n_ctx, n_lat, gl_ref, ol_ref)):
        t = min(tbs, n_rows)
        for r0 in range(0, n_rows, t):
            o = oacc[off + r0:off + r0 + t, :]
            y = o * lax.rsqrt(jnp.mean(o * o, axis=-1, keepdims=True) + EPS) * ng
            o_ref[0, r0:r0 + t, :] = (y * _silu(g_ref[0, r0:r0 + t, :])).astype(BF16)


def _dn_call(main_l, main_c, col_l, row_l, col_c, row_c, conv_w, norm_g, col0):
    b, l, _ = main_l.shape
    lc = main_c.shape[1]
    heads = col_l.shape[1]
    lt = l + lc
    nch = lt // DN_CHUNK
    cwt = conv_w.T

    def blk(n_rows, which):
        return pl.BlockSpec((1, n_rows, LANES), lambda bi, h, which=which: (bi, 0, col0 + which * heads + h))

    def cw(which):
        return pl.BlockSpec((SHORT_CONV, LANES), lambda bi, h, which=which: (0, which * heads + h))

    in_specs = ([blk(l, w) for w in range(4)] + [blk(lc, w) for w in range(4)] + [cw(0), cw(1), cw(2)] +
                [pl.BlockSpec((1, 1, l, 4), lambda bi, h: (bi, h, 0, 0)),
                 pl.BlockSpec((1, 1, l // DN_CHUNK, 4, DN_CHUNK), lambda bi, h: (bi, h, 0, 0, 0)),
                 pl.BlockSpec((1, 1, lc, 4), lambda bi, h: (bi, h, 0, 0)),
                 pl.BlockSpec((1, 1, lc // DN_CHUNK, 4, DN_CHUNK), lambda bi, h: (bi, h, 0, 0, 0)),
                 pl.BlockSpec((1, LANES), lambda bi, h: (0, 0))])
    return pl.pallas_call(
        _dn_kernel,
        grid=(b, heads),
        in_specs=in_specs,
        out_specs=[pl.BlockSpec((1, l, LANES), lambda bi, h: (bi, 0, h)),
                   pl.BlockSpec((1, lc, LANES), lambda bi, h: (bi, 0, h))],
        out_shape=[jax.ShapeDtypeStruct((b, l, heads * LANES), BF16),
                   jax.ShapeDtypeStruct((b, lc, heads * LANES), BF16)],
        scratch_shapes=[pltpu.VMEM((l + 2 * SUBLANES, LANES), F32),
                        pltpu.VMEM((lt, LANES), F32),
                        pltpu.VMEM((lt, LANES), F32),
                        pltpu.VMEM((lt, LANES), F32),
                        pltpu.VMEM((lt, LANES), F32),
                        pltpu.VMEM((lt, LANES), F32),
                        pltpu.VMEM((lt, LANES), BF16),
                        pltpu.VMEM((lt, LANES), BF16),
                        pltpu.VMEM((nch, DN_DK, DN_CHUNK), BF16),
                        pltpu.VMEM((lt, DN_CHUNK), BF16),
                        pltpu.VMEM((nch, LANES), F32)],
        compiler_params=_params("parallel", "parallel"),
        name="gated_deltanet",
    )(main_l, main_l, main_l, main_l, main_c, main_c, main_c, main_c, cwt, cwt, cwt,
      col_l, row_l, col_c, row_c, norm_g.reshape(1, LANES))


def _ret_kernel(th_ref, ql_ref, kl_ref, vl_ref, qc_ref, kc_ref, vc_ref, cos_ref, sin_ref, o_ref, s_ref, *, ncc):
    d = pl.program_id(2)
    s = pl.program_id(3)
    c = RET_CHUNK
    half = RET_DK // 2

    @pl.when(s == 0)
    def _():
        s_ref[...] = jnp.zeros_like(s_ref)

    lg = -jnp.exp(th_ref[0, 0])
    lg1 = lg[:, 0:1]
    fwd = d == 0
    icol = lax.broadcasted_iota(jnp.int32, (c, 1), 0)
    fidx = jnp.where(fwd, icol, c - 1 - icol).astype(F32)
    ii = lax.broadcasted_iota(jnp.int32, (c, c), 0)
    jj = lax.broadcasted_iota(jnp.int32, (c, c), 1)
    rel = jnp.where(fwd, ii - jj, jj - ii)
    mask = rel >= 0
    dec = jnp.where(mask, jnp.exp(jnp.where(mask, rel, 0).astype(F32) * lg), 0.0)
    xi = jnp.exp((fidx + 1.0) * lg1)
    zeta = jnp.exp((c - 1.0 - fidx) * lg1)
    gch = jnp.exp(c * lg1)
    scale = RET_DK ** -0.5

    def step(q, k, v, write):
        state = s_ref[...]
        vb = v.astype(BF16)
        if write:
            inner = _dot_nt(q.astype(BF16), k.astype(BF16)) * dec
            o = _dot(inner.astype(BF16), vb) + _dot((q * xi).astype(BF16), state.astype(BF16))
            o_ref[0, 0] = o
        s_ref[...] = state * gch + _dot((k * zeta).T.astype(BF16), vb)

    def rope(t):
        t1, t2 = t[:, 0:half], t[:, half:2 * half]
        cos, sin = cos_ref[...], sin_ref[...]
        return jnp.concatenate([t1 * cos - t2 * sin, t1 * sin + t2 * cos], axis=-1)

    @pl.when(s < ncc)
    def _():
        step(qc_ref[0], kc_ref[0] * scale, vc_ref[0], False)

    @pl.when(s >= ncc)
    def _():
        step(rope(ql_ref[0]), rope(kl_ref[0]) * scale, vl_ref[0], True)


def _ret_call(main_l, main_c, theta, cos, sin, q_col0):
    b, l, _ = main_l.shape
    lc = main_c.shape[1]
    heads = theta.shape[1]
    c = RET_CHUNK
    ncc, nlc = lc // c, l // c
    qb0 = q_col0 // RET_DK
    kb0 = qb0 + heads
    vb0 = (q_col0 + 2 * heads * RET_DK) // RET_DV

    def lat_idx(d, s):
        i = jnp.maximum(s - ncc, 0)
        return jnp.where(d == 0, i, nlc - 1 - i)

    def ctx_idx(d, s):
        i = jnp.minimum(s, ncc - 1)
        return jnp.where(d == 0, i, ncc - 1 - i)

    th = jnp.broadcast_to(theta.reshape(2, heads, 1, 1), (2, heads, 1, LANES))
    return pl.pallas_call(
        functools.partial(_ret_kernel, ncc=ncc),
        grid=(b, heads, 2, ncc + nlc),
        in_specs=[pl.BlockSpec((1, 1, 1, LANES), lambda bi, h, d, s: (d, h, 0, 0)),
                  pl.BlockSpec((1, c, RET_DK), lambda bi, h, d, s: (bi, lat_idx(d, s), qb0 + h)),
                  pl.BlockSpec((1, c, RET_DK), lambda bi, h, d, s: (bi, lat_idx(d, s), kb0 + h)),
                  pl.BlockSpec((1, c, RET_DV), lambda bi, h, d, s: (bi, lat_idx(d, s), vb0 + h)),
                  pl.BlockSpec((1, c, RET_DK), lambda bi, h, d, s: (bi, ctx_idx(d, s), qb0 + h)),
                  pl.BlockSpec((1, c, RET_DK), lambda bi, h, d, s: (bi, ctx_idx(d, s), kb0 + h)),
                  pl.BlockSpec((1, c, RET_DV), lambda bi, h, d, s: (bi, ctx_idx(d, s), vb0 + h)),
                  pl.BlockSpec((c, RET_DK // 2), lambda bi, h, d, s: (lat_idx(d, s), 0)),
                  pl.BlockSpec((c, RET_DK // 2), lambda bi, h, d, s: (lat_idx(d, s), 0))],
        out_specs=pl.BlockSpec((1, 1, c, RET_DV), lambda bi, h, d, s: (d, bi, lat_idx(d, s), h)),
        out_shape=jax.ShapeDtypeStruct((2, b, l, heads * RET_DV), F32),
        scratch_shapes=[pltpu.VMEM((RET_DK, RET_DV), F32)],
        compiler_params=_params("parallel", "parallel", "arbitrary", "arbitrary"),
        name="retention",
    )(th, main_l, main_l, main_l, main_c, main_c, main_c, cos, sin)


def _ret_finish_kernel(o0_ref, o1_ref, r_ref, y_ref):
    o = o0_ref[0, 0] + o1_ref[0, 0]
    y = o * lax.rsqrt(jnp.mean(o * o, axis=-1, keepdims=True) + EPS)
    y_ref[0] = (y * _silu(r_ref[0])).astype(BF16)


def _ret_finish_call(o, main_l, r_col0):
    _, b, l, w = o.shape
    heads = w // RET_DV
    tm = min(l, 1024)
    rb0 = r_col0 // RET_DV
    return pl.pallas_call(
        _ret_finish_kernel,
        grid=(b, l // tm, heads),
        in_specs=[pl.BlockSpec((1, 1, tm, RET_DV), lambda bi, i, h: (0, bi, i, h)),
                  pl.BlockSpec((1, 1, tm, RET_DV), lambda bi, i, h: (1, bi, i, h)),
                  pl.BlockSpec((1, tm, RET_DV), lambda bi, i, h: (bi, i, rb0 + h))],
        out_specs=pl.BlockSpec((1, tm, RET_DV), lambda bi, i, h: (bi, i, h)),
        out_shape=jax.ShapeDtypeStruct((b, l, w), BF16),
        compiler_params=_params("parallel", "parallel", "parallel"),
        name="retention_finish",
    )(o, o, main_l)


def _rope_tables(n_tokens):
    rows = n_tokens // GRID_W
    r, col = jnp.meshgrid(jnp.arange(rows), jnp.arange(GRID_W), indexing='ij')
    n_freq = RET_DK // 4
    inv = ROPE_BASE ** (-jnp.arange(n_freq, dtype=F32) / n_freq)
    ang = jnp.concatenate([r.reshape(-1, 1) * inv, col.reshape(-1, 1) * inv], axis=-1)
    return jnp.cos(ang), jnp.sin(ang)


def _even_layer(xl, xc, mods_l, mods_c, norm1_g, norm2_g, w_in, w_out, s5p, glu_w, glu_b,
                dn_conv_w, dn_a_log, dn_dt_bias, dn_norm_g, ffn, final_g, final_norm):
    s5_width = glu_w.shape[0]
    dn_width = dn_conv_w.shape[0] // 3
    heads = dn_width // LANES
    n_main = s5_width + 4 * dn_width
    w_main = w_in[:, :n_main].astype(BF16)
    w_small = jnp.pad(w_in[:, n_main:], ((0, 0), (0, LANES - (w_in.shape[1] - n_main)))).astype(BF16)

    main_l, ab_l = _inproj_call(xl, norm1_g, mods_l[0], mods_l[1], w_main, w_small)
    main_c, ab_c = _inproj_call(xc, norm1_g, mods_c[0], mods_c[1], w_main, w_small)

    abar, wd, wo, d_skip = s5p
    y_l, y_c = _s5_scan_call(main_l, main_c, abar, wd, wo, d_skip)
    glu_wb = glu_w.astype(BF16)
    s5_l = _s5_glu_call(y_l, glu_wb, glu_b)
    s5_c = _s5_glu_call(y_c, glu_wb, glu_b)

    col_l, row_l = _dn_gate_layouts(_dn_gate_call(ab_l, dn_a_log, dn_dt_bias), heads)
    col_c, row_c = _dn_gate_layouts(_dn_gate_call(ab_c, dn_a_log, dn_dt_bias), heads)
    dn_l, dn_c = _dn_call(main_l, main_c, col_l, row_l, col_c, row_c, dn_conv_w, dn_norm_g,
                          s5_width // LANES)

    w_o1 = w_out[:s5_width].astype(BF16)
    w_o2 = w_out[s5_width:].astype(BF16)
    xl = _outproj_call(xl, mods_l[2], s5_l, dn_l, w_o1, w_o2)
    xc = _outproj_call(xc, mods_c[2], s5_c, dn_c, w_o1, w_o2)
    w1, w3, w2 = ffn
    xl = _ffn_call(xl, norm2_g, mods_l[3], mods_l[4], mods_l[5], w1, w3, w2, final_g, final_norm)
    xc = _ffn_call(xc, norm2_g, mods_c[3], mods_c[4], mods_c[5], w1, w3, w2, final_g, False)
    return xl, xc


def _odd_layer(xl, xc, mods_l, mods_c, norm1_g, norm2_g, w_in, w_out, conv_w, conv_b, wa, ba, wx, bx, lam,
               theta, cos, sin, ffn, final_g, final_norm):
    lru_width = conv_b.shape[0]
    heads = theta.shape[1]
    w_inb = w_in.astype(BF16)
    main_l = _inproj_call(xl, norm1_g, mods_l[0], mods_l[1], w_inb)
    main_c = _inproj_call(xc, norm1_g, mods_c[0], mods_c[1], w_inb)

    lru_l = _lru_call(main_l, main_c, conv_w, conv_b, wa.astype(BF16), ba, wx.astype(BF16), bx, lam)
    q_col0 = 2 * lru_width
    o = _ret_call(main_l, main_c, theta, cos, sin, q_col0)
    ret_l = _ret_finish_call(o, main_l, q_col0 + 2 * heads * RET_DK + heads * RET_DV)

    w_o1 = w_out[:lru_width].astype(BF16)
    w_o2 = w_out[lru_width:].astype(BF16)
    xl = _outproj_call(xl, mods_l[2], lru_l, ret_l, w_o1, w_o2)
    w1, w3, w2 = ffn
    return _ffn_call(xl, norm2_g, mods_l[3], mods_l[4], mods_l[5], w1, w3, w2, final_g, final_norm)


def kernel(x, c, ctx, c_ctx, mod_w, mod_b, norm1_g, norm2_g, ffn_w1, ffn_w3, ffn_w2, final_g, even_w_in, even_w_out, s5_lam_re, s5_lam_im, s5_log_step, s5_b_re, s5_b_im, s5_c_re, s5_c_im, s5_d, s5_glu_w, s5_glu_b, dn_conv_w, dn_a_log, dn_dt_bias, dn_norm_g, odd_w_in, odd_w_out, lru_conv_w, lru_conv_b, lru_wa, lru_ba, lru_wx, lru_bx, lru_lam, ret_theta):
    bsz, n_tok, d = x.shape
    depth = mod_w.shape[0]
    assert depth == 2 and bsz + 1 <= SUBLANES
    cos, sin = _rope_tables(n_tok)

    rows = jnp.concatenate([c, c_ctx[None, :], jnp.zeros((SUBLANES - bsz - 1, d), F32)], axis=0)
    mods = _mod_call(rows, mod_w, mod_b)

    def split_mods(i):
        m = mods[i].reshape(SUBLANES, 6, d)
        ml = [m[:bsz, k][:, None, :] for k in range(6)]
        mc = [jnp.broadcast_to(m[bsz, k][None, None, :], (bsz, 1, d)) for k in range(6)]
        return ml, mc

    xl, xc = x, ctx
    ml, mc = split_mods(0)
    abar, wd, wo = _s5_weights(s5_lam_re[0], s5_lam_im[0], s5_log_step[0], s5_b_re[0], s5_b_im[0],
                               s5_c_re[0], s5_c_im[0])
    ffn0 = (ffn_w1[0].astype(BF16), ffn_w3[0].astype(BF16), ffn_w2[0].astype(BF16))
    xl, xc = _even_layer(xl, xc, ml, mc, norm1_g[0], norm2_g[0], even_w_in[0], even_w_out[0],
                         (abar, wd, wo, s5_d[0]), s5_glu_w[0], s5_glu_b[0],
                         dn_conv_w[0], dn_a_log[0], dn_dt_bias[0], dn_norm_g[0], ffn0, final_g, False)
    ml, mc = split_mods(1)
    ffn1 = (ffn_w1[1].astype(BF16), ffn_w3[1].astype(BF16), ffn_w2[1].astype(BF16))
    return _odd_layer(xl, xc, ml, mc, norm1_g[1], norm2_g[1], odd_w_in[0], odd_w_out[0],
                      lru_conv_w[0], lru_conv_b[0], lru_wa[0], lru_ba[0], lru_wx[0], lru_bx[0], lru_lam[0],
                      ret_theta[0], cos, sin, ffn1, final_g, True)
```

```python
import functools
import math

import jax
import jax.numpy as jnp
from jax import lax
from jax.experimental import pallas as pl
from jax.experimental.pallas import tpu as pltpu

F32 = jnp.float32
BF16 = jnp.bfloat16

EPS = 1e-6
GRID_W = 64
ROPE_BASE = 10000.0

S5_GROUP = 16
S5_STATE = 64
S5_SLAB_GROUPS = 8
S5_SLAB_STATES = S5_SLAB_GROUPS * S5_STATE

DN_DK = 128
DN_CHUNK = 64
SHORT_CONV = 4

LRU_BLOCK = 128
LRU_C = 8.0

RET_DK = 256
RET_DV = 512
RET_CHUNK = 128

LANES = 128
SUBLANES = 8
VMEM_LIMIT = 56 * 1024 * 1024


def _params(*sem):
    return pltpu.CompilerParams(dimension_semantics=sem, vmem_limit_bytes=VMEM_LIMIT)


def _silu(x):
    return x * jax.nn.sigmoid(x)


def _gelu(x):
    return 0.5 * x * (1.0 + jnp.tanh(0.7978845608028654 * (x + 0.044715 * (x * x * x))))


def _softplus(x):
    return jnp.maximum(x, 0.0) + jnp.log1p(jnp.exp(-jnp.abs(x)))


def _expm1(x):
    u = jnp.exp(x)
    safe = jnp.where(u == 1.0, 2.0, u)
    return jnp.where(u == 1.0, x, (u - 1.0) * x / jnp.where(u == 0.0, x, jnp.log(safe)))


def _adaln(x, g, shift, scale):
    y = x * lax.rsqrt(jnp.mean(x * x, axis=-1, keepdims=True) + EPS)
    return (y * g) * (1.0 + scale) + shift


def _dot(a, b):
    return jnp.dot(a, b, preferred_element_type=F32)


def _dot_nt(a, b):
    return lax.dot_general(a, b, (((1,), (1,)), ((), ())), preferred_element_type=F32)


def _mod_kernel(s_ref, w_ref, b_ref, o_ref):
    s = _silu(s_ref[...])
    o_ref[0] = _dot(s.astype(BF16), w_ref[0].astype(BF16)) + b_ref[0]


def _mod_call(rows, mod_w, mod_b):
    depth, d, n = mod_w.shape
    tn = 1024
    return pl.pallas_call(
        _mod_kernel,
        grid=(depth, n // tn),
        in_specs=[pl.BlockSpec((SUBLANES, d), lambda i, j: (0, 0)),
                  pl.BlockSpec((1, d, tn), lambda i, j: (i, 0, j)),
                  pl.BlockSpec((1, 1, tn), lambda i, j: (i, 0, j))],
        out_specs=pl.BlockSpec((1, SUBLANES, tn), lambda i, j: (i, 0, j)),
        out_shape=jax.ShapeDtypeStruct((depth, SUBLANES, n), F32),
        compiler_params=_params("parallel", "parallel"),
        name="mod_proj",
    )(rows, mod_w, mod_b.reshape(depth, 1, n))


def _inproj_kernel(x_ref, g_ref, sh_ref, sc_ref, w_ref, *rest, has_small):
    if has_small:
        ws_ref, o_ref, os_ref, h_ref = rest
    else:
        o_ref, h_ref = rest

    @pl.when(pl.program_id(2) == 0)
    def _():
        h = _adaln(x_ref[0], g_ref[...], sh_ref[0], sc_ref[0]).astype(BF16)
        h_ref[...] = h
        if has_small:
            os_ref[0] = _dot(h, ws_ref[...])

    o_ref[0] = _dot(h_ref[...], w_ref[...]).astype(o_ref.dtype)


def _inproj_call(x, g, shift, scale, w, w_small=None):
    b, l, d = x.shape
    n = w.shape[1]
    tm = min(l, 1024)
    tn = 512
    has_small = w_small is not None
    in_specs = [pl.BlockSpec((1, tm, d), lambda bi, i, j: (bi, i, 0)),
                pl.BlockSpec((1, d), lambda bi, i, j: (0, 0)),
                pl.BlockSpec((1, 1, d), lambda bi, i, j: (bi, 0, 0)),
                pl.BlockSpec((1, 1, d), lambda bi, i, j: (bi, 0, 0)),
                pl.BlockSpec((d, tn), lambda bi, i, j: (0, j))]
    out_specs = [pl.BlockSpec((1, tm, tn), lambda bi, i, j: (bi, i, j))]
    out_shape = [jax.ShapeDtypeStruct((b, l, n), BF16)]
    args = [x, g.reshape(1, d), shift, scale, w]
    if has_small:
        in_specs.append(pl.BlockSpec((d, LANES), lambda bi, i, j: (0, 0)))
        out_specs.append(pl.BlockSpec((1, tm, LANES), lambda bi, i, j: (bi, i, 0)))
        out_shape.append(jax.ShapeDtypeStruct((b, l, LANES), F32))
        args.append(w_small)
    outs = pl.pallas_call(
        functools.partial(_inproj_kernel, has_small=has_small),
        grid=(b, l // tm, n // tn),
        in_specs=in_specs, out_specs=out_specs, out_shape=out_shape,
        scratch_shapes=[pltpu.VMEM((tm, d), BF16)],
        compiler_params=_params("parallel", "parallel", "arbitrary"),
        name="adaln_inproj",
    )(*args)
    return outs if has_small else outs[0]


def _outproj_kernel(x_ref, gate_ref, a1_ref, a2_ref, w1_ref, w2_ref, o_ref):
    y = _dot(a1_ref[0], w1_ref[...]) + _dot(a2_ref[0], w2_ref[...])
    o_ref[0] = x_ref[0] + gate_ref[0] * y


def _outproj_call(x, gate, a1, a2, w1, w2):
    b, l, d = x.shape
    k1, k2 = a1.shape[2], a2.shape[2]
    tm = min(l, 1024)
    tn = 512
    return pl.pallas_call(
        _outproj_kernel,
        grid=(b, l // tm, d // tn),
        in_specs=[pl.BlockSpec((1, tm, tn), lambda bi, i, j: (bi, i, j)),
                  pl.BlockSpec((1, 1, tn), lambda bi, i, j: (bi, 0, j)),
                  pl.BlockSpec((1, tm, k1), lambda bi, i, j: (bi, i, 0)),
                  pl.BlockSpec((1, tm, k2), lambda bi, i, j: (bi, i, 0)),
                  pl.BlockSpec((k1, tn), lambda bi, i, j: (0, j)),
                  pl.BlockSpec((k2, tn), lambda bi, i, j: (0, j))],
        out_specs=pl.BlockSpec((1, tm, tn), lambda bi, i, j: (bi, i, j)),
        out_shape=jax.ShapeDtypeStruct((b, l, d), F32),
        compiler_params=_params("parallel", "parallel", "arbitrary"),
        name="outproj_residual",
    )(x, gate, a1, a2, w1, w2)


def _ffn_kernel(x_ref, g_ref, sh_ref, sc_ref, gate_ref, w1_ref, w3_ref, w2_ref, fg_ref, o_ref,
                h_ref, acc_ref, *, final_norm):
    f = pl.program_id(2)

    @pl.when(f == 0)
    def _():
        h_ref[...] = _adaln(x_ref[0], g_ref[...], sh_ref[0], sc_ref[0]).astype(BF16)
        acc_ref[...] = jnp.zeros_like(acc_ref)

    h = h_ref[...]
    a = _dot(h, w1_ref[...])
    b = _dot(h, w3_ref[...])
    acc_ref[...] += _dot((_silu(a) * b).astype(BF16), w2_ref[...])

    @pl.when(f == pl.num_programs(2) - 1)
    def _():
        y = x_ref[0] + gate_ref[0] * acc_ref[...]
        if final_norm:
            y = y * lax.rsqrt(jnp.mean(y * y, axis=-1, keepdims=True) + EPS) * fg_ref[...]
        o_ref[0] = y


def _ffn_call(x, g, shift, scale, gate, w1, w3, w2, final_g, final_norm):
    b, l, d = x.shape
    dff = w1.shape[1]
    tm = min(l, 512)
    tf = 512
    vec = pl.BlockSpec((1, 1, d), lambda bi, i, f: (bi, 0, 0))
    return pl.pallas_call(
        functools.partial(_ffn_kernel, final_norm=final_norm),
        grid=(b, l // tm, dff // tf),
        in_specs=[pl.BlockSpec((1, tm, d), lambda bi, i, f: (bi, i, 0)),
                  pl.BlockSpec((1, d), lambda bi, i, f: (0, 0)),
                  vec, vec, vec,
                  pl.BlockSpec((d, tf), lambda bi, i, f: (0, f)),
                  pl.BlockSpec((d, tf), lambda bi, i, f: (0, f)),
                  pl.BlockSpec((tf, d), lambda bi, i, f: (f, 0)),
                  pl.BlockSpec((1, d), lambda bi, i, f: (0, 0))],
        out_specs=pl.BlockSpec((1, tm, d), lambda bi, i, f: (bi, i, 0)),
        out_shape=jax.ShapeDtypeStruct((b, l, d), F32),
        scratch_shapes=[pltpu.VMEM((tm, d), BF16), pltpu.VMEM((tm, d), F32)],
        compiler_params=_params("parallel", "parallel", "arbitrary"),
        name="ffn_swiglu",
    )(x, g.reshape(1, d), shift, scale, gate, w1, w3, w2, final_g.reshape(1, d))


def _s5_disc_kernel(lre_ref, lim_ref, ls_ref, bre_ref, bim_ref, are_ref, aim_ref, bbre_ref, bbim_ref):
    lre = jnp.minimum(lre_ref[...], -1e-4)
    lim = lim_ref[...]
    dt = jnp.exp(ls_ref[...])
    mag = jnp.exp(lre * dt)
    ar = mag * jnp.cos(lim * dt)
    ai = mag * jnp.sin(lim * dt)
    nr, ni = ar - 1.0, ai
    den = lre * lre + lim * lim
    cr = (nr * lre + ni * lim) / den
    ci = (ni * lre - nr * lim) / den
    bre, bim = bre_ref[...], bim_ref[...]
    are_ref[...] = ar
    aim_ref[...] = ai
    bbre_ref[...] = cr * bre - ci * bim
    bbim_ref[...] = cr * bim + ci * bre


def _s5_weights(lam_re, lam_im, log_step, b_re, b_im, c_re, c_im):
    nd, g, p = lam_re.shape
    s = b_re.shape[-1]
    rows = nd * g * s
    rep = lambda t: jnp.repeat(t.reshape(nd * g, p), s, axis=0)
    ls = jnp.broadcast_to(log_step.reshape(nd * g, 1), (nd * g, p))
    tb = lambda t: jnp.transpose(t, (0, 1, 3, 2)).reshape(rows, p)
    shp = jax.ShapeDtypeStruct((rows, p), F32)
    are, aim, bbre, bbim = pl.pallas_call(
        _s5_disc_kernel, out_shape=[shp, shp, shp, shp], name="s5_discretise",
    )(rep(lam_re), rep(lam_im), rep(ls), tb(b_re), tb(b_im))
    nslab = g // S5_SLAB_GROUPS
    eye = jnp.eye(S5_SLAB_GROUPS, dtype=F32)
    abar = jnp.concatenate([are[::s].reshape(nd, nslab, 1, S5_SLAB_STATES),
                            aim[::s].reshape(nd, nslab, 1, S5_SLAB_STATES)], axis=-1)

    def drive(t):
        t = t.reshape(nd, nslab, S5_SLAB_GROUPS, s, p)
        return jnp.einsum('dagsp,gh->dagshp', t, eye).reshape(nd, nslab, S5_SLAB_GROUPS * s, S5_SLAB_STATES)

    def readout(t):
        t = t.reshape(nd, nslab, S5_SLAB_GROUPS, s, p)
        return jnp.einsum('dagsp,gh->dagphs', t, eye).reshape(nd, nslab, S5_SLAB_STATES, S5_SLAB_GROUPS * s)

    wd = jnp.concatenate([drive(bbre), drive(bbim)], axis=-1).astype(BF16)
    wo = jnp.concatenate([readout(c_re), readout(-c_im)], axis=-2).astype(BF16)
    return abar, wd, wo


def _s5_kernel(ul_ref, uc_ref, wd_ref, wo_ref, a_ref, dsk_ref, yl_ref, yc_ref, hbuf, tab, *, tb):
    n_lat, n_ctx = ul_ref.shape[1], uc_ref.shape[1]
    hs = S5_SLAB_STATES
    row = lax.broadcasted_iota(jnp.int32, (SUBLANES, hs), 0)

    for d in range(2):
        a1r = jnp.broadcast_to(a_ref[d, 0, :, 0:hs], (SUBLANES, hs))
        a1i = jnp.broadcast_to(a_ref[d, 0, :, hs:2 * hs], (SUBLANES, hs))
        pw = [(a1r, a1i)]
        for _ in range(SUBLANES - 1):
            pr, pi = pw[-1]
            pw.append((pr * a1r - pi * a1i, pr * a1i + pi * a1r))
        for idx, k in enumerate((1, 2, 4)):
            keep = (row >= k) if d == 0 else (row <= SUBLANES - 1 - k)
            tab[d, 2 * idx] = jnp.where(keep, pw[k - 1][0], 0.0)
            tab[d, 2 * idx + 1] = jnp.where(keep, pw[k - 1][1], 0.0)
        cr = jnp.zeros((SUBLANES, hs), F32)
        ci = jnp.zeros((SUBLANES, hs), F32)
        for r in range(SUBLANES):
            e = r + 1 if d == 0 else SUBLANES - r
            cr = jnp.where(row == r, pw[e - 1][0], cr)
            ci = jnp.where(row == r, pw[e - 1][1], ci)
        tab[d, 6] = cr
        tab[d, 7] = ci

    def tile_body(d, ntiles, t, carry):
        cr, ci = carry
        ti = t if d == 0 else ntiles - 1 - t
        r0 = pl.multiple_of(ti * SUBLANES, SUBLANES)
        xr = hbuf[pl.ds(r0, SUBLANES), 0:hs]
        xi = hbuf[pl.ds(r0, SUBLANES), hs:2 * hs]
        for idx, k in enumerate((1, 2, 4)):
            sh = k if d == 0 else SUBLANES - k
            sr = pltpu.roll(xr, sh, 0)
            si = pltpu.roll(xi, sh, 0)
            akr, aki = tab[d, 2 * idx], tab[d, 2 * idx + 1]
            xr, xi = xr + akr * sr - aki * si, xi + akr * si + aki * sr
        pr, pi = tab[d, 6], tab[d, 7]
        hr = xr + pr * cr - pi * ci
        hi = xi + pr * ci + pi * cr
        hbuf[pl.ds(r0, SUBLANES), 0:hs] = hr
        hbuf[pl.ds(r0, SUBLANES), hs:2 * hs] = hi
        last = SUBLANES - 1 if d == 0 else 0
        return (jnp.broadcast_to(hr[last:last + 1], (SUBLANES, hs)),
                jnp.broadcast_to(hi[last:last + 1], (SUBLANES, hs)))

    def run_segment(d, u_ref, y_ref, n_rows, carry):
        tbs = min(tb, n_rows)
        nblk = n_rows // tbs

        def blk_body(i, carry):
            bi = i if d == 0 else nblk - 1 - i
            r0 = pl.multiple_of(bi * tbs, tbs)
            ub = u_ref[0, pl.ds(r0, tbs), :]
            hbuf[0:tbs, :] = _dot(ub, wd_ref[d, 0])
            carry = lax.fori_loop(0, tbs // SUBLANES,
                                  functools.partial(tile_body, d, tbs // SUBLANES), carry)
            yb = _dot(hbuf[0:tbs, :].astype(BF16), wo_ref[d, 0])
            if d == 0:
                y_ref[0, pl.ds(r0, tbs), :] = dsk_ref[...] * ub.astype(F32) + yb
            else:
                y_ref[0, pl.ds(r0, tbs), :] += yb
            return carry

        return lax.fori_loop(0, nblk, blk_body, carry)

    for d in range(2):
        carry = (jnp.zeros((SUBLANES, hs), F32), jnp.zeros((SUBLANES, hs), F32))
        carry = run_segment(d, uc_ref, yc_ref, n_ctx, carry)
        run_segment(d, ul_ref, yl_ref, n_lat, carry)


def _s5_scan_call(main_l, main_c, abar, wd, wo, d_skip):
    b, l, _ = main_l.shape
    lc = main_c.shape[1]
    width = d_skip.shape[0]
    nslab = width // LANES
    tb = 256
    return pl.pallas_call(
        functools.partial(_s5_kernel, tb=tb),
        grid=(b, nslab),
        in_specs=[pl.BlockSpec((1, l, LANES), lambda bi, s: (bi, 0, s)),
                  pl.BlockSpec((1, lc, LANES), lambda bi, s: (bi, 0, s)),
                  pl.BlockSpec((2, 1, LANES, 2 * S5_SLAB_STATES), lambda bi, s: (0, s, 0, 0)),
                  pl.BlockSpec((2, 1, 2 * S5_SLAB_STATES, LANES), lambda bi, s: (0, s, 0, 0)),
                  pl.BlockSpec((2, 1, 1, 2 * S5_SLAB_STATES), lambda bi, s: (0, s, 0, 0)),
                  pl.BlockSpec((1, LANES), lambda bi, s: (0, s))],
        out_specs=[pl.BlockSpec((1, l, LANES), lambda bi, s: (bi, 0, s)),
                   pl.BlockSpec((1, lc, LANES), lambda bi, s: (bi, 0, s))],
        out_shape=[jax.ShapeDtypeStruct((b, l, width), F32),
                   jax.ShapeDtypeStruct((b, lc, width), F32)],
        scratch_shapes=[pltpu.VMEM((tb, 2 * S5_SLAB_STATES), F32),
                        pltpu.VMEM((2, 8, SUBLANES, S5_SLAB_STATES), F32)],
        compiler_params=_params("parallel", "parallel"),
        name="s5_scan",
    )(main_l, main_c, wd, wo, abar, d_skip.reshape(1, width))


def _s5_glu_kernel(y_ref, w_ref, b_ref, o_ref):
    g = _gelu(y_ref[0])
    o_ref[0] = (g * jax.nn.sigmoid(_dot(g.astype(BF16), w_ref[...]) + b_ref[...])).astype(BF16)


def _s5_glu_call(y, glu_w, glu_b):
    b, l, w = y.shape
    tm = min(l, 1024)
    return pl.pallas_call(
        _s5_glu_kernel,
        grid=(b, l // tm),
        in_specs=[pl.BlockSpec((1, tm, w), lambda bi, i: (bi, i, 0)),
                  pl.BlockSpec((w, w), lambda bi, i: (0, 0)),
                  pl.BlockSpec((1, w), lambda bi, i: (0, 0))],
        out_specs=pl.BlockSpec((1, tm, w), lambda bi, i: (bi, i, 0)),
        out_shape=jax.ShapeDtypeStruct((b, l, w), BF16),
        compiler_params=_params("parallel", "parallel"),
        name="s5_glu",
    )(y, glu_w, glu_b.reshape(1, w))


def _conv_rows(src_ref, n_rows, xpad, cw, bias, emit, tbs):
    zeros = jnp.zeros((SUBLANES, LANES), F32)
    xpad[0:SUBLANES, :] = zeros
    xpad[SUBLANES + n_rows:2 * SUBLANES + n_rows, :] = zeros
    tbs = min(tbs, n_rows)
    for r0 in range(0, n_rows, tbs):
        xpad[SUBLANES + r0:SUBLANES + r0 + tbs, :] = src_ref[0, r0:r0 + tbs, :].astype(F32)
    for r0 in range(0, n_rows, tbs):
        acc = None
        for j in range(SHORT_CONV):
            off = SUBLANES + r0 + j - SHORT_CONV // 2
            term = cw[j:j + 1, :] * xpad[off:off + tbs, :]
            acc = term if acc is None else acc + term
        if bias is not None:
            acc = acc + bias
        emit(r0, acc)


def _lru_kernel(xl_ref, xc_ref, gl_ref, cw_ref, cb_ref, wa_ref, wx_ref, ba_ref, bx_ref, lam_ref, o_ref,
                xpad, xconv, abuf, bbuf, hsum, *, tb):
    n_lat, n_ctx = xl_ref.shape[1], xc_ref.shape[1]
    cw = cw_ref[...]
    cb = cb_ref[...]

    def put(off):
        def emit(r0, y):
            xconv[off + r0:off + r0 + y.shape[0], :] = y
        return emit

    _conv_rows(xc_ref, n_ctx, xpad, cw, cb, put(0), tb)
    _conv_rows(xl_ref, n_lat, xpad, cw, cb, put(n_ctx), tb)

    row = lax.broadcasted_iota(jnp.int32, (SUBLANES, LANES), 0)

    def tile_body(d, ntiles, t, carry):
        ti = t if d == 0 else ntiles - 1 - t
        r0 = pl.multiple_of(ti * SUBLANES, SUBLANES)
        at = abuf[pl.ds(r0, SUBLANES), :]
        bt = bbuf[pl.ds(r0, SUBLANES), :]
        for k in (1, 2, 4):
            sh = k if d == 0 else SUBLANES - k
            keep = (row >= k) if d == 0 else (row <= SUBLANES - 1 - k)
            sa = jnp.where(keep, pltpu.roll(at, sh, 0), 1.0)
            sb = jnp.where(keep, pltpu.roll(bt, sh, 0), 0.0)
            bt = bt + at * sb
            at = at * sa
        h = bt + at * carry
        bbuf[pl.ds(r0, SUBLANES), :] = h
        last = SUBLANES - 1 if d == 0 else 0
        return jnp.broadcast_to(h[last:last + 1], (SUBLANES, LANES))

    for d in range(2):
        sp = _softplus(-lam_ref[d])
        wa, wx = wa_ref[d, 0], wx_ref[d, 0]
        ba, bx = ba_ref[d], bx_ref[d]

        def run_segment(off, n_rows, is_lat, carry, d=d, sp=sp, wa=wa, wx=wx, ba=ba, bx=bx):
            tbs = min(tb, n_rows)
            nblk = n_rows // tbs

            def blk_body(i, carry):
                bi = i if d == 0 else nblk - 1 - i
                lr0 = pl.multiple_of(bi * tbs, tbs)
                xc = xconv[pl.ds(off + lr0, tbs), :]
                xb = xc.astype(BF16)
                r = jax.nn.sigmoid(_dot(xb, wa) + ba)
                ig = jax.nn.sigmoid(_dot(xb, wx) + bx)
                log_a = -LRU_C * r * sp
                abuf[0:tbs, :] = jnp.exp(log_a)
                bbuf[0:tbs, :] = jnp.sqrt(jnp.maximum(-_expm1(2.0 * log_a), 0.0)) * (ig * xc)
                carry = lax.fori_loop(0, tbs // SUBLANES,
                                      functools.partial(tile_body, d, tbs // SUBLANES), carry)
                if is_lat:
                    if d == 0:
                        hsum[pl.ds(lr0, tbs), :] = bbuf[0:tbs, :]
                    else:
                        y = (hsum[pl.ds(lr0, tbs), :] + bbuf[0:tbs, :]) * _gelu(gl_ref[0, pl.ds(lr0, tbs), :].astype(F32))
                        o_ref[0, pl.ds(lr0, tbs), :] = y.astype(BF16)
                return carry

            return lax.fori_loop(0, nblk, blk_body, carry)

        carry = run_segment(0, n_ctx, False, jnp.zeros((SUBLANES, LANES), F32))
        run_segment(n_ctx, n_lat, True, carry)


def _lru_call(main_l, main_c, conv_w, conv_b, wa, ba, wx, bx, lam):
    b, l, _ = main_l.shape
    lc = main_c.shape[1]
    width = conv_b.shape[0]
    nb = width // LRU_BLOCK
    tb = 256
    vec = pl.BlockSpec((2, 1, LANES), lambda bi, n: (0, 0, n))
    wspec = pl.BlockSpec((2, 1, LRU_BLOCK, LRU_BLOCK), lambda bi, n: (0, n, 0, 0))
    return pl.pallas_call(
        functools.partial(_lru_kernel, tb=tb),
        grid=(b, nb),
        in_specs=[pl.BlockSpec((1, l, LANES), lambda bi, n: (bi, 0, n)),
                  pl.BlockSpec((1, lc, LANES), lambda bi, n: (bi, 0, n)),
                  pl.BlockSpec((1, l, LANES), lambda bi, n, nb=nb: (bi, 0, nb + n)),
                  pl.BlockSpec((SHORT_CONV, LANES), lambda bi, n: (0, n)),
                  pl.BlockSpec((1, LANES), lambda bi, n: (0, n)),
                  wspec, wspec, vec, vec, vec],
        out_specs=pl.BlockSpec((1, l, LANES), lambda bi, n: (bi, 0, n)),
        out_shape=jax.ShapeDtypeStruct((b, l, width), BF16),
        scratch_shapes=[pltpu.VMEM((l + 2 * SUBLANES, LANES), F32),
                        pltpu.VMEM((lc + l, LANES), F32),
                        pltpu.VMEM((tb, LANES), F32),
                        pltpu.VMEM((tb, LANES), F32),
                        pltpu.VMEM((l, LANES), F32)],
        compiler_params=_params("parallel", "parallel"),
        name="rglru",
    )(main_l, main_c, main_l, conv_w.T, conv_b.reshape(1, width), wa, wx,
      ba.reshape(2, 1, width), bx.reshape(2, 1, width), lam.reshape(2, 1, width))


def _dn_gate_kernel(ab_ref, alog_ref, dtb_ref, o_ref, *, heads):
    x = ab_ref[0]
    tm = x.shape[0]
    lane = lax.broadcasted_iota(jnp.int32, x.shape, 1)
    g = -jnp.exp(alog_ref[...]) * _softplus(x + dtb_ref[...])
    ii = lax.broadcasted_iota(jnp.int32, (tm, tm), 0)
    jj = lax.broadcasted_iota(jnp.int32, (tm, tm), 1)
    same = (ii // DN_CHUNK) == (jj // DN_CHUNK)
    lower = jnp.where(same & (jj <= ii), 1.0, 0.0).astype(F32)
    upper = jnp.where(same & (jj >= ii), 1.0, 0.0).astype(F32)
    pre = jnp.dot(lower, g, preferred_element_type=F32, precision=lax.Precision.HIGHEST)
    suf = jnp.dot(upper, g, preferred_element_type=F32, precision=lax.Precision.HIGHEST)
    gc = jnp.where(lane < heads, pre, suf)
    o_ref[0] = jnp.where(lane < 2 * heads, gc, jax.nn.sigmoid(x))


def _dn_gate_call(ab, a_log, dt_bias):
    b, l, _ = ab.shape
    tm = min(l, 256)
    pad = lambda t: jnp.pad(t.reshape(1, -1), ((0, 0), (0, LANES - t.size)))
    return pl.pallas_call(
        functools.partial(_dn_gate_kernel, heads=a_log.shape[1]),
        grid=(b, l // tm),
        in_specs=[pl.BlockSpec((1, tm, LANES), lambda bi, i: (bi, i, 0)),
                  pl.BlockSpec((1, LANES), lambda bi, i: (0, 0)),
                  pl.BlockSpec((1, LANES), lambda bi, i: (0, 0))],
        out_specs=pl.BlockSpec((1, tm, LANES), lambda bi, i: (bi, i, 0)),
        out_shape=jax.ShapeDtypeStruct((b, l, LANES), F32),
        compiler_params=_params("parallel", "parallel"),
        name="deltanet_gates",
    )(ab, pad(a_log), pad(dt_bias))


def _dn_gate_layouts(gates, heads):
    b, l, _ = gates.shape
    t = gates[:, :, :4 * heads].reshape(b, l, 2, 2, heads)
    col = jnp.transpose(t, (0, 4, 1, 2, 3)).reshape(b, heads, l, 4)
    row = jnp.transpose(col.reshape(b, heads, l // DN_CHUNK, DN_CHUNK, 4), (0, 1, 2, 4, 3))
    return col, row


def _dn_kernel(ql_ref, kl_ref, vl_ref, gl_ref, qc_ref, kc_ref, vc_ref, gc_ref,
               cwq_ref, cwk_ref, cwv_ref, coll_ref, rowl_ref, colc_ref, rowc_ref, ng_ref,
               ol_ref, oc_ref,
               xpad, qs, ks, vs, oacc, pbuf, xbuf, atb, abuf, bbuf, qpbuf, egl, *, group):
    n_lat, n_ctx = ql_ref.shape[1], qc_ref.shape[1]
    c = DN_CHUNK
    tbs = 256

    def prep(src_ref, n_rows, off, cw_ref, dst, mode):
        def emit(r0, y):
            y = _silu(y)
            if mode != "v":
                y = y * lax.rsqrt(jnp.sum(y * y, axis=-1, keepdims=True) + EPS)
            if mode == "q":
                y = y * (DN_DK ** -0.5)
            dst[off + r0:off + r0 + y.shape[0], :] = y
        _conv_rows(src_ref, n_rows, xpad, cw_ref[...], None, emit, tbs)

    for src_c, src_l, cw_ref, dst, mode in ((qc_ref, ql_ref, cwq_ref, qs, "q"),
                                            (kc_ref, kl_ref, cwk_ref, ks, "k"),
                                            (vc_ref, vl_ref, cwv_ref, vs, "v")):
        prep(src_c, n_ctx, 0, cw_ref, dst, mode)
        prep(src_l, n_lat, n_ctx, cw_ref, dst, mode)

    ii = lax.broadcasted_iota(jnp.int32, (c, c), 0)
    jj = lax.broadcasted_iota(jnp.int32, (c, c), 1)
    n_apply = int(math.log2(c))
    segments = ((0, n_ctx, colc_ref, rowc_ref), (n_ctx, n_lat, coll_ref, rowl_ref))

    def phase1_group(it, grp, off, col_ref, row_ref):
        def chunk_ids(g):
            ci = it * grp + g
            return ci, pl.multiple_of(ci * c, c), pl.multiple_of(off + ci * c, c), off // c + ci

        def gates(col_ref, lr0, d):
            gcb = col_ref[0, 0, pl.ds(lr0, c), :]
            return gcb[:, d:d + 1], gcb[:, 2 + d:3 + d]

        for g in range(grp):
            ci, lr0, r0, _ = chunk_ids(g)
            q = qs[pl.ds(r0, c), :]
            k = ks[pl.ds(r0, c), :]
            v = vs[pl.ds(r0, c), :]
            kbf = k.astype(BF16)
            kk = _dot_nt(kbf, kbf)
            qk = _dot_nt(q.astype(BF16), kbf)
            rows4 = row_ref[0, 0, ci]
            for d in range(2):
                causal = (ii >= jj) if d == 0 else (ii <= jj)
                strict = (ii > jj) if d == 0 else (ii < jj)
                gc_col, beta = gates(col_ref, lr0, d)
                gc_row = rows4[d:d + 1, :]
                gam = jnp.where(causal, jnp.exp(jnp.where(causal, gc_col - gc_row, 0.0)), 0.0)
                pbuf[2 * g + d] = jnp.where(strict, -(beta * kk) * gam, 0.0).astype(BF16)
                xbuf[2 * g + d] = jnp.concatenate([v * beta, (k * beta) * jnp.exp(gc_col)], axis=1)
                atb[2 * g + d] = (qk * gam).astype(BF16)

        for j in range(n_apply):
            for ch in range(2 * grp):
                pb = pbuf[ch]
                xv = xbuf[ch]
                xbuf[ch] = xv + _dot(pb, xv.astype(BF16))
                if j < n_apply - 1:
                    pbuf[ch] = _dot(pb, pb).astype(BF16)

        for g in range(grp):
            ci, lr0, r0, cg = chunk_ids(g)
            q = qs[pl.ds(r0, c), :]
            k = ks[pl.ds(r0, c), :]
            oloc = None
            for d in range(2):
                gc_col, _ = gates(col_ref, lr0, d)
                g_last = gc_col[c - 1:c, :] if d == 0 else gc_col[0:1, :]
                kdt = (k * jnp.exp(g_last - gc_col)).T.astype(BF16)
                xv = xbuf[2 * g + d]
                ub = xv[:, 0:LANES].astype(BF16)
                wb = xv[:, LANES:2 * LANES].astype(BF16)
                at = atb[2 * g + d]
                abuf[d, cg] = _dot(kdt, wb).astype(BF16)
                bbuf[d, cg] = _dot(kdt, ub).astype(BF16)
                qpbuf[d, pl.ds(r0, c), :] = (q * jnp.exp(gc_col) - _dot(at, wb)).astype(BF16)
                part = _dot(at, ub)
                oloc = part if oloc is None else oloc + part
                egl[d, pl.ds(cg, 1), :] = jnp.broadcast_to(jnp.exp(g_last), (1, LANES))
            oacc[pl.ds(r0, c), :] = oloc

    for off, n_rows, col_ref, row_ref in segments:
        nch = n_rows // c
        grp = min(group, nch)

        def p1_body(it, carry, off=off, col_ref=col_ref, row_ref=row_ref, grp=grp):
            phase1_group(it, grp, off, col_ref, row_ref)
            return carry
        lax.fori_loop(0, nch // grp, p1_body, 0)

    def phase2(d, ci, state, off):
        r0 = pl.multiple_of(off + ci * c, c)
        cg = off // c + ci
        sb = state.astype(BF16)
        oacc[pl.ds(r0, c), :] += _dot(qpbuf[d, pl.ds(r0, c), :], sb)
        return state * egl[d, pl.ds(cg, 1), :] - _dot(abuf[d, cg], sb) + bbuf[d, cg].astype(F32)

    states = (jnp.zeros((DN_DK, LANES), F32), jnp.zeros((DN_DK, LANES), F32))
    for off, n_rows, _, _ in segments:
        nch = n_rows // c

        def p2_body(i, st, off=off, nch=nch):
            return (phase2(0, i, st[0], off), phase2(1, nch - 1 - i, st[1], off))
        states = lax.fori_loop(0, nch, p2_body, states)

    ng = ng_ref[...]
    for off, n_rows, g_ref, o_ref in ((0, n_ctx, gc_ref, oc_ref), (n_ctx, n_lat, gl_ref, ol_ref)):
        t = min(tbs, n_rows)
        for r0 in range(0, n_rows, t):
            o = oacc[off + r0:off + r0 + t, :]
            y = o * lax.rsqrt(jnp.mean(o * o, axis=-1, keepdims=True) + EPS) * ng
            o_ref[0, r0:r0 + t, :] = (y * _silu(g_ref[0, r0:r0 + t, :].astype(F32))).astype(BF16)


def _dn_call(main_l, main_c, col_l, row_l, col_c, row_c, conv_w, norm_g, col0):
    b, l, _ = main_l.shape
    lc = main_c.shape[1]
    heads = col_l.shape[1]
    lt = l + lc
    nch = lt // DN_CHUNK
    cwt = conv_w.T
    group = 8

    def blk(n_rows, which):
        return pl.BlockSpec((1, n_rows, LANES), lambda bi, h, which=which: (bi, 0, col0 + which * heads + h))

    def cw(which):
        return pl.BlockSpec((SHORT_CONV, LANES), lambda bi, h, which=which: (0, which * heads + h))

    in_specs = ([blk(l, w) for w in range(4)] + [blk(lc, w) for w in range(4)] + [cw(0), cw(1), cw(2)] +
                [pl.BlockSpec((1, 1, l, 4), lambda bi, h: (bi, h, 0, 0)),
                 pl.BlockSpec((1, 1, l // DN_CHUNK, 4, DN_CHUNK), lambda bi, h: (bi, h, 0, 0, 0)),
                 pl.BlockSpec((1, 1, lc, 4), lambda bi, h: (bi, h, 0, 0)),
                 pl.BlockSpec((1, 1, lc // DN_CHUNK, 4, DN_CHUNK), lambda bi, h: (bi, h, 0, 0, 0)),
                 pl.BlockSpec((1, LANES), lambda bi, h: (0, 0))])
    return pl.pallas_call(
        functools.partial(_dn_kernel, group=group),
        grid=(b, heads),
        in_specs=in_specs,
        out_specs=[pl.BlockSpec((1, l, LANES), lambda bi, h: (bi, 0, h)),
                   pl.BlockSpec((1, lc, LANES), lambda bi, h: (bi, 0, h))],
        out_shape=[jax.ShapeDtypeStruct((b, l, heads * LANES), BF16),
                   jax.ShapeDtypeStruct((b, lc, heads * LANES), BF16)],
        scratch_shapes=[pltpu.VMEM((l + 2 * SUBLANES, LANES), F32),
                        pltpu.VMEM((lt, LANES), F32),
                        pltpu.VMEM((lt, LANES), F32),
                        pltpu.VMEM((lt, LANES), F32),
                        pltpu.VMEM((lt, LANES), F32),
                        pltpu.VMEM((2 * group, DN_CHUNK, DN_CHUNK), BF16),
                        pltpu.VMEM((2 * group, DN_CHUNK, 2 * LANES), F32),
                        pltpu.VMEM((2 * group, DN_CHUNK, DN_CHUNK), BF16),
                        pltpu.VMEM((2, nch, DN_DK, LANES), BF16),
                        pltpu.VMEM((2, nch, DN_DK, LANES), BF16),
                        pltpu.VMEM((2, lt, LANES), BF16),
                        pltpu.VMEM((2, nch, LANES), F32)],
        compiler_params=_params("parallel", "parallel"),
        name="gated_deltanet",
    )(main_l, main_l, main_l, main_l, main_c, main_c, main_c, main_c, cwt, cwt, cwt,
      col_l, row_l, col_c, row_c, norm_g.reshape(1, LANES))


def _ret_kernel(th_ref, ql_ref, kl_ref, vl_ref, qc_ref, kc_ref, vc_ref, cos_ref, sin_ref, o_ref,
                qs, kts, s_ref, dec_ref, xz_ref):
    n_lat, n_ctx = ql_ref.shape[1], qc_ref.shape[1]
    c = RET_CHUNK
    half = RET_DK // 2
    ncc, nlc = n_ctx // c, n_lat // c
    scale = RET_DK ** -0.5

    icol = lax.broadcasted_iota(jnp.int32, (c, RET_DV), 0)
    ii = lax.broadcasted_iota(jnp.int32, (c, c), 0)
    jj = lax.broadcasted_iota(jnp.int32, (c, c), 1)
    gch = []
    for d in range(2):
        lg = -jnp.exp(th_ref[d, 0])
        lg1 = lg[:, 0:1]
        fidx = (icol if d == 0 else c - 1 - icol).astype(F32)
        rel = (ii - jj) if d == 0 else (jj - ii)
        mask = rel >= 0
        dec_ref[d] = jnp.where(mask, jnp.exp(jnp.where(mask, rel, 0).astype(F32) * lg), 0.0)
        xz_ref[d, 0] = jnp.exp((fidx + 1.0) * lg1)
        xz_ref[d, 1] = jnp.exp((c - 1.0 - fidx) * lg1)
        gch.append(jnp.exp(c * lg1))
    s_ref[...] = jnp.zeros_like(s_ref)

    def prep_ctx(ci, carry):
        r0 = pl.multiple_of(ci * c, c)
        qs[pl.ds(r0, c), :] = qc_ref[0, pl.ds(r0, c), :].astype(BF16)
        k = kc_ref[0, pl.ds(r0, c), :].astype(F32) * scale
        kts[ci] = k.T.astype(BF16)
        return carry

    def prep_lat(ci, carry):
        r0 = pl.multiple_of(ci * c, c)
        cos = cos_ref[pl.ds(r0, c), :]
        sin = sin_ref[pl.ds(r0, c), :]

        def rope(t):
            t1, t2 = t[:, 0:half], t[:, half:2 * half]
            return jnp.concatenate([t1 * cos - t2 * sin, t1 * sin + t2 * cos], axis=-1)

        q = rope(ql_ref[0, pl.ds(r0, c), :].astype(F32))
        k = rope(kl_ref[0, pl.ds(r0, c), :].astype(F32)) * scale
        qs[pl.ds(pl.multiple_of(n_ctx + ci * c, c), c), :] = q.astype(BF16)
        kts[ncc + ci] = k.T.astype(BF16)
        return carry

    lax.fori_loop(0, ncc, prep_ctx, 0)
    lax.fori_loop(0, nlc, prep_lat, 0)

    def one(d, ci, row_off, ch_off, v_ref, write, first):
        r0 = pl.multiple_of(ci * c, c)
        qb = qs[pl.ds(pl.multiple_of(row_off + ci * c, c), c), :]
        kt = kts[ch_off + ci]
        v = v_ref[0, pl.ds(r0, c), :].astype(BF16)
        state = s_ref[d]
        if write:
            inner = _dot(qb, kt) * dec_ref[d]
            o = _dot(inner.astype(BF16), v) + xz_ref[d, 0] * _dot(qb, state.astype(BF16))
            if first:
                o_ref[0, pl.ds(r0, c), :] = o
            else:
                o_ref[0, pl.ds(r0, c), :] += o
        vz = (v.astype(F32) * xz_ref[d, 1]).astype(BF16)
        s_ref[d] = state * gch[d] + _dot(kt, vz)

    def ctx_body(i, carry):
        one(0, i, 0, 0, vc_ref, False, False)
        one(1, ncc - 1 - i, 0, 0, vc_ref, False, False)
        return carry

    lax.fori_loop(0, ncc, ctx_body, 0)
    for lo, hi, first in ((0, nlc // 2, True), (nlc // 2, nlc, False)):
        def lat_body(i, carry, first=first):
            one(0, i, n_ctx, ncc, vl_ref, True, first)
            one(1, nlc - 1 - i, n_ctx, ncc, vl_ref, True, first)
            return carry
        lax.fori_loop(lo, hi, lat_body, 0)


def _ret_call(main_l, main_c, theta, cos, sin, q_col0):
    b, l, _ = main_l.shape
    lc = main_c.shape[1]
    heads = theta.shape[1]
    c = RET_CHUNK
    qb0 = q_col0 // RET_DK
    kb0 = qb0 + heads
    vb0 = (q_col0 + 2 * heads * RET_DK) // RET_DV
    th = jnp.broadcast_to(theta.reshape(2, heads, 1, 1), (2, heads, 1, LANES))
    return pl.pallas_call(
        _ret_kernel,
        grid=(b, heads),
        in_specs=[pl.BlockSpec((2, 1, 1, LANES), lambda bi, h: (0, h, 0, 0)),
                  pl.BlockSpec((1, l, RET_DK), lambda bi, h: (bi, 0, qb0 + h)),
                  pl.BlockSpec((1, l, RET_DK), lambda bi, h: (bi, 0, kb0 + h)),
                  pl.BlockSpec((1, l, RET_DV), lambda bi, h: (bi, 0, vb0 + h)),
                  pl.BlockSpec((1, lc, RET_DK), lambda bi, h: (bi, 0, qb0 + h)),
                  pl.BlockSpec((1, lc, RET_DK), lambda bi, h: (bi, 0, kb0 + h)),
                  pl.BlockSpec((1, lc, RET_DV), lambda bi, h: (bi, 0, vb0 + h)),
                  pl.BlockSpec((l, RET_DK // 2), lambda bi, h: (0, 0)),
                  pl.BlockSpec((l, RET_DK // 2), lambda bi, h: (0, 0))],
        out_specs=pl.BlockSpec((1, l, RET_DV), lambda bi, h: (bi, 0, h)),
        out_shape=jax.ShapeDtypeStruct((b, l, heads * RET_DV), F32),
        scratch_shapes=[pltpu.VMEM((lc + l, RET_DK), BF16),
                        pltpu.VMEM(((lc + l) // c, RET_DK, c), BF16),
                        pltpu.VMEM((2, RET_DK, RET_DV), F32),
                        pltpu.VMEM((2, c, c), F32),
                        pltpu.VMEM((2, 2, c, RET_DV), F32)],
        compiler_params=_params("parallel", "parallel"),
        name="retention",
    )(th, main_l, main_l, main_l, main_c, main_c, main_c, cos, sin)


def _ret_finish_kernel(o_ref, r_ref, y_ref):
    o = o_ref[0]
    y = o * lax.rsqrt(jnp.mean(o * o, axis=-1, keepdims=True) + EPS)
    y_ref[0] = (y * _silu(r_ref[0].astype(F32))).astype(BF16)


def _ret_finish_call(o, main_l, r_col0):
    b, l, w = o.shape
    heads = w // RET_DV
    tm = min(l, 1024)
    rb0 = r_col0 // RET_DV
    return pl.pallas_call(
        _ret_finish_kernel,
        grid=(b, l // tm, heads),
        in_specs=[pl.BlockSpec((1, tm, RET_DV), lambda bi, i, h: (bi, i, h)),
                  pl.BlockSpec((1, tm, RET_DV), lambda bi, i, h: (bi, i, rb0 + h))],
        out_specs=pl.BlockSpec((1, tm, RET_DV), lambda bi, i, h: (bi, i, h)),
        out_shape=jax.ShapeDtypeStruct((b, l, w), BF16),
        compiler_params=_params("parallel", "parallel", "parallel"),
        name="retention_finish",
    )(o, main_l)


def _rope_tables(n_tokens):
    rows = n_tokens // GRID_W
    r, col = jnp.meshgrid(jnp.arange(rows), jnp.arange(GRID_W), indexing='ij')
    n_freq = RET_DK // 4
    inv = ROPE_BASE ** (-jnp.arange(n_freq, dtype=F32) / n_freq)
    ang = jnp.concatenate([r.reshape(-1, 1) * inv, col.reshape(-1, 1) * inv], axis=-1)
    return jnp.cos(ang), jnp.sin(ang)


def _even_layer(xl, xc, mods_l, mods_c, norm1_g, norm2_g, w_in, w_out, s5p, glu_w, glu_b,
                dn_conv_w, dn_a_log, dn_dt_bias, dn_norm_g, ffn, final_g, final_norm):
    s5_width = glu_w.shape[0]
    dn_width = dn_conv_w.shape[0] // 3
    heads = dn_width // LANES
    n_main = s5_width + 4 * dn_width
    w_main = w_in[:, :n_main].astype(BF16)
    w_small = jnp.pad(w_in[:, n_main:], ((0, 0), (0, LANES - (w_in.shape[1] - n_main)))).astype(BF16)

    main_l, ab_l = _inproj_call(xl, norm1_g, mods_l[0], mods_l[1], w_main, w_small)
    main_c, ab_c = _inproj_call(xc, norm1_g, mods_c[0], mods_c[1], w_main, w_small)

    abar, wd, wo, d_skip = s5p
    y_l, y_c = _s5_scan_call(main_l, main_c, abar, wd, wo, d_skip)
    glu_wb = glu_w.astype(BF16)
    s5_l = _s5_glu_call(y_l, glu_wb, glu_b)
    s5_c = _s5_glu_call(y_c, glu_wb, glu_b)

    col_l, row_l = _dn_gate_layouts(_dn_gate_call(ab_l, dn_a_log, dn_dt_bias), heads)
    col_c, row_c = _dn_gate_layouts(_dn_gate_call(ab_c, dn_a_log, dn_dt_bias), heads)
    dn_l, dn_c = _dn_call(main_l, main_c, col_l, row_l, col_c, row_c, dn_conv_w, dn_norm_g,
                          s5_width // LANES)

    w_o1 = w_out[:s5_width].astype(BF16)
    w_o2 = w_out[s5_width:].astype(BF16)
    xl = _outproj_call(xl, mods_l[2], s5_l, dn_l, w_o1, w_o2)
    xc = _outproj_call(xc, mods_c[2], s5_c, dn_c, w_o1, w_o2)
    w1, w3, w2 = ffn
    xl = _ffn_call(xl, norm2_g, mods_l[3], mods_l[4], mods_l[5], w1, w3, w2, final_g, final_norm)
    xc = _ffn_call(xc, norm2_g, mods_c[3], mods_c[4], mods_c[5], w1, w3, w2, final_g, False)
    return xl, xc


def _odd_layer(xl, xc, mods_l, mods_c, norm1_g, norm2_g, w_in, w_out, conv_w, conv_b, wa, ba, wx, bx, lam,
               theta, cos, sin, ffn, final_g, final_norm):
    lru_width = conv_b.shape[0]
    heads = theta.shape[1]
    w_inb = w_in.astype(BF16)
    main_l = _inproj_call(xl, norm1_g, mods_l[0], mods_l[1], w_inb)
    main_c = _inproj_call(xc, norm1_g, mods_c[0], mods_c[1], w_inb)

    lru_l = _lru_call(main_l, main_c, conv_w, conv_b, wa.astype(BF16), ba, wx.astype(BF16), bx, lam)
    q_col0 = 2 * lru_width
    o = _ret_call(main_l, main_c, theta, cos, sin, q_col0)
    ret_l = _ret_finish_call(o, main_l, q_col0 + 2 * heads * RET_DK + heads * RET_DV)

    w_o1 = w_out[:lru_width].astype(BF16)
    w_o2 = w_out[lru_width:].astype(BF16)
    xl = _outproj_call(xl, mods_l[2], lru_l, ret_l, w_o1, w_o2)
    w1, w3, w2 = ffn
    return _ffn_call(xl, norm2_g, mods_l[3], mods_l[4], mods_l[5], w1, w3, w2, final_g, final_norm)


def kernel(x, c, ctx, c_ctx, mod_w, mod_b, norm1_g, norm2_g, ffn_w1, ffn_w3, ffn_w2, final_g, even_w_in, even_w_out, s5_lam_re, s5_lam_im, s5_log_step, s5_b_re, s5_b_im, s5_c_re, s5_c_im, s5_d, s5_glu_w, s5_glu_b, dn_conv_w, dn_a_log, dn_dt_bias, dn_norm_g, odd_w_in, odd_w_out, lru_conv_w, lru_conv_b, lru_wa, lru_ba, lru_wx, lru_bx, lru_lam, ret_theta):
    bsz, n_tok, d = x.shape
    depth = mod_w.shape[0]
    assert depth == 2 and bsz + 1 <= SUBLANES
    cos, sin = _rope_tables(n_tok)

    rows = jnp.concatenate([c, c_ctx[None, :], jnp.zeros((SUBLANES - bsz - 1, d), F32)], axis=0)
    mods = _mod_call(rows, mod_w, mod_b)

    def split_mods(i):
        m = mods[i].reshape(SUBLANES, 6, d)
        ml = [m[:bsz, k][:, None, :] for k in range(6)]
        mc = [jnp.broadcast_to(m[bsz, k][None, None, :], (bsz, 1, d)) for k in range(6)]
        return ml, mc

    xl, xc = x, ctx
    ml, mc = split_mods(0)
    abar, wd, wo = _s5_weights(s5_lam_re[0], s5_lam_im[0], s5_log_step[0], s5_b_re[0], s5_b_im[0],
                               s5_c_re[0], s5_c_im[0])
    ffn0 = (ffn_w1[0].astype(BF16), ffn_w3[0].astype(BF16), ffn_w2[0].astype(BF16))
    xl, xc = _even_layer(xl, xc, ml, mc, norm1_g[0], norm2_g[0], even_w_in[0], even_w_out[0],
                         (abar, wd, wo, s5_d[0]), s5_glu_w[0], s5_glu_b[0],
                         dn_conv_w[0], dn_a_log[0], dn_dt_bias[0], dn_norm_g[0], ffn0, final_g, False)
    ml, mc = split_mods(1)
    ffn1 = (ffn_w1[1].astype(BF16), ffn_w3[1].astype(BF16), ffn_w2[1].astype(BF16))
    return _odd_layer(xl, xc, ml, mc, norm1_g[1], norm2_g[1], odd_w_in[0], odd_w_out[0],
                      lru_conv_w[0], lru_conv_b[0], lru_wa[0], lru_ba[0], lru_wx[0], lru_bx[0], lru_lam[0],
                      ret_theta[0], cos, sin, ffn1, final_g, True)
```

```python
import functools
import math

import jax
import jax.numpy as jnp
from jax import lax
from jax.experimental import pallas as pl
from jax.experimental.pallas import tpu as pltpu

F32 = jnp.float32
BF16 = jnp.bfloat16

EPS = 1e-6
GRID_W = 64
ROPE_BASE = 10000.0

S5_GROUP = 16
S5_STATE = 64
S5_BLOCK = 16

DN_DK = 128
DN_CHUNK = 64
SHORT_CONV = 4

LRU_BLOCK = 128
LRU_C = 8.0

RET_DK = 256
RET_DV = 512
RET_CHUNK = 128

LANES = 128
SUBLANES = 8
VMEM_LIMIT = 56 * 1024 * 1024


def _params(*sem):
    return pltpu.CompilerParams(dimension_semantics=sem, vmem_limit_bytes=VMEM_LIMIT)


def _silu(x):
    return x * jax.nn.sigmoid(x)


def _gelu(x):
    return 0.5 * x * (1.0 + jnp.tanh(0.7978845608028654 * (x + 0.044715 * (x * x * x))))


def _softplus(x):
    return jnp.maximum(x, 0.0) + jnp.log1p(jnp.exp(-jnp.abs(x)))


def _sigmoid(x):
    return 0.5 * (1.0 + jnp.tanh(0.5 * x))


def _expm1_given_exp(x, u):
    safe = jnp.where(u == 1.0, 2.0, u)
    return jnp.where(u == 1.0, x, (u - 1.0) * x / jnp.where(u == 0.0, x, jnp.log(safe)))


def _adaln(x, g, shift, scale):
    y = x * lax.rsqrt(jnp.mean(x * x, axis=-1, keepdims=True) + EPS)
    return (y * g) * (1.0 + scale) + shift


def _dot(a, b):
    return jnp.dot(a, b, preferred_element_type=F32)


def _dot_nt(a, b):
    return lax.dot_general(a, b, (((1,), (1,)), ((), ())), preferred_element_type=F32)


def _mod_kernel(s_ref, w_ref, b_ref, o_ref):
    s = _silu(s_ref[...])
    o_ref[0] = _dot(s.astype(BF16), w_ref[0].astype(BF16)) + b_ref[0]


def _mod_call(rows, mod_w, mod_b):
    depth, d, n = mod_w.shape
    tn = 1024
    return pl.pallas_call(
        _mod_kernel,
        grid=(depth, n // tn),
        in_specs=[pl.BlockSpec((SUBLANES, d), lambda i, j: (0, 0)),
                  pl.BlockSpec((1, d, tn), lambda i, j: (i, 0, j)),
                  pl.BlockSpec((1, 1, tn), lambda i, j: (i, 0, j))],
        out_specs=pl.BlockSpec((1, SUBLANES, tn), lambda i, j: (i, 0, j)),
        out_shape=jax.ShapeDtypeStruct((depth, SUBLANES, n), F32),
        compiler_params=_params("parallel", "parallel"),
        name="mod_proj",
    )(rows, mod_w, mod_b.reshape(depth, 1, n))


def _inproj_kernel(x_ref, g_ref, sh_ref, sc_ref, w_ref, *rest, has_small):
    if has_small:
        ws_ref, o_ref, os_ref, h_ref = rest
    else:
        o_ref, h_ref = rest

    @pl.when(pl.program_id(2) == 0)
    def _():
        h = _adaln(x_ref[0], g_ref[...], sh_ref[0], sc_ref[0]).astype(BF16)
        h_ref[...] = h
        if has_small:
            os_ref[0] = _dot(h, ws_ref[...])

    o_ref[0] = _dot(h_ref[...], w_ref[...]).astype(o_ref.dtype)


def _inproj_call(x, g, shift, scale, w, w_small=None):
    b, l, d = x.shape
    n = w.shape[1]
    tm = min(l, 1024)
    tn = 512
    has_small = w_small is not None
    in_specs = [pl.BlockSpec((1, tm, d), lambda bi, i, j: (bi, i, 0)),
                pl.BlockSpec((1, d), lambda bi, i, j: (0, 0)),
                pl.BlockSpec((1, 1, d), lambda bi, i, j: (bi, 0, 0)),
                pl.BlockSpec((1, 1, d), lambda bi, i, j: (bi, 0, 0)),
                pl.BlockSpec((d, tn), lambda bi, i, j: (0, j))]
    out_specs = [pl.BlockSpec((1, tm, tn), lambda bi, i, j: (bi, i, j))]
    out_shape = [jax.ShapeDtypeStruct((b, l, n), BF16)]
    args = [x, g.reshape(1, d), shift, scale, w]
    if has_small:
        in_specs.append(pl.BlockSpec((d, LANES), lambda bi, i, j: (0, 0)))
        out_specs.append(pl.BlockSpec((1, tm, LANES), lambda bi, i, j: (bi, i, 0)))
        out_shape.append(jax.ShapeDtypeStruct((b, l, LANES), F32))
        args.append(w_small)
    outs = pl.pallas_call(
        functools.partial(_inproj_kernel, has_small=has_small),
        grid=(b, l // tm, n // tn),
        in_specs=in_specs, out_specs=out_specs, out_shape=out_shape,
        scratch_shapes=[pltpu.VMEM((tm, d), BF16)],
        compiler_params=_params("parallel", "parallel", "arbitrary"),
        name="adaln_inproj",
    )(*args)
    return outs if has_small else outs[0]


def _outproj_kernel(x_ref, gate_ref, a1_ref, a2_ref, w1_ref, w2_ref, o_ref):
    y = _dot(a1_ref[0], w1_ref[...]) + _dot(a2_ref[0], w2_ref[...])
    o_ref[0] = x_ref[0] + gate_ref[0] * y


def _outproj_call(x, gate, a1, a2, w1, w2):
    b, l, d = x.shape
    k1, k2 = a1.shape[2], a2.shape[2]
    tm = min(l, 1024)
    tn = 512
    return pl.pallas_call(
        _outproj_kernel,
        grid=(b, l // tm, d // tn),
        in_specs=[pl.BlockSpec((1, tm, tn), lambda bi, i, j: (bi, i, j)),
                  pl.BlockSpec((1, 1, tn), lambda bi, i, j: (bi, 0, j)),
                  pl.BlockSpec((1, tm, k1), lambda bi, i, j: (bi, i, 0)),
                  pl.BlockSpec((1, tm, k2), lambda bi, i, j: (bi, i, 0)),
                  pl.BlockSpec((k1, tn), lambda bi, i, j: (0, j)),
                  pl.BlockSpec((k2, tn), lambda bi, i, j: (0, j))],
        out_specs=pl.BlockSpec((1, tm, tn), lambda bi, i, j: (bi, i, j)),
        out_shape=jax.ShapeDtypeStruct((b, l, d), F32),
        compiler_params=_params("parallel", "parallel", "arbitrary"),
        name="outproj_residual",
    )(x, gate, a1, a2, w1, w2)


def _ffn_kernel(x_ref, g_ref, sh_ref, sc_ref, gate_ref, w1_ref, w3_ref, w2_ref, fg_ref, o_ref,
                h_ref, acc_ref, *, final_norm):
    f = pl.program_id(2)

    @pl.when(f == 0)
    def _():
        h_ref[...] = _adaln(x_ref[0], g_ref[...], sh_ref[0], sc_ref[0]).astype(BF16)
        acc_ref[...] = jnp.zeros_like(acc_ref)

    h = h_ref[...]
    a = _dot(h, w1_ref[...])
    b = _dot(h, w3_ref[...])
    acc_ref[...] += _dot((_silu(a) * b).astype(BF16), w2_ref[...])

    @pl.when(f == pl.num_programs(2) - 1)
    def _():
        y = x_ref[0] + gate_ref[0] * acc_ref[...]
        if final_norm:
            y = y * lax.rsqrt(jnp.mean(y * y, axis=-1, keepdims=True) + EPS) * fg_ref[...]
        o_ref[0] = y


def _ffn_call(x, g, shift, scale, gate, w1, w3, w2, final_g, final_norm):
    b, l, d = x.shape
    dff = w1.shape[1]
    tm = min(l, 512)
    tf = 512
    vec = pl.BlockSpec((1, 1, d), lambda bi, i, f: (bi, 0, 0))
    return pl.pallas_call(
        functools.partial(_ffn_kernel, final_norm=final_norm),
        grid=(b, l // tm, dff // tf),
        in_specs=[pl.BlockSpec((1, tm, d), lambda bi, i, f: (bi, i, 0)),
                  pl.BlockSpec((1, d), lambda bi, i, f: (0, 0)),
                  vec, vec, vec,
                  pl.BlockSpec((d, tf), lambda bi, i, f: (0, f)),
                  pl.BlockSpec((d, tf), lambda bi, i, f: (0, f)),
                  pl.BlockSpec((tf, d), lambda bi, i, f: (f, 0)),
                  pl.BlockSpec((1, d), lambda bi, i, f: (0, 0))],
        out_specs=pl.BlockSpec((1, tm, d), lambda bi, i, f: (bi, i, 0)),
        out_shape=jax.ShapeDtypeStruct((b, l, d), F32),
        scratch_shapes=[pltpu.VMEM((tm, d), BF16), pltpu.VMEM((tm, d), F32)],
        compiler_params=_params("parallel", "parallel", "arbitrary"),
        name="ffn_swiglu",
    )(x, g.reshape(1, d), shift, scale, gate, w1, w3, w2, final_g.reshape(1, d))


def _s5_disc_kernel(lre_ref, lim_ref, ls_ref, bre_ref, bim_ref, are_ref, aim_ref, bbre_ref, bbim_ref):
    lre = jnp.minimum(lre_ref[...], -1e-4)
    lim = lim_ref[...]
    dt = jnp.exp(ls_ref[...])
    mag = jnp.exp(lre * dt)
    ar = mag * jnp.cos(lim * dt)
    ai = mag * jnp.sin(lim * dt)
    nr, ni = ar - 1.0, ai
    den = lre * lre + lim * lim
    cr = (nr * lre + ni * lim) / den
    ci = (ni * lre - nr * lim) / den
    bre, bim = bre_ref[...], bim_ref[...]
    are_ref[...] = ar
    aim_ref[...] = ai
    bbre_ref[...] = cr * bre - ci * bim
    bbim_ref[...] = cr * bim + ci * bre


def _s5_taps_kernel(ar_ref, ai_ref, br_ref, bi_ref, cr_ref, ci_ref,
                    k_ref, dr_ref, di_ref, gr_ref, gi_ref, apw_ref, ccr, cci, *, reverse_from):
    t_blk = S5_BLOCK
    s = S5_GROUP
    ar, ai = ar_ref[0], ai_ref[0]
    br, bi = br_ref[0], bi_ref[0]
    cr, ci = cr_ref[0], ci_ref[0]
    backward = pl.program_id(0) >= reverse_from
    pw = [(jnp.ones_like(ar), jnp.zeros_like(ar))]
    for _ in range(t_blk):
        pr, pi = pw[-1]
        pw.append((pr * ar - pi * ai, pr * ai + pi * ar))
    for t in range(t_blk):
        pr, pi = pw[t]
        ccr[t * s:(t + 1) * s, :] = cr * pr - ci * pi
        cci[t * s:(t + 1) * s, :] = cr * pi + ci * pr
        dfr, dfi = pw[t_blk - 1 - t]
        dpr = jnp.where(backward, pr, dfr)
        dpi = jnp.where(backward, pi, dfi)
        dr_ref[0, t * s:(t + 1) * s, :] = br * dpr - bi * dpi
        di_ref[0, t * s:(t + 1) * s, :] = br * dpi + bi * dpr
        gfr, gfi = pw[t + 1]
        gbr, gbi = pw[t_blk - t]
        gpr = jnp.where(backward, gbr, gfr)
        gpi = jnp.where(backward, gbi, gfi)
        gr_ref[0, t * s:(t + 1) * s, :] = cr * gpr - ci * gpi
        gi_ref[0, t * s:(t + 1) * s, :] = -(cr * gpi + ci * gpr)
    pad = jnp.zeros((LANES - s, br.shape[1]), F32)
    hi = lax.Precision.HIGHEST
    brp = jnp.concatenate([br, pad], axis=0)
    bip = jnp.concatenate([bi, pad], axis=0)
    nt = (((1,), (1,)), ((), ()))
    k_ref[0] = (lax.dot_general(ccr[...], brp, nt, precision=hi, preferred_element_type=F32)
                - lax.dot_general(cci[...], bip, nt, precision=hi, preferred_element_type=F32))
    qr, qi = pw[t_blk]
    er, ei = jnp.ones_like(qr), jnp.zeros_like(qr)
    for kk in range(9):
        apw_ref[0, 2 * kk:2 * kk + 1, :] = jnp.concatenate([er, er], axis=1)
        apw_ref[0, 2 * kk + 1:2 * kk + 2, :] = jnp.concatenate([-ei, ei], axis=1)
        er, ei = er * qr - ei * qi, er * qi + ei * qr


def _s5_weights(lam_re, lam_im, log_step, b_re, b_im, c_re, c_im, d_skip):
    nd, g, p = lam_re.shape
    s = b_re.shape[-1]
    t_blk = S5_BLOCK
    rows = nd * g * s
    rep = lambda t: jnp.repeat(t.reshape(nd * g, p), s, axis=0)
    ls = jnp.broadcast_to(log_step.reshape(nd * g, 1), (nd * g, p))
    tb = lambda t: jnp.transpose(t, (0, 1, 3, 2)).reshape(rows, p)
    shp = jax.ShapeDtypeStruct((rows, p), F32)
    are, aim, bbre, bbim = pl.pallas_call(
        _s5_disc_kernel, out_shape=[shp, shp, shp, shp], name="s5_discretise",
    )(rep(lam_re), rep(lam_im), rep(ls), tb(b_re), tb(b_im))
    n = nd * g
    ts = t_blk * s
    vec = pl.BlockSpec((1, 1, p), lambda i: (i, 0, 0))
    mat = pl.BlockSpec((1, s, p), lambda i: (i, 0, 0))
    big = pl.BlockSpec((1, ts, p), lambda i: (i, 0, 0))
    bigshape = jax.ShapeDtypeStruct((n, ts, p), F32)
    kraw, dr, di, gr, gi, apw = pl.pallas_call(
        functools.partial(_s5_taps_kernel, reverse_from=g),
        grid=(n,),
        in_specs=[vec, vec, mat, mat, mat, mat],
        out_specs=[pl.BlockSpec((1, ts, LANES), lambda i: (i, 0, 0)), big, big, big, big,
                   pl.BlockSpec((1, 18, 2 * p), lambda i: (i, 0, 0))],
        out_shape=[jax.ShapeDtypeStruct((n, ts, LANES), F32), bigshape, bigshape, bigshape, bigshape,
                   jax.ShapeDtypeStruct((n, 18, 2 * p), F32)],
        scratch_shapes=[pltpu.VMEM((ts, p), F32), pltpu.VMEM((ts, p), F32)],
        compiler_params=_params("parallel"),
        name="s5_block_taps",
    )(are[::s].reshape(n, 1, p), aim[::s].reshape(n, 1, p), bbre.reshape(n, s, p), bbim.reshape(n, s, p),
      c_re.reshape(n, s, p), c_im.reshape(n, s, p))

    taps = kraw[:, :, :s].reshape(nd, g, t_blk, s, s)
    ti = jnp.arange(t_blk)
    lag = ti[None, :] - ti[:, None]

    def toeplitz(tp, lag):
        m = tp[:, jnp.clip(lag, 0, t_blk - 1)]
        m = jnp.where((lag >= 0)[None, :, :, None, None], m, 0.0)
        return jnp.transpose(m, (0, 1, 4, 2, 3)).reshape(g, ts, ts)

    ktot = toeplitz(taps[0], lag) + toeplitz(taps[1], -lag)
    dcat = lambda d: jnp.concatenate([dr.reshape(nd, g, ts, p)[d], di.reshape(nd, g, ts, p)[d]], axis=-1)
    w1 = jnp.concatenate([ktot, dcat(0), dcat(1)], axis=-1).astype(BF16)
    gt = lambda t, d: jnp.transpose(t.reshape(nd, g, ts, p)[d], (0, 2, 1))
    w2 = jnp.concatenate([gt(gr, 0), gt(gi, 0), gt(gr, 1), gt(gi, 1)], axis=1).astype(BF16)
    apw = jnp.transpose(apw.reshape(nd, g, 18, 2 * p), (1, 0, 2, 3))
    dsk = jnp.tile(d_skip.reshape(g, 1, s), (1, t_blk, 1)).reshape(g, 1, ts)
    return w1, w2, apw, dsk


def _s5_kernel(u_ref, w1_ref, w2_ref, apw_ref, dsk_ref, y_ref, zbuf, hbuf, sbuf, *, bsz, nb_ctx, nb_lat):
    ts = u_ref.shape[2]
    half = LANES // 2
    nb = nb_ctx + nb_lat
    u = u_ref[0]
    zbuf[...] = _dot(u, w1_ref[0])
    row = lax.broadcasted_iota(jnp.int32, (SUBLANES, LANES), 0)

    def cmul(x, p1, p2):
        return x * p1 + pltpu.roll(x, half, x.ndim - 1) * p2

    nt_all = bsz * nb // SUBLANES
    for d in range(2):
        col = ts + d * LANES
        pw = lambda k, d=d: (apw_ref[0, d, 2 * k:2 * k + 1, :], apw_ref[0, d, 2 * k + 1:2 * k + 2, :])
        x = zbuf[:, col:col + LANES].reshape(nt_all, SUBLANES, LANES)
        row3 = lax.broadcasted_iota(jnp.int32, x.shape, 1)
        for k in (1, 2, 4):
            keep = (row3 >= k) if d == 0 else (row3 <= SUBLANES - 1 - k)
            sh = jnp.where(keep, pltpu.roll(x, k if d == 0 else SUBLANES - k, 1), 0.0)
            p1, p2 = pw(k)
            x = x + cmul(sh, p1, p2)
        sbuf[0] = x.reshape(nt_all * SUBLANES, LANES)
        sbuf[1] = pltpu.roll(x, half, 2).reshape(nt_all * SUBLANES, LANES)

        pex1 = jnp.zeros((SUBLANES, LANES), F32)
        pex2 = jnp.zeros((SUBLANES, LANES), F32)
        for r in range(SUBLANES):
            p1, p2 = pw(r if d == 0 else SUBLANES - 1 - r)
            pex1 = jnp.where(row == r, p1, pex1)
            pex2 = jnp.where(row == r, p2, pex2)
        p81, p82 = pw(SUBLANES)
        keep1 = (row >= 1) if d == 0 else (row <= SUBLANES - 2)
        last = SUBLANES - 1 if d == 0 else 0

        def tile_step(r0, carry):
            cn, cs = carry
            s_t = sbuf[0, pl.ds(r0, SUBLANES), :]
            ss_t = sbuf[1, pl.ds(r0, SUBLANES), :]
            excl = jnp.where(keep1, pltpu.roll(s_t, 1 if d == 0 else SUBLANES - 1, 0), 0.0)
            hbuf[pl.ds(r0, SUBLANES), d * LANES:(d + 1) * LANES] = excl + cn * pex1 + cs * pex2
            return (jnp.broadcast_to(s_t[last:last + 1], (SUBLANES, LANES)) + cn * p81 + cs * p82,
                    jnp.broadcast_to(ss_t[last:last + 1], (SUBLANES, LANES)) + cs * p81 - cn * p82)

        for seg0, ntile in ((0, nb_ctx // SUBLANES), (nb_ctx // SUBLANES, nb_lat // SUBLANES)):
            def body(i, carries, seg0=seg0, ntile=ntile):
                ti = seg0 + (i if d == 0 else ntile - 1 - i)
                out = []
                for b in range(bsz):
                    r0 = pl.multiple_of(b * nb + ti * SUBLANES, SUBLANES)
                    out.append(tile_step(r0, carries[b]))
                return tuple(out)
            if seg0 == 0:
                zero = jnp.zeros((SUBLANES, LANES), F32)
                carries = tuple((zero, zero) for _ in range(bsz))
            carries = lax.fori_loop(0, ntile, body, carries)

    y_ref[0] = (dsk_ref[0] * u.astype(F32) + zbuf[:, 0:ts]
                + _dot(hbuf[...].astype(BF16), w2_ref[0]))


def _s5_scan_call(main_l, main_c, w1, w2, apw, dsk):
    b, l, _ = main_l.shape
    lc = main_c.shape[1]
    g, ts, _ = w2.shape
    s = S5_GROUP
    t_blk = ts // s
    nb_ctx, nb_lat = lc // t_blk, l // t_blk
    nb = nb_ctx + nb_lat
    width = g * s

    def blocks(m, n):
        return jnp.transpose(m[:, :, :width].reshape(b, n, t_blk, g, s), (3, 0, 1, 2, 4)).reshape(g, b, n, ts)

    u = jnp.concatenate([blocks(main_c, nb_ctx), blocks(main_l, nb_lat)], axis=2).reshape(g, b * nb, ts)
    y = pl.pallas_call(
        functools.partial(_s5_kernel, bsz=b, nb_ctx=nb_ctx, nb_lat=nb_lat),
        grid=(g,),
        in_specs=[pl.BlockSpec((1, b * nb, ts), lambda i: (i, 0, 0)),
                  pl.BlockSpec((1, ts, ts + 2 * LANES), lambda i: (i, 0, 0)),
                  pl.BlockSpec((1, 2 * LANES, ts), lambda i: (i, 0, 0)),
                  pl.BlockSpec((1, 2, 18, LANES), lambda i: (i, 0, 0, 0)),
                  pl.BlockSpec((1, 1, ts), lambda i: (i, 0, 0))],
        out_specs=pl.BlockSpec((1, b * nb, ts), lambda i: (i, 0, 0)),
        out_shape=jax.ShapeDtypeStruct((g, b * nb, ts), F32),
        scratch_shapes=[pltpu.VMEM((b * nb, ts + 2 * LANES), F32),
                        pltpu.VMEM((b * nb, 2 * LANES), F32),
                        pltpu.VMEM((2, b * nb, LANES), F32)],
        compiler_params=_params("parallel"),
        name="s5_scan",
    )(u, w1, w2, apw, dsk)
    y = y.reshape(g, b, nb, t_blk, s)

    def unblocks(t, n):
        return jnp.transpose(t, (1, 2, 3, 0, 4)).reshape(b, n * t_blk, width)

    return unblocks(y[:, :, nb_ctx:], nb_lat), unblocks(y[:, :, :nb_ctx], nb_ctx)


def _s5_glu_kernel(y_ref, w_ref, b_ref, o_ref):
    g = _gelu(y_ref[0])
    o_ref[0] = (g * jax.nn.sigmoid(_dot(g.astype(BF16), w_ref[...]) + b_ref[...])).astype(BF16)


def _s5_glu_call(y, glu_w, glu_b):
    b, l, w = y.shape
    tm = min(l, 1024)
    return pl.pallas_call(
        _s5_glu_kernel,
        grid=(b, l // tm),
        in_specs=[pl.BlockSpec((1, tm, w), lambda bi, i: (bi, i, 0)),
                  pl.BlockSpec((w, w), lambda bi, i: (0, 0)),
                  pl.BlockSpec((1, w), lambda bi, i: (0, 0))],
        out_specs=pl.BlockSpec((1, tm, w), lambda bi, i: (bi, i, 0)),
        out_shape=jax.ShapeDtypeStruct((b, l, w), BF16),
        compiler_params=_params("parallel", "parallel"),
        name="s5_glu",
    )(y, glu_w, glu_b.reshape(1, w))


def _conv_rows(src_ref, n_rows, xpad, cw, bias, emit, tbs):
    zeros = jnp.zeros((SUBLANES, LANES), F32)
    xpad[0:SUBLANES, :] = zeros
    xpad[SUBLANES + n_rows:2 * SUBLANES + n_rows, :] = zeros
    tbs = min(tbs, n_rows)
    for r0 in range(0, n_rows, tbs):
        xpad[SUBLANES + r0:SUBLANES + r0 + tbs, :] = src_ref[0, r0:r0 + tbs, :].astype(F32)
    for r0 in range(0, n_rows, tbs):
        acc = None
        for j in range(SHORT_CONV):
            off = SUBLANES + r0 + j - SHORT_CONV // 2
            term = cw[j:j + 1, :] * xpad[off:off + tbs, :]
            acc = term if acc is None else acc + term
        if bias is not None:
            acc = acc + bias
        emit(r0, acc)


def _lru_kernel(xl_ref, xc_ref, gl_ref, cw_ref, cb_ref, wa_ref, wx_ref, ba_ref, bx_ref, lam_ref, o_ref,
                xpad, xconv, abuf, bbuf, hsum, *, tb):
    n_lat, n_ctx = xl_ref.shape[1], xc_ref.shape[1]
    cw = cw_ref[...]
    cb = cb_ref[...]

    def put(off):
        def emit(r0, y):
            xconv[off + r0:off + r0 + y.shape[0], :] = y
        return emit

    _conv_rows(xc_ref, n_ctx, xpad, cw, cb, put(0), tb)
    _conv_rows(xl_ref, n_lat, xpad, cw, cb, put(n_ctx), tb)

    row = lax.broadcasted_iota(jnp.int32, (SUBLANES, LANES), 0)

    def tile_body(d, ntiles, t, carry):
        ti = t if d == 0 else ntiles - 1 - t
        r0 = pl.multiple_of(ti * SUBLANES, SUBLANES)
        h = bbuf[pl.ds(r0, SUBLANES), :] + abuf[pl.ds(r0, SUBLANES), :] * carry
        bbuf[pl.ds(r0, SUBLANES), :] = h
        last = SUBLANES - 1 if d == 0 else 0
        return jnp.broadcast_to(h[last:last + 1], (SUBLANES, LANES))

    def local_scans(d, a, b):
        nt = a.shape[0] // SUBLANES
        a3 = a.reshape(nt, SUBLANES, LANES)
        b3 = b.reshape(nt, SUBLANES, LANES)
        row3 = lax.broadcasted_iota(jnp.int32, a3.shape, 1)
        for k in (1, 2, 4):
            sh = k if d == 0 else SUBLANES - k
            keep = (row3 >= k) if d == 0 else (row3 <= SUBLANES - 1 - k)
            sa = jnp.where(keep, pltpu.roll(a3, sh, 1), 1.0)
            sb = jnp.where(keep, pltpu.roll(b3, sh, 1), 0.0)
            b3 = b3 + a3 * sb
            a3 = a3 * sa
        return a3.reshape(a.shape), b3.reshape(b.shape)

    for d in range(2):
        sp = _softplus(-lam_ref[d])
        wa, wx = wa_ref[d, 0], wx_ref[d, 0]
        ba, bx = ba_ref[d], bx_ref[d]

        def run_segment(off, n_rows, is_lat, carry, d=d, sp=sp, wa=wa, wx=wx, ba=ba, bx=bx):
            tbs = min(tb, n_rows)
            nblk = n_rows // tbs

            def blk_body(i, carry):
                bi = i if d == 0 else nblk - 1 - i
                lr0 = pl.multiple_of(bi * tbs, tbs)
                xc = xconv[pl.ds(off + lr0, tbs), :]
                xb = xc.astype(BF16)
                r = _sigmoid(_dot(xb, wa) + ba)
                ig = _sigmoid(_dot(xb, wx) + bx)
                log_a = -LRU_C * r * sp
                a = jnp.exp(log_a)
                em = _expm1_given_exp(2.0 * log_a, a * a)
                acum, hloc = local_scans(d, a, jnp.sqrt(jnp.maximum(-em, 0.0)) * (ig * xc))
                abuf[0:tbs, :] = acum
                bbuf[0:tbs, :] = hloc
                carry = lax.fori_loop(0, tbs // SUBLANES,
                                      functools.partial(tile_body, d, tbs // SUBLANES), carry)
                if is_lat:
                    if d == 0:
                        hsum[pl.ds(lr0, tbs), :] = bbuf[0:tbs, :]
                    else:
                        y = (hsum[pl.ds(lr0, tbs), :] + bbuf[0:tbs, :]) * _gelu(gl_ref[0, pl.ds(lr0, tbs), :].astype(F32))
                        o_ref[0, pl.ds(lr0, tbs), :] = y.astype(BF16)
                return carry

            return lax.fori_loop(0, nblk, blk_body, carry)

        carry = run_segment(0, n_ctx, False, jnp.zeros((SUBLANES, LANES), F32))
        run_segment(n_ctx, n_lat, True, carry)


def _lru_call(main_l, main_c, conv_w, conv_b, wa, ba, wx, bx, lam):
    b, l, _ = main_l.shape
    lc = main_c.shape[1]
    width = conv_b.shape[0]
    nb = width // LRU_BLOCK
    tb = 256
    vec = pl.BlockSpec((2, 1, LANES), lambda bi, n: (0, 0, n))
    wspec = pl.BlockSpec((2, 1, LRU_BLOCK, LRU_BLOCK), lambda bi, n: (0, n, 0, 0))
    return pl.pallas_call(
        functools.partial(_lru_kernel, tb=tb),
        grid=(b, nb),
        in_specs=[pl.BlockSpec((1, l, LANES), lambda bi, n: (bi, 0, n)),
                  pl.BlockSpec((1, lc, LANES), lambda bi, n: (bi, 0, n)),
                  pl.BlockSpec((1, l, LANES), lambda bi, n, nb=nb: (bi, 0, nb + n)),
                  pl.BlockSpec((SHORT_CONV, LANES), lambda bi, n: (0, n)),
                  pl.BlockSpec((1, LANES), lambda bi, n: (0, n)),
                  wspec, wspec, vec, vec, vec],
        out_specs=pl.BlockSpec((1, l, LANES), lambda bi, n: (bi, 0, n)),
        out_shape=jax.ShapeDtypeStruct((b, l, width), BF16),
        scratch_shapes=[pltpu.VMEM((l + 2 * SUBLANES, LANES), F32),
                        pltpu.VMEM((lc + l, LANES), F32),
                        pltpu.VMEM((tb, LANES), F32),
                        pltpu.VMEM((tb, LANES), F32),
                        pltpu.VMEM((l, LANES), F32)],
        compiler_params=_params("parallel", "parallel"),
        name="rglru",
    )(main_l, main_c, main_l, conv_w.T, conv_b.reshape(1, width), wa, wx,
      ba.reshape(2, 1, width), bx.reshape(2, 1, width), lam.reshape(2, 1, width))


def _dn_gate_kernel(ab_ref, alog_ref, dtb_ref, o_ref, *, heads):
    x = ab_ref[0]
    tm = x.shape[0]
    lane = lax.broadcasted_iota(jnp.int32, x.shape, 1)
    g = -jnp.exp(alog_ref[...]) * _softplus(x + dtb_ref[...])
    ii = lax.broadcasted_iota(jnp.int32, (tm, tm), 0)
    jj = lax.broadcasted_iota(jnp.int32, (tm, tm), 1)
    same = (ii // DN_CHUNK) == (jj // DN_CHUNK)
    lower = jnp.where(same & (jj <= ii), 1.0, 0.0).astype(F32)
    upper = jnp.where(same & (jj >= ii), 1.0, 0.0).astype(F32)
    pre = jnp.dot(lower, g, preferred_element_type=F32, precision=lax.Precision.HIGHEST)
    suf = jnp.dot(upper, g, preferred_element_type=F32, precision=lax.Precision.HIGHEST)
    gc = jnp.where(lane < heads, pre, suf)
    o_ref[0] = jnp.where(lane < 2 * heads, gc, jax.nn.sigmoid(x))


def _dn_gate_call(ab, a_log, dt_bias):
    b, l, _ = ab.shape
    tm = min(l, 256)
    pad = lambda t: jnp.pad(t.reshape(1, -1), ((0, 0), (0, LANES - t.size)))
    return pl.pallas_call(
        functools.partial(_dn_gate_kernel, heads=a_log.shape[1]),
        grid=(b, l // tm),
        in_specs=[pl.BlockSpec((1, tm, LANES), lambda bi, i: (bi, i, 0)),
                  pl.BlockSpec((1, LANES), lambda bi, i: (0, 0)),
                  pl.BlockSpec((1, LANES), lambda bi, i: (0, 0))],
        out_specs=pl.BlockSpec((1, tm, LANES), lambda bi, i: (bi, i, 0)),
        out_shape=jax.ShapeDtypeStruct((b, l, LANES), F32),
        compiler_params=_params("parallel", "parallel"),
        name="deltanet_gates",
    )(ab, pad(a_log), pad(dt_bias))


def _dn_gate_layouts(gates, heads):
    b, l, _ = gates.shape
    t = gates[:, :, :4 * heads].reshape(b, l, 2, 2, heads)
    col = jnp.transpose(t, (0, 4, 1, 2, 3)).reshape(b, heads, l, 4)
    row = jnp.transpose(col.reshape(b, heads, l // DN_CHUNK, DN_CHUNK, 4), (0, 1, 2, 4, 3))
    return col, row


def _dn_kernel(ql_ref, kl_ref, vl_ref, gl_ref, qc_ref, kc_ref, vc_ref, gc_ref,
               cwq_ref, cwk_ref, cwv_ref, coll_ref, rowl_ref, colc_ref, rowc_ref, ng_ref,
               ol_ref, oc_ref,
               xpad, qs, ks, vs, oacc, pbuf, xbuf, atb, abuf, bbuf, qpbuf, egl, *, group):
    n_lat, n_ctx = ql_ref.shape[1], qc_ref.shape[1]
    c = DN_CHUNK
    tbs = 256

    def prep(src_ref, n_rows, off, cw_ref, dst, mode):
        def emit(r0, y):
            y = _silu(y)
            if mode != "v":
                y = y * lax.rsqrt(jnp.sum(y * y, axis=-1, keepdims=True) + EPS)
            if mode == "q":
                y = y * (DN_DK ** -0.5)
            dst[off + r0:off + r0 + y.shape[0], :] = y
        _conv_rows(src_ref, n_rows, xpad, cw_ref[...], None, emit, tbs)

    for src_c, src_l, cw_ref, dst, mode in ((qc_ref, ql_ref, cwq_ref, qs, "q"),
                                            (kc_ref, kl_ref, cwk_ref, ks, "k"),
                                            (vc_ref, vl_ref, cwv_ref, vs, "v")):
        prep(src_c, n_ctx, 0, cw_ref, dst, mode)
        prep(src_l, n_lat, n_ctx, cw_ref, dst, mode)

    ii = lax.broadcasted_iota(jnp.int32, (c, c), 0)
    jj = lax.broadcasted_iota(jnp.int32, (c, c), 1)
    n_apply = int(math.log2(c))
    segments = ((0, n_ctx, colc_ref, rowc_ref), (n_ctx, n_lat, coll_ref, rowl_ref))

    def phase1_group(it, grp, off, col_ref, row_ref):
        def chunk_ids(g):
            ci = it * grp + g
            return ci, pl.multiple_of(ci * c, c), pl.multiple_of(off + ci * c, c), off // c + ci

        def gates(col_ref, lr0, d):
            gcb = col_ref[0, 0, pl.ds(lr0, c), :]
            return gcb[:, d:d + 1], gcb[:, 2 + d:3 + d]

        for g in range(grp):
            ci, lr0, r0, _ = chunk_ids(g)
            q = qs[pl.ds(r0, c), :]
            k = ks[pl.ds(r0, c), :]
            v = vs[pl.ds(r0, c), :]
            kbf = k.astype(BF16)
            kk = _dot_nt(kbf, kbf)
            qk = _dot_nt(q.astype(BF16), kbf)
            rows4 = row_ref[0, 0, ci]
            for d in range(2):
                causal = (ii >= jj) if d == 0 else (ii <= jj)
                strict = (ii > jj) if d == 0 else (ii < jj)
                gc_col, beta = gates(col_ref, lr0, d)
                gc_row = rows4[d:d + 1, :]
                gam = jnp.where(causal, jnp.exp(jnp.where(causal, gc_col - gc_row, 0.0)), 0.0)
                pbuf[2 * g + d] = jnp.where(strict, -(beta * kk) * gam, 0.0).astype(BF16)
                xbuf[2 * g + d] = jnp.concatenate([v * beta, (k * beta) * jnp.exp(gc_col)], axis=1)
                atb[2 * g + d] = (qk * gam).astype(BF16)

        for j in range(n_apply):
            for ch in range(2 * grp):
                pb = pbuf[ch]
                xv = xbuf[ch]
                xbuf[ch] = xv + _dot(pb, xv.astype(BF16))
                if j < n_apply - 1:
                    pbuf[ch] = _dot(pb, pb).astype(BF16)

        for g in range(grp):
            ci, lr0, r0, cg = chunk_ids(g)
            q = qs[pl.ds(r0, c), :]
            k = ks[pl.ds(r0, c), :]
            oloc = None
            for d in range(2):
                gc_col, _ = gates(col_ref, lr0, d)
                g_last = gc_col[c - 1:c, :] if d == 0 else gc_col[0:1, :]
                kdt = (k * jnp.exp(g_last - gc_col)).T.astype(BF16)
                xv = xbuf[2 * g + d]
                ub = xv[:, 0:LANES].astype(BF16)
                wb = xv[:, LANES:2 * LANES].astype(BF16)
                at = atb[2 * g + d]
                abuf[d, cg] = _dot(kdt, wb).astype(BF16)
                bbuf[d, cg] = _dot(kdt, ub).astype(BF16)
                qpbuf[d, pl.ds(r0, c), :] = (q * jnp.exp(gc_col) - _dot(at, wb)).astype(BF16)
                part = _dot(at, ub)
                oloc = part if oloc is None else oloc + part
                egl[d, pl.ds(cg, 1), :] = jnp.broadcast_to(jnp.exp(g_last), (1, LANES))
            oacc[pl.ds(r0, c), :] = oloc

    for off, n_rows, col_ref, row_ref in segments:
        nch = n_rows // c
        grp = min(group, nch)

        def p1_body(it, carry, off=off, col_ref=col_ref, row_ref=row_ref, grp=grp):
            phase1_group(it, grp, off, col_ref, row_ref)
            return carry
        lax.fori_loop(0, nch // grp, p1_body, 0)

    def phase2(d, ci, state, off):
        r0 = pl.multiple_of(off + ci * c, c)
        cg = off // c + ci
        sb = state.astype(BF16)
        oacc[pl.ds(r0, c), :] += _dot(qpbuf[d, pl.ds(r0, c), :], sb)
        return state * egl[d, pl.ds(cg, 1), :] - _dot(abuf[d, cg], sb) + bbuf[d, cg].astype(F32)

    states = (jnp.zeros((DN_DK, LANES), F32), jnp.zeros((DN_DK, LANES), F32))
    for off, n_rows, _, _ in segments:
        nch = n_rows // c

        def p2_body(i, st, off=off, nch=nch):
            return (phase2(0, i, st[0], off), phase2(1, nch - 1 - i, st[1], off))
        states = lax.fori_loop(0, nch, p2_body, states)

    ng = ng_ref[...]
    for off, n_rows, g_ref, o_ref in ((0, n_ctx, gc_ref, oc_ref), (n_ctx, n_lat, gl_ref, ol_ref)):
        t = min(tbs, n_rows)
        for r0 in range(0, n_rows, t):
            o = oacc[off + r0:off + r0 + t, :]
            y = o * lax.rsqrt(jnp.mean(o * o, axis=-1, keepdims=True) + EPS) * ng
            o_ref[0, r0:r0 + t, :] = (y * _silu(g_ref[0, r0:r0 + t, :].astype(F32))).astype(BF16)


def _dn_call(main_l, main_c, col_l, row_l, col_c, row_c, conv_w, norm_g, col0):
    b, l, _ = main_l.shape
    lc = main_c.shape[1]
    heads = col_l.shape[1]
    lt = l + lc
    nch = lt // DN_CHUNK
    cwt = conv_w.T
    group = 8

    def blk(n_rows, which):
        return pl.BlockSpec((1, n_rows, LANES), lambda bi, h, which=which: (bi, 0, col0 + which * heads + h))

    def cw(which):
        return pl.BlockSpec((SHORT_CONV, LANES), lambda bi, h, which=which: (0, which * heads + h))

    in_specs = ([blk(l, w) for w in range(4)] + [blk(lc, w) for w in range(4)] + [cw(0), cw(1), cw(2)] +
                [pl.BlockSpec((1, 1, l, 4), lambda bi, h: (bi, h, 0, 0)),
                 pl.BlockSpec((1, 1, l // DN_CHUNK, 4, DN_CHUNK), lambda bi, h: (bi, h, 0, 0, 0)),
                 pl.BlockSpec((1, 1, lc, 4), lambda bi, h: (bi, h, 0, 0)),
                 pl.BlockSpec((1, 1, lc // DN_CHUNK, 4, DN_CHUNK), lambda bi, h: (bi, h, 0, 0, 0)),
                 pl.BlockSpec((1, LANES), lambda bi, h: (0, 0))])
    return pl.pallas_call(
        functools.partial(_dn_kernel, group=group),
        grid=(b, heads),
        in_specs=in_specs,
        out_specs=[pl.BlockSpec((1, l, LANES), lambda bi, h: (bi, 0, h)),
                   pl.BlockSpec((1, lc, LANES), lambda bi, h: (bi, 0, h))],
        out_shape=[jax.ShapeDtypeStruct((b, l, heads * LANES), BF16),
                   jax.ShapeDtypeStruct((b, lc, heads * LANES), BF16)],
        scratch_shapes=[pltpu.VMEM((l + 2 * SUBLANES, LANES), F32),
                        pltpu.VMEM((lt, LANES), F32),
                        pltpu.VMEM((lt, LANES), F32),
                        pltpu.VMEM((lt, LANES), F32),
                        pltpu.VMEM((lt, LANES), F32),
                        pltpu.VMEM((2 * group, DN_CHUNK, DN_CHUNK), BF16),
                        pltpu.VMEM((2 * group, DN_CHUNK, 2 * LANES), F32),
                        pltpu.VMEM((2 * group, DN_CHUNK, DN_CHUNK), BF16),
                        pltpu.VMEM((2, nch, DN_DK, LANES), BF16),
                        pltpu.VMEM((2, nch, DN_DK, LANES), BF16),
                        pltpu.VMEM((2, lt, LANES), BF16),
                        pltpu.VMEM((2, nch, LANES), F32)],
        compiler_params=_params("parallel", "parallel"),
        name="gated_deltanet",
    )(main_l, main_l, main_l, main_l, main_c, main_c, main_c, main_c, cwt, cwt, cwt,
      col_l, row_l, col_c, row_c, norm_g.reshape(1, LANES))


def _ret_kernel(th_ref, ql_ref, kl_ref, vl_ref, qc_ref, kc_ref, vc_ref, cos_ref, sin_ref, o_ref,
                qs, kts, s_ref, dec_ref, xz_ref):
    n_lat, n_ctx = ql_ref.shape[1], qc_ref.shape[1]
    c = RET_CHUNK
    half = RET_DK // 2
    ncc, nlc = n_ctx // c, n_lat // c
    scale = RET_DK ** -0.5

    icol = lax.broadcasted_iota(jnp.int32, (c, RET_DV), 0)
    ii = lax.broadcasted_iota(jnp.int32, (c, c), 0)
    jj = lax.broadcasted_iota(jnp.int32, (c, c), 1)
    gch = []
    for d in range(2):
        lg = -jnp.exp(th_ref[d, 0])
        lg1 = lg[:, 0:1]
        fidx = (icol if d == 0 else c - 1 - icol).astype(F32)
        rel = (ii - jj) if d == 0 else (jj - ii)
        mask = rel >= 0
        dec_ref[d] = jnp.where(mask, jnp.exp(jnp.where(mask, rel, 0).astype(F32) * lg), 0.0)
        xz_ref[d, 0] = jnp.exp((fidx + 1.0) * lg1)
        xz_ref[d, 1] = jnp.exp((c - 1.0 - fidx) * lg1)
        gch.append(jnp.exp(c * lg1))
    s_ref[...] = jnp.zeros_like(s_ref)

    def prep_ctx(ci, carry):
        r0 = pl.multiple_of(ci * c, c)
        qs[pl.ds(r0, c), :] = qc_ref[0, pl.ds(r0, c), :].astype(BF16)
        k = kc_ref[0, pl.ds(r0, c), :].astype(F32) * scale
        kts[ci] = k.T.astype(BF16)
        return carry

    def prep_lat(ci, carry):
        r0 = pl.multiple_of(ci * c, c)
        cos = cos_ref[pl.ds(r0, c), :]
        sin = sin_ref[pl.ds(r0, c), :]

        def rope(t):
            t1, t2 = t[:, 0:half], t[:, half:2 * half]
            return jnp.concatenate([t1 * cos - t2 * sin, t1 * sin + t2 * cos], axis=-1)

        q = rope(ql_ref[0, pl.ds(r0, c), :].astype(F32))
        k = rope(kl_ref[0, pl.ds(r0, c), :].astype(F32)) * scale
        qs[pl.ds(pl.multiple_of(n_ctx + ci * c, c), c), :] = q.astype(BF16)
        kts[ncc + ci] = k.T.astype(BF16)
        return carry

    lax.fori_loop(0, ncc, prep_ctx, 0)
    lax.fori_loop(0, nlc, prep_lat, 0)

    def one(d, ci, row_off, ch_off, v_ref, write, first):
        r0 = pl.multiple_of(ci * c, c)
        qb = qs[pl.ds(pl.multiple_of(row_off + ci * c, c), c), :]
        kt = kts[ch_off + ci]
        v = v_ref[0, pl.ds(r0, c), :].astype(BF16)
        state = s_ref[d]
        if write:
            inner = _dot(qb, kt) * dec_ref[d]
            o = _dot(inner.astype(BF16), v) + xz_ref[d, 0] * _dot(qb, state.astype(BF16))
            if first:
                o_ref[0, pl.ds(r0, c), :] = o
            else:
                o_ref[0, pl.ds(r0, c), :] += o
        vz = (v.astype(F32) * xz_ref[d, 1]).astype(BF16)
        s_ref[d] = state * gch[d] + _dot(kt, vz)

    def ctx_body(i, carry):
        one(0, i, 0, 0, vc_ref, False, False)
        one(1, ncc - 1 - i, 0, 0, vc_ref, False, False)
        return carry

    lax.fori_loop(0, ncc, ctx_body, 0)
    for lo, hi, first in ((0, nlc // 2, True), (nlc // 2, nlc, False)):
        def lat_body(i, carry, first=first):
            one(0, i, n_ctx, ncc, vl_ref, True, first)
            one(1, nlc - 1 - i, n_ctx, ncc, vl_ref, True, first)
            return carry
        lax.fori_loop(lo, hi, lat_body, 0)


def _ret_call(main_l, main_c, theta, cos, sin, q_col0):
    b, l, _ = main_l.shape
    lc = main_c.shape[1]
    heads = theta.shape[1]
    c = RET_CHUNK
    qb0 = q_col0 // RET_DK
    kb0 = qb0 + heads
    vb0 = (q_col0 + 2 * heads * RET_DK) // RET_DV
    th = jnp.broadcast_to(theta.reshape(2, heads, 1, 1), (2, heads, 1, LANES))
    return pl.pallas_call(
        _ret_kernel,
        grid=(b, heads),
        in_specs=[pl.BlockSpec((2, 1, 1, LANES), lambda bi, h: (0, h, 0, 0)),
                  pl.BlockSpec((1, l, RET_DK), lambda bi, h: (bi, 0, qb0 + h)),
                  pl.BlockSpec((1, l, RET_DK), lambda bi, h: (bi, 0, kb0 + h)),
                  pl.BlockSpec((1, l, RET_DV), lambda bi, h: (bi, 0, vb0 + h)),
                  pl.BlockSpec((1, lc, RET_DK), lambda bi, h: (bi, 0, qb0 + h)),
                  pl.BlockSpec((1, lc, RET_DK), lambda bi, h: (bi, 0, kb0 + h)),
                  pl.BlockSpec((1, lc, RET_DV), lambda bi, h: (bi, 0, vb0 + h)),
                  pl.BlockSpec((l, RET_DK // 2), lambda bi, h: (0, 0)),
                  pl.BlockSpec((l, RET_DK // 2), lambda bi, h: (0, 0))],
        out_specs=pl.BlockSpec((1, l, RET_DV), lambda bi, h: (bi, 0, h)),
        out_shape=jax.ShapeDtypeStruct((b, l, heads * RET_DV), F32),
        scratch_shapes=[pltpu.VMEM((lc + l, RET_DK), BF16),
                        pltpu.VMEM(((lc + l) // c, RET_DK, c), BF16),
                        pltpu.VMEM((2, RET_DK, RET_DV), F32),
                        pltpu.VMEM((2, c, c), F32),
                        pltpu.VMEM((2, 2, c, RET_DV), F32)],
        compiler_params=_params("parallel", "parallel"),
        name="retention",
    )(th, main_l, main_l, main_l, main_c, main_c, main_c, cos, sin)


def _ret_finish_kernel(o_ref, r_ref, y_ref):
    o = o_ref[0]
    y = o * lax.rsqrt(jnp.mean(o * o, axis=-1, keepdims=True) + EPS)
    y_ref[0] = (y * _silu(r_ref[0].astype(F32))).astype(BF16)


def _ret_finish_call(o, main_l, r_col0):
    b, l, w = o.shape
    heads = w // RET_DV
    tm = min(l, 1024)
    rb0 = r_col0 // RET_DV
    return pl.pallas_call(
        _ret_finish_kernel,
        grid=(b, l // tm, heads),
        in_specs=[pl.BlockSpec((1, tm, RET_DV), lambda bi, i, h: (bi, i, h)),
                  pl.BlockSpec((1, tm, RET_DV), lambda bi, i, h: (bi, i, rb0 + h))],
        out_specs=pl.BlockSpec((1, tm, RET_DV), lambda bi, i, h: (bi, i, h)),
        out_shape=jax.ShapeDtypeStruct((b, l, w), BF16),
        compiler_params=_params("parallel", "parallel", "parallel"),
        name="retention_finish",
    )(o, main_l)


def _rope_tables(n_tokens):
    rows = n_tokens // GRID_W
    r, col = jnp.meshgrid(jnp.arange(rows), jnp.arange(GRID_W), indexing='ij')
    n_freq = RET_DK // 4
    inv = ROPE_BASE ** (-jnp.arange(n_freq, dtype=F32) / n_freq)
    ang = jnp.concatenate([r.reshape(-1, 1) * inv, col.reshape(-1, 1) * inv], axis=-1)
    return jnp.cos(ang), jnp.sin(ang)


def _even_layer(xl, xc, mods_l, mods_c, norm1_g, norm2_g, w_in, w_out, s5p, glu_w, glu_b,
                dn_conv_w, dn_a_log, dn_dt_bias, dn_norm_g, ffn, final_g, final_norm):
    s5_width = glu_w.shape[0]
    dn_width = dn_conv_w.shape[0] // 3
    heads = dn_width // LANES
    n_main = s5_width + 4 * dn_width
    w_main = w_in[:, :n_main].astype(BF16)
    w_small = jnp.pad(w_in[:, n_main:], ((0, 0), (0, LANES - (w_in.shape[1] - n_main)))).astype(BF16)

    main_l, ab_l = _inproj_call(xl, norm1_g, mods_l[0], mods_l[1], w_main, w_small)
    main_c, ab_c = _inproj_call(xc, norm1_g, mods_c[0], mods_c[1], w_main, w_small)

    y_l, y_c = _s5_scan_call(main_l, main_c, *s5p)
    glu_wb = glu_w.astype(BF16)
    s5_l = _s5_glu_call(y_l, glu_wb, glu_b)
    s5_c = _s5_glu_call(y_c, glu_wb, glu_b)

    col_l, row_l = _dn_gate_layouts(_dn_gate_call(ab_l, dn_a_log, dn_dt_bias), heads)
    col_c, row_c = _dn_gate_layouts(_dn_gate_call(ab_c, dn_a_log, dn_dt_bias), heads)
    dn_l, dn_c = _dn_call(main_l, main_c, col_l, row_l, col_c, row_c, dn_conv_w, dn_norm_g,
                          s5_width // LANES)

    w_o1 = w_out[:s5_width].astype(BF16)
    w_o2 = w_out[s5_width:].astype(BF16)
    xl = _outproj_call(xl, mods_l[2], s5_l, dn_l, w_o1, w_o2)
    xc = _outproj_call(xc, mods_c[2], s5_c, dn_c, w_o1, w_o2)
    w1, w3, w2 = ffn
    xl = _ffn_call(xl, norm2_g, mods_l[3], mods_l[4], mods_l[5], w1, w3, w2, final_g, final_norm)
    xc = _ffn_call(xc, norm2_g, mods_c[3], mods_c[4], mods_c[5], w1, w3, w2, final_g, False)
    return xl, xc


def _odd_layer(xl, xc, mods_l, mods_c, norm1_g, norm2_g, w_in, w_out, conv_w, conv_b, wa, ba, wx, bx, lam,
               theta, cos, sin, ffn, final_g, final_norm):
    lru_width = conv_b.shape[0]
    heads = theta.shape[1]
    w_inb = w_in.astype(BF16)
    main_l = _inproj_call(xl, norm1_g, mods_l[0], mods_l[1], w_inb)
    main_c = _inproj_call(xc, norm1_g, mods_c[0], mods_c[1], w_inb)

    lru_l = _lru_call(main_l, main_c, conv_w, conv_b, wa.astype(BF16), ba, wx.astype(BF16), bx, lam)
    q_col0 = 2 * lru_width
    o = _ret_call(main_l, main_c, theta, cos, sin, q_col0)
    ret_l = _ret_finish_call(o, main_l, q_col0 + 2 * heads * RET_DK + heads * RET_DV)

    w_o1 = w_out[:lru_width].astype(BF16)
    w_o2 = w_out[lru_width:].astype(BF16)
    xl = _outproj_call(xl, mods_l[2], lru_l, ret_l, w_o1, w_o2)
    w1, w3, w2 = ffn
    return _ffn_call(xl, norm2_g, mods_l[3], mods_l[4], mods_l[5], w1, w3, w2, final_g, final_norm)


def kernel(x, c, ctx, c_ctx, mod_w, mod_b, norm1_g, norm2_g, ffn_w1, ffn_w3, ffn_w2, final_g, even_w_in, even_w_out, s5_lam_re, s5_lam_im, s5_log_step, s5_b_re, s5_b_im, s5_c_re, s5_c_im, s5_d, s5_glu_w, s5_glu_b, dn_conv_w, dn_a_log, dn_dt_bias, dn_norm_g, odd_w_in, odd_w_out, lru_conv_w, lru_conv_b, lru_wa, lru_ba, lru_wx, lru_bx, lru_lam, ret_theta):
    bsz, n_tok, d = x.shape
    depth = mod_w.shape[0]
    assert depth == 2 and bsz + 1 <= SUBLANES
    cos, sin = _rope_tables(n_tok)

    rows = jnp.concatenate([c, c_ctx[None, :], jnp.zeros((SUBLANES - bsz - 1, d), F32)], axis=0)
    mods = _mod_call(rows, mod_w, mod_b)

    def split_mods(i):
        m = mods[i].reshape(SUBLANES, 6, d)
        ml = [m[:bsz, k][:, None, :] for k in range(6)]
        mc = [jnp.broadcast_to(m[bsz, k][None, None, :], (bsz, 1, d)) for k in range(6)]
        return ml, mc

    xl, xc = x, ctx
    ml, mc = split_mods(0)
    s5p = _s5_weights(s5_lam_re[0], s5_lam_im[0], s5_log_step[0], s5_b_re[0], s5_b_im[0],
                      s5_c_re[0], s5_c_im[0], s5_d[0])
    ffn0 = (ffn_w1[0].astype(BF16), ffn_w3[0].astype(BF16), ffn_w2[0].astype(BF16))
    xl, xc = _even_layer(xl, xc, ml, mc, norm1_g[0], norm2_g[0], even_w_in[0], even_w_out[0],
                         s5p, s5_glu_w[0], s5_glu_b[0],
                         dn_conv_w[0], dn_a_log[0], dn_dt_bias[0], dn_norm_g[0], ffn0, final_g, False)
    ml, mc = split_mods(1)
    ffn1 = (ffn_w1[1].astype(BF16), ffn_w3[1].astype(BF16), ffn_w2[1].astype(BF16))
    return _odd_layer(xl, xc, ml, mc, norm1_g[1], norm2_g[1], odd_w_in[0], odd_w_out[0],
                      lru_conv_w[0], lru_conv_b[0], lru_wa[0], lru_ba[0], lru_wx[0], lru_bx[0], lru_lam[0],
                      ret_theta[0], cos, sin, ffn1, final_g, True)
```

```python
import functools
import math

import jax
import jax.numpy as jnp
from jax import lax
from jax.experimental import pallas as pl
from jax.experimental.pallas import tpu as pltpu

F32 = jnp.float32
BF16 = jnp.bfloat16

EPS = 1e-6
GRID_W = 64
ROPE_BASE = 10000.0

S5_GROUP = 16
S5_STATE = 64
S5_BLOCK = 8

DN_DK = 128
DN_CHUNK = 64
SHORT_CONV = 4

LRU_BLOCK = 128
LRU_C = 8.0

RET_DK = 256
RET_DV = 512
RET_CHUNK = 128

LANES = 128
SUBLANES = 8
VMEM_LIMIT = 56 * 1024 * 1024


def _params(*sem):
    return pltpu.CompilerParams(dimension_semantics=sem, vmem_limit_bytes=VMEM_LIMIT)


def _silu(x):
    return x * jax.nn.sigmoid(x)


def _gelu(x):
    return 0.5 * x * (1.0 + jnp.tanh(0.7978845608028654 * (x + 0.044715 * (x * x * x))))


def _softplus(x):
    return jnp.maximum(x, 0.0) + jnp.log1p(jnp.exp(-jnp.abs(x)))


def _sigmoid(x):
    return 0.5 * (1.0 + jnp.tanh(0.5 * x))


def _expm1_given_exp(x, u):
    safe = jnp.where(u == 1.0, 2.0, u)
    return jnp.where(u == 1.0, x, (u - 1.0) * x / jnp.where(u == 0.0, x, jnp.log(safe)))


def _adaln(x, g, shift, scale):
    y = x * lax.rsqrt(jnp.mean(x * x, axis=-1, keepdims=True) + EPS)
    return (y * g) * (1.0 + scale) + shift


def _dot(a, b):
    return jnp.dot(a, b, preferred_element_type=F32)


def _dot_nt(a, b):
    return lax.dot_general(a, b, (((1,), (1,)), ((), ())), preferred_element_type=F32)


def _mod_kernel(s_ref, w_ref, b_ref, o_ref):
    s = _silu(s_ref[...])
    o_ref[0] = _dot(s.astype(BF16), w_ref[0].astype(BF16)) + b_ref[0]


def _mod_call(rows, mod_w, mod_b):
    depth, d, n = mod_w.shape
    tn = 1024
    return pl.pallas_call(
        _mod_kernel,
        grid=(depth, n // tn),
        in_specs=[pl.BlockSpec((SUBLANES, d), lambda i, j: (0, 0)),
                  pl.BlockSpec((1, d, tn), lambda i, j: (i, 0, j)),
                  pl.BlockSpec((1, 1, tn), lambda i, j: (i, 0, j))],
        out_specs=pl.BlockSpec((1, SUBLANES, tn), lambda i, j: (i, 0, j)),
        out_shape=jax.ShapeDtypeStruct((depth, SUBLANES, n), F32),
        compiler_params=_params("parallel", "parallel"),
        name="mod_proj",
    )(rows, mod_w, mod_b.reshape(depth, 1, n))


def _inproj_kernel(x_ref, g_ref, sh_ref, sc_ref, w_ref, *rest, has_small):
    if has_small:
        ws_ref, o_ref, os_ref, h_ref = rest
    else:
        o_ref, h_ref = rest

    @pl.when(pl.program_id(2) == 0)
    def _():
        h = _adaln(x_ref[0], g_ref[...], sh_ref[0], sc_ref[0]).astype(BF16)
        h_ref[...] = h
        if has_small:
            os_ref[0] = _dot(h, ws_ref[...])

    o_ref[0] = _dot(h_ref[...], w_ref[...]).astype(o_ref.dtype)


def _inproj_call(x, g, shift, scale, w, w_small=None):
    b, l, d = x.shape
    n = w.shape[1]
    tm = min(l, 1024)
    tn = 512
    has_small = w_small is not None
    in_specs = [pl.BlockSpec((1, tm, d), lambda bi, i, j: (bi, i, 0)),
                pl.BlockSpec((1, d), lambda bi, i, j: (0, 0)),
                pl.BlockSpec((1, 1, d), lambda bi, i, j: (bi, 0, 0)),
                pl.BlockSpec((1, 1, d), lambda bi, i, j: (bi, 0, 0)),
                pl.BlockSpec((d, tn), lambda bi, i, j: (0, j))]
    out_specs = [pl.BlockSpec((1, tm, tn), lambda bi, i, j: (bi, i, j))]
    out_shape = [jax.ShapeDtypeStruct((b, l, n), BF16)]
    args = [x, g.reshape(1, d), shift, scale, w]
    if has_small:
        in_specs.append(pl.BlockSpec((d, LANES), lambda bi, i, j: (0, 0)))
        out_specs.append(pl.BlockSpec((1, tm, LANES), lambda bi, i, j: (bi, i, 0)))
        out_shape.append(jax.ShapeDtypeStruct((b, l, LANES), F32))
        args.append(w_small)
    outs = pl.pallas_call(
        functools.partial(_inproj_kernel, has_small=has_small),
        grid=(b, l // tm, n // tn),
        in_specs=in_specs, out_specs=out_specs, out_shape=out_shape,
        scratch_shapes=[pltpu.VMEM((tm, d), BF16)],
        compiler_params=_params("parallel", "parallel", "arbitrary"),
        name="adaln_inproj",
    )(*args)
    return outs if has_small else outs[0]


def _outproj_kernel(x_ref, gate_ref, a1_ref, a2_ref, w1_ref, w2_ref, o_ref):
    y = _dot(a1_ref[0], w1_ref[...]) + _dot(a2_ref[0], w2_ref[...])
    o_ref[0] = x_ref[0] + gate_ref[0] * y


def _outproj_call(x, gate, a1, a2, w1, w2):
    b, l, d = x.shape
    k1, k2 = a1.shape[2], a2.shape[2]
    tm = min(l, 1024)
    tn = 512
    return pl.pallas_call(
        _outproj_kernel,
        grid=(b, l // tm, d // tn),
        in_specs=[pl.BlockSpec((1, tm, tn), lambda bi, i, j: (bi, i, j)),
                  pl.BlockSpec((1, 1, tn), lambda bi, i, j: (bi, 0, j)),
                  pl.BlockSpec((1, tm, k1), lambda bi, i, j: (bi, i, 0)),
                  pl.BlockSpec((1, tm, k2), lambda bi, i, j: (bi, i, 0)),
                  pl.BlockSpec((k1, tn), lambda bi, i, j: (0, j)),
                  pl.BlockSpec((k2, tn), lambda bi, i, j: (0, j))],
        out_specs=pl.BlockSpec((1, tm, tn), lambda bi, i, j: (bi, i, j)),
        out_shape=jax.ShapeDtypeStruct((b, l, d), F32),
        compiler_params=_params("parallel", "parallel", "arbitrary"),
        name="outproj_residual",
    )(x, gate, a1, a2, w1, w2)


def _ffn_kernel(x_ref, g_ref, sh_ref, sc_ref, gate_ref, w1_ref, w3_ref, w2_ref, fg_ref, o_ref,
                h_ref, acc_ref, *, final_norm):
    f = pl.program_id(2)

    @pl.when(f == 0)
    def _():
        h_ref[...] = _adaln(x_ref[0], g_ref[...], sh_ref[0], sc_ref[0]).astype(BF16)
        acc_ref[...] = jnp.zeros_like(acc_ref)

    h = h_ref[...]
    a = _dot(h, w1_ref[...])
    b = _dot(h, w3_ref[...])
    acc_ref[...] += _dot((_silu(a) * b).astype(BF16), w2_ref[...])

    @pl.when(f == pl.num_programs(2) - 1)
    def _():
        y = x_ref[0] + gate_ref[0] * acc_ref[...]
        if final_norm:
            y = y * lax.rsqrt(jnp.mean(y * y, axis=-1, keepdims=True) + EPS) * fg_ref[...]
        o_ref[0] = y


def _ffn_call(x, g, shift, scale, gate, w1, w3, w2, final_g, final_norm):
    b, l, d = x.shape
    dff = w1.shape[1]
    tm = min(l, 512)
    tf = 512
    vec = pl.BlockSpec((1, 1, d), lambda bi, i, f: (bi, 0, 0))
    return pl.pallas_call(
        functools.partial(_ffn_kernel, final_norm=final_norm),
        grid=(b, l // tm, dff // tf),
        in_specs=[pl.BlockSpec((1, tm, d), lambda bi, i, f: (bi, i, 0)),
                  pl.BlockSpec((1, d), lambda bi, i, f: (0, 0)),
                  vec, vec, vec,
                  pl.BlockSpec((d, tf), lambda bi, i, f: (0, f)),
                  pl.BlockSpec((d, tf), lambda bi, i, f: (0, f)),
                  pl.BlockSpec((tf, d), lambda bi, i, f: (f, 0)),
                  pl.BlockSpec((1, d), lambda bi, i, f: (0, 0))],
        out_specs=pl.BlockSpec((1, tm, d), lambda bi, i, f: (bi, i, 0)),
        out_shape=jax.ShapeDtypeStruct((b, l, d), F32),
        scratch_shapes=[pltpu.VMEM((tm, d), BF16), pltpu.VMEM((tm, d), F32)],
        compiler_params=_params("parallel", "parallel", "arbitrary"),
        name="ffn_swiglu",
    )(x, g.reshape(1, d), shift, scale, gate, w1, w3, w2, final_g.reshape(1, d))


def _s5_disc_kernel(lre_ref, lim_ref, ls_ref, bre_ref, bim_ref, are_ref, aim_ref, bbre_ref, bbim_ref):
    lre = jnp.minimum(lre_ref[...], -1e-4)
    lim = lim_ref[...]
    dt = jnp.exp(ls_ref[...])
    mag = jnp.exp(lre * dt)
    ar = mag * jnp.cos(lim * dt)
    ai = mag * jnp.sin(lim * dt)
    nr, ni = ar - 1.0, ai
    den = lre * lre + lim * lim
    cr = (nr * lre + ni * lim) / den
    ci = (ni * lre - nr * lim) / den
    bre, bim = bre_ref[...], bim_ref[...]
    are_ref[...] = ar
    aim_ref[...] = ai
    bbre_ref[...] = cr * bre - ci * bim
    bbim_ref[...] = cr * bim + ci * bre


def _s5_taps_kernel(ar_ref, ai_ref, br_ref, bi_ref, cr_ref, ci_ref,
                    k_ref, dr_ref, di_ref, gr_ref, gi_ref, apw_ref, ccr, cci, *, reverse_from):
    t_blk = S5_BLOCK
    s = S5_GROUP
    ar, ai = ar_ref[0], ai_ref[0]
    br, bi = br_ref[0], bi_ref[0]
    cr, ci = cr_ref[0], ci_ref[0]
    backward = pl.program_id(0) >= reverse_from
    pw = [(jnp.ones_like(ar), jnp.zeros_like(ar))]
    for _ in range(t_blk):
        pr, pi = pw[-1]
        pw.append((pr * ar - pi * ai, pr * ai + pi * ar))
    for t in range(t_blk):
        pr, pi = pw[t]
        ccr[t * s:(t + 1) * s, :] = cr * pr - ci * pi
        cci[t * s:(t + 1) * s, :] = cr * pi + ci * pr
        dfr, dfi = pw[t_blk - 1 - t]
        dpr = jnp.where(backward, pr, dfr)
        dpi = jnp.where(backward, pi, dfi)
        dr_ref[0, t * s:(t + 1) * s, :] = br * dpr - bi * dpi
        di_ref[0, t * s:(t + 1) * s, :] = br * dpi + bi * dpr
        gfr, gfi = pw[t + 1]
        gbr, gbi = pw[t_blk - t]
        gpr = jnp.where(backward, gbr, gfr)
        gpi = jnp.where(backward, gbi, gfi)
        gr_ref[0, t * s:(t + 1) * s, :] = cr * gpr - ci * gpi
        gi_ref[0, t * s:(t + 1) * s, :] = -(cr * gpi + ci * gpr)
    pad = jnp.zeros((LANES - s, br.shape[1]), F32)
    hi = lax.Precision.HIGHEST
    brp = jnp.concatenate([br, pad], axis=0)
    bip = jnp.concatenate([bi, pad], axis=0)
    nt = (((1,), (1,)), ((), ()))
    k_ref[0] = (lax.dot_general(ccr[...], brp, nt, precision=hi, preferred_element_type=F32)
                - lax.dot_general(cci[...], bip, nt, precision=hi, preferred_element_type=F32))
    qr, qi = pw[t_blk]
    er, ei = jnp.ones_like(qr), jnp.zeros_like(qr)
    for kk in range(9):
        apw_ref[0, 2 * kk:2 * kk + 1, :] = jnp.concatenate([er, er], axis=1)
        apw_ref[0, 2 * kk + 1:2 * kk + 2, :] = jnp.concatenate([-ei, ei], axis=1)
        er, ei = er * qr - ei * qi, er * qi + ei * qr


def _s5_weights(lam_re, lam_im, log_step, b_re, b_im, c_re, c_im):
    nd, g, p = lam_re.shape
    s = b_re.shape[-1]
    t_blk = S5_BLOCK
    rows = nd * g * s
    rep = lambda t: jnp.repeat(t.reshape(nd * g, p), s, axis=0)
    ls = jnp.broadcast_to(log_step.reshape(nd * g, 1), (nd * g, p))
    tb = lambda t: jnp.transpose(t, (0, 1, 3, 2)).reshape(rows, p)
    shp = jax.ShapeDtypeStruct((rows, p), F32)
    are, aim, bbre, bbim = pl.pallas_call(
        _s5_disc_kernel, out_shape=[shp, shp, shp, shp], name="s5_discretise",
    )(rep(lam_re), rep(lam_im), rep(ls), tb(b_re), tb(b_im))
    n = nd * g
    ts = t_blk * s
    vec = pl.BlockSpec((1, 1, p), lambda i: (i, 0, 0))
    mat = pl.BlockSpec((1, s, p), lambda i: (i, 0, 0))
    big = pl.BlockSpec((1, ts, p), lambda i: (i, 0, 0))
    bigshape = jax.ShapeDtypeStruct((n, ts, p), F32)
    kraw, dr, di, gr, gi, apw = pl.pallas_call(
        functools.partial(_s5_taps_kernel, reverse_from=g),
        grid=(n,),
        in_specs=[vec, vec, mat, mat, mat, mat],
        out_specs=[pl.BlockSpec((1, ts, LANES), lambda i: (i, 0, 0)), big, big, big, big,
                   pl.BlockSpec((1, 18, 2 * p), lambda i: (i, 0, 0))],
        out_shape=[jax.ShapeDtypeStruct((n, ts, LANES), F32), bigshape, bigshape, bigshape, bigshape,
                   jax.ShapeDtypeStruct((n, 18, 2 * p), F32)],
        scratch_shapes=[pltpu.VMEM((ts, p), F32), pltpu.VMEM((ts, p), F32)],
        compiler_params=_params("parallel"),
        name="s5_block_taps",
    )(are[::s].reshape(n, 1, p), aim[::s].reshape(n, 1, p), bbre.reshape(n, s, p), bbim.reshape(n, s, p),
      c_re.reshape(n, s, p), c_im.reshape(n, s, p))

    gs = LANES // s
    nslab = g // gs
    eye = jnp.eye(gs, dtype=F32)
    taps = kraw[:, :, :s].reshape(nd, g, t_blk, s, s)
    ti = jnp.arange(t_blk)
    lag = ti[None, :] - ti[:, None]

    def toeplitz(tp, lag):
        m = tp[:, jnp.clip(lag, 0, t_blk - 1)]
        m = jnp.where((lag >= 0)[None, :, :, None, None], m, 0.0)
        return jnp.transpose(m, (0, 1, 4, 2, 3))

    kg = (toeplitz(taps[0], lag) + toeplitz(taps[1], -lag)).reshape(nslab, gs, t_blk, s, t_blk, s)
    ktot = jnp.einsum('agtsuo,gh->atgsuho', kg, eye).reshape(nslab, t_blk * LANES, t_blk * LANES)

    def drive(d):
        dd = jnp.stack([dr.reshape(nd, nslab, gs, t_blk, s, p)[d], di.reshape(nd, nslab, gs, t_blk, s, p)[d]], 0)
        return jnp.einsum('cagtsp,gh->atgschp', dd, eye).reshape(nslab, t_blk * LANES, 2 * gs * p)

    w1 = jnp.concatenate([ktot, drive(0), drive(1)], axis=-1).astype(BF16)
    gg = jnp.stack([gr.reshape(nd, nslab, gs, t_blk, s, p), gi.reshape(nd, nslab, gs, t_blk, s, p)], 1)
    w2 = jnp.einsum('dcagtop,gh->adcgptho', gg, eye).reshape(nslab, 2 * 2 * gs * p, t_blk * LANES).astype(BF16)
    ap = apw.reshape(nd, nslab, gs, 18, 2 * p)
    lo = jnp.transpose(ap[..., :p], (1, 0, 3, 2, 4)).reshape(nslab, nd, 18, gs * p)
    hi_ = jnp.transpose(ap[..., p:], (1, 0, 3, 2, 4)).reshape(nslab, nd, 18, gs * p)
    return w1, w2, jnp.concatenate([lo, hi_], axis=-1)


def _s5_kernel(ul_ref, uc_ref, w1_ref, w2_ref, apw_ref, dsk_ref, yl_ref, yc_ref,
               uf, ubuf, zbuf, hbuf, sbuf):
    t_blk = S5_BLOCK
    n_lat, n_ctx = ul_ref.shape[1], uc_ref.shape[1]
    nb_ctx, nb_lat = n_ctx // t_blk, n_lat // t_blk
    nb = nb_ctx + nb_lat
    ts = t_blk * LANES
    hw = apw_ref.shape[3]
    rows = 512

    for src, off, n in ((uc_ref, 0, n_ctx), (ul_ref, n_ctx, n_lat)):
        for r0 in range(0, n, min(rows, n)):
            r1 = min(r0 + rows, n)
            uf[off + r0:off + r1, :] = src[0, r0:r1, :].astype(F32)
    for t in range(t_blk):
        ubuf[:, t * LANES:(t + 1) * LANES] = uf[pl.ds(t, nb, stride=t_blk), :].astype(BF16)
    zbuf[...] = _dot(ubuf[...], w1_ref[0])

    row = lax.broadcasted_iota(jnp.int32, (SUBLANES, hw), 0)

    def cmul(x, p1, p2):
        return x * p1 + pltpu.roll(x, hw // 2, 1) * p2

    for d in range(2):
        col = ts + d * hw
        pw = lambda k, d=d: (apw_ref[0, d, 2 * k:2 * k + 1, :], apw_ref[0, d, 2 * k + 1:2 * k + 2, :])

        def local(i, carry, d=d, col=col, pw=pw):
            r0 = pl.multiple_of(i * SUBLANES, SUBLANES)
            x = zbuf[pl.ds(r0, SUBLANES), col:col + hw]
            for k in (1, 2, 4):
                keep = (row >= k) if d == 0 else (row <= SUBLANES - 1 - k)
                sh = jnp.where(keep, pltpu.roll(x, k if d == 0 else SUBLANES - k, 0), 0.0)
                p1, p2 = pw(k)
                x = x + cmul(sh, p1, p2)
            sbuf[pl.ds(r0, SUBLANES), :] = x
            return carry
        lax.fori_loop(0, nb // SUBLANES, local, 0)

        pex1 = jnp.zeros((SUBLANES, hw), F32)
        pex2 = jnp.zeros((SUBLANES, hw), F32)
        for r in range(SUBLANES):
            p1, p2 = pw(r if d == 0 else SUBLANES - 1 - r)
            pex1 = jnp.where(row == r, p1, pex1)
            pex2 = jnp.where(row == r, p2, pex2)
        p81, p82 = pw(SUBLANES)
        keep1 = (row >= 1) if d == 0 else (row <= SUBLANES - 2)
        last = SUBLANES - 1 if d == 0 else 0
        carry = jnp.zeros((SUBLANES, hw), F32)
        for seg0, ntile in ((0, nb_ctx // SUBLANES), (nb_ctx // SUBLANES, nb_lat // SUBLANES)):
            def chain(i, carry, seg0=seg0, ntile=ntile, d=d):
                ti = seg0 + (i if d == 0 else ntile - 1 - i)
                r0 = pl.multiple_of(ti * SUBLANES, SUBLANES)
                s_t = sbuf[pl.ds(r0, SUBLANES), :]
                excl = jnp.where(keep1, pltpu.roll(s_t, 1 if d == 0 else SUBLANES - 1, 0), 0.0)
                hbuf[pl.ds(r0, SUBLANES), d * hw:(d + 1) * hw] = excl + cmul(carry, pex1, pex2)
                return jnp.broadcast_to(s_t[last:last + 1], (SUBLANES, hw)) + cmul(carry, p81, p82)
            carry = lax.fori_loop(0, ntile, chain, carry)

    zbuf[:, 0:ts] += _dot(hbuf[...].astype(BF16), w2_ref[0])
    dsk = dsk_ref[...]
    for t in range(t_blk):
        y_t = zbuf[:, t * LANES:(t + 1) * LANES] + dsk * uf[pl.ds(t, nb, stride=t_blk), :]
        yc_ref[0, pl.ds(t, nb_ctx, stride=t_blk), :] = y_t[0:nb_ctx]
        yl_ref[0, pl.ds(t, nb_lat, stride=t_blk), :] = y_t[nb_ctx:nb]


def _s5_scan_call(main_l, main_c, w1, w2, apw, d_skip):
    b, l, _ = main_l.shape
    lc = main_c.shape[1]
    nslab, ts, n1 = w1.shape
    hw = apw.shape[3]
    width = d_skip.shape[0]
    nb = (l + lc) // S5_BLOCK
    return pl.pallas_call(
        _s5_kernel,
        grid=(nslab, b),
        in_specs=[pl.BlockSpec((1, l, LANES), lambda s, bi: (bi, 0, s)),
                  pl.BlockSpec((1, lc, LANES), lambda s, bi: (bi, 0, s)),
                  pl.BlockSpec((1, ts, n1), lambda s, bi: (s, 0, 0)),
                  pl.BlockSpec((1, 2 * hw, ts), lambda s, bi: (s, 0, 0)),
                  pl.BlockSpec((1, 2, 18, hw), lambda s, bi: (s, 0, 0, 0)),
                  pl.BlockSpec((1, LANES), lambda s, bi: (0, s))],
        out_specs=[pl.BlockSpec((1, l, LANES), lambda s, bi: (bi, 0, s)),
                   pl.BlockSpec((1, lc, LANES), lambda s, bi: (bi, 0, s))],
        out_shape=[jax.ShapeDtypeStruct((b, l, width), F32),
                   jax.ShapeDtypeStruct((b, lc, width), F32)],
        scratch_shapes=[pltpu.VMEM((l + lc, LANES), F32),
                        pltpu.VMEM((nb, ts), BF16),
                        pltpu.VMEM((nb, n1), F32),
                        pltpu.VMEM((nb, 2 * hw), F32),
                        pltpu.VMEM((nb, hw), F32)],
        compiler_params=_params("parallel", "parallel"),
        name="s5_scan",
    )(main_l, main_c, w1, w2, apw, d_skip.reshape(1, width))


def _s5_glu_kernel(y_ref, w_ref, b_ref, o_ref):
    g = _gelu(y_ref[0])
    o_ref[0] = (g * jax.nn.sigmoid(_dot(g.astype(BF16), w_ref[...]) + b_ref[...])).astype(BF16)


def _s5_glu_call(y, glu_w, glu_b):
    b, l, w = y.shape
    tm = min(l, 1024)
    return pl.pallas_call(
        _s5_glu_kernel,
        grid=(b, l // tm),
        in_specs=[pl.BlockSpec((1, tm, w), lambda bi, i: (bi, i, 0)),
                  pl.BlockSpec((w, w), lambda bi, i: (0, 0)),
                  pl.BlockSpec((1, w), lambda bi, i: (0, 0))],
        out_specs=pl.BlockSpec((1, tm, w), lambda bi, i: (bi, i, 0)),
        out_shape=jax.ShapeDtypeStruct((b, l, w), BF16),
        compiler_params=_params("parallel", "parallel"),
        name="s5_glu",
    )(y, glu_w, glu_b.reshape(1, w))


def _conv_rows(src_ref, n_rows, xpad, cw, bias, emit, tbs):
    zeros = jnp.zeros((SUBLANES, LANES), F32)
    xpad[0:SUBLANES, :] = zeros
    xpad[SUBLANES + n_rows:2 * SUBLANES + n_rows, :] = zeros
    tbs = min(tbs, n_rows)
    for r0 in range(0, n_rows, tbs):
        xpad[SUBLANES + r0:SUBLANES + r0 + tbs, :] = src_ref[0, r0:r0 + tbs, :].astype(F32)
    for r0 in range(0, n_rows, tbs):
        acc = None
        for j in range(SHORT_CONV):
            off = SUBLANES + r0 + j - SHORT_CONV // 2
            term = cw[j:j + 1, :] * xpad[off:off + tbs, :]
            acc = term if acc is None else acc + term
        if bias is not None:
            acc = acc + bias
        emit(r0, acc)


def _lru_kernel(xl_ref, xc_ref, gl_ref, cw_ref, cb_ref, wa_ref, wx_ref, ba_ref, bx_ref, lam_ref, o_ref,
                xpad, xconv, abuf, bbuf, hsum, *, tb):
    n_lat, n_ctx = xl_ref.shape[1], xc_ref.shape[1]
    cw = cw_ref[...]
    cb = cb_ref[...]

    def put(off):
        def emit(r0, y):
            xconv[off + r0:off + r0 + y.shape[0], :] = y
        return emit

    _conv_rows(xc_ref, n_ctx, xpad, cw, cb, put(0), tb)
    _conv_rows(xl_ref, n_lat, xpad, cw, cb, put(n_ctx), tb)

    row = lax.broadcasted_iota(jnp.int32, (SUBLANES, LANES), 0)

    def tile_body(d, ntiles, t, carry):
        ti = t if d == 0 else ntiles - 1 - t
        r0 = pl.multiple_of(ti * SUBLANES, SUBLANES)
        h = bbuf[pl.ds(r0, SUBLANES), :] + abuf[pl.ds(r0, SUBLANES), :] * carry
        bbuf[pl.ds(r0, SUBLANES), :] = h
        last = SUBLANES - 1 if d == 0 else 0
        return jnp.broadcast_to(h[last:last + 1], (SUBLANES, LANES))

    def local_scans(d, a, b):
        nt = a.shape[0] // SUBLANES
        a3 = a.reshape(nt, SUBLANES, LANES)
        b3 = b.reshape(nt, SUBLANES, LANES)
        row3 = lax.broadcasted_iota(jnp.int32, a3.shape, 1)
        for k in (1, 2, 4):
            sh = k if d == 0 else SUBLANES - k
            keep = (row3 >= k) if d == 0 else (row3 <= SUBLANES - 1 - k)
            sa = jnp.where(keep, pltpu.roll(a3, sh, 1), 1.0)
            sb = jnp.where(keep, pltpu.roll(b3, sh, 1), 0.0)
            b3 = b3 + a3 * sb
            a3 = a3 * sa
        return a3.reshape(a.shape), b3.reshape(b.shape)

    for d in range(2):
        sp = _softplus(-lam_ref[d])
        wa, wx = wa_ref[d, 0], wx_ref[d, 0]
        ba, bx = ba_ref[d], bx_ref[d]

        def run_segment(off, n_rows, is_lat, carry, d=d, sp=sp, wa=wa, wx=wx, ba=ba, bx=bx):
            tbs = min(tb, n_rows)
            nblk = n_rows // tbs

            def blk_body(i, carry):
                bi = i if d == 0 else nblk - 1 - i
                lr0 = pl.multiple_of(bi * tbs, tbs)
                xc = xconv[pl.ds(off + lr0, tbs), :]
                xb = xc.astype(BF16)
                r = _sigmoid(_dot(xb, wa) + ba)
                ig = _sigmoid(_dot(xb, wx) + bx)
                log_a = -LRU_C * r * sp
                a = jnp.exp(log_a)
                em = _expm1_given_exp(2.0 * log_a, a * a)
                acum, hloc = local_scans(d, a, jnp.sqrt(jnp.maximum(-em, 0.0)) * (ig * xc))
                abuf[0:tbs, :] = acum
                bbuf[0:tbs, :] = hloc
                carry = lax.fori_loop(0, tbs // SUBLANES,
                                      functools.partial(tile_body, d, tbs // SUBLANES), carry)
                if is_lat:
                    if d == 0:
                        hsum[pl.ds(lr0, tbs), :] = bbuf[0:tbs, :]
                    else:
                        y = (hsum[pl.ds(lr0, tbs), :] + bbuf[0:tbs, :]) * _gelu(gl_ref[0, pl.ds(lr0, tbs), :].astype(F32))
                        o_ref[0, pl.ds(lr0, tbs), :] = y.astype(BF16)
                return carry

            return lax.fori_loop(0, nblk, blk_body, carry)

        carry = run_segment(0, n_ctx, False, jnp.zeros((SUBLANES, LANES), F32))
        run_segment(n_ctx, n_lat, True, carry)


def _lru_call(main_l, main_c, conv_w, conv_b, wa, ba, wx, bx, lam):
    b, l, _ = main_l.shape
    lc = main_c.shape[1]
    width = conv_b.shape[0]
    nb = width // LRU_BLOCK
    tb = 256
    vec = pl.BlockSpec((2, 1, LANES), lambda bi, n: (0, 0, n))
    wspec = pl.BlockSpec((2, 1, LRU_BLOCK, LRU_BLOCK), lambda bi, n: (0, n, 0, 0))
    return pl.pallas_call(
        functools.partial(_lru_kernel, tb=tb),
        grid=(b, nb),
        in_specs=[pl.BlockSpec((1, l, LANES), lambda bi, n: (bi, 0, n)),
                  pl.BlockSpec((1, lc, LANES), lambda bi, n: (bi, 0, n)),
                  pl.BlockSpec((1, l, LANES), lambda bi, n, nb=nb: (bi, 0, nb + n)),
                  pl.BlockSpec((SHORT_CONV, LANES), lambda bi, n: (0, n)),
                  pl.BlockSpec((1, LANES), lambda bi, n: (0, n)),
                  wspec, wspec, vec, vec, vec],
        out_specs=pl.BlockSpec((1, l, LANES), lambda bi, n: (bi, 0, n)),
        out_shape=jax.ShapeDtypeStruct((b, l, width), BF16),
        scratch_shapes=[pltpu.VMEM((l + 2 * SUBLANES, LANES), F32),
                        pltpu.VMEM((lc + l, LANES), F32),
                        pltpu.VMEM((tb, LANES), F32),
                        pltpu.VMEM((tb, LANES), F32),
                        pltpu.VMEM((l, LANES), F32)],
        compiler_params=_params("parallel", "parallel"),
        name="rglru",
    )(main_l, main_c, main_l, conv_w.T, conv_b.reshape(1, width), wa, wx,
      ba.reshape(2, 1, width), bx.reshape(2, 1, width), lam.reshape(2, 1, width))


def _dn_gate_kernel(ab_ref, alog_ref, dtb_ref, o_ref, *, heads):
    x = ab_ref[0]
    tm = x.shape[0]
    lane = lax.broadcasted_iota(jnp.int32, x.shape, 1)
    g = -jnp.exp(alog_ref[...]) * _softplus(x + dtb_ref[...])
    ii = lax.broadcasted_iota(jnp.int32, (tm, tm), 0)
    jj = lax.broadcasted_iota(jnp.int32, (tm, tm), 1)
    same = (ii // DN_CHUNK) == (jj // DN_CHUNK)
    lower = jnp.where(same & (jj <= ii), 1.0, 0.0).astype(F32)
    upper = jnp.where(same & (jj >= ii), 1.0, 0.0).astype(F32)
    pre = jnp.dot(lower, g, preferred_element_type=F32, precision=lax.Precision.HIGHEST)
    suf = jnp.dot(upper, g, preferred_element_type=F32, precision=lax.Precision.HIGHEST)
    gc = jnp.where(lane < heads, pre, suf)
    o_ref[0] = jnp.where(lane < 2 * heads, gc, jax.nn.sigmoid(x))


def _dn_gate_call(ab, a_log, dt_bias):
    b, l, _ = ab.shape
    tm = min(l, 256)
    pad = lambda t: jnp.pad(t.reshape(1, -1), ((0, 0), (0, LANES - t.size)))
    return pl.pallas_call(
        functools.partial(_dn_gate_kernel, heads=a_log.shape[1]),
        grid=(b, l // tm),
        in_specs=[pl.BlockSpec((1, tm, LANES), lambda bi, i: (bi, i, 0)),
                  pl.BlockSpec((1, LANES), lambda bi, i: (0, 0)),
                  pl.BlockSpec((1, LANES), lambda bi, i: (0, 0))],
        out_specs=pl.BlockSpec((1, tm, LANES), lambda bi, i: (bi, i, 0)),
        out_shape=jax.ShapeDtypeStruct((b, l, LANES), F32),
        compiler_params=_params("parallel", "parallel"),
        name="deltanet_gates",
    )(ab, pad(a_log), pad(dt_bias))


def _dn_gate_layouts(gates, heads):
    b, l, _ = gates.shape
    t = gates[:, :, :4 * heads].reshape(b, l, 2, 2, heads)
    col = jnp.transpose(t, (0, 4, 1, 2, 3)).reshape(b, heads, l, 4)
    row = jnp.transpose(col.reshape(b, heads, l // DN_CHUNK, DN_CHUNK, 4), (0, 1, 2, 4, 3))
    return col, row


def _dn_kernel(ql_ref, kl_ref, vl_ref, gl_ref, qc_ref, kc_ref, vc_ref, gc_ref,
               cwq_ref, cwk_ref, cwv_ref, coll_ref, rowl_ref, colc_ref, rowc_ref, ng_ref,
               ol_ref, oc_ref,
               xpad, qs, ks, vs, oacc, pbuf, xbuf, atb, abuf, bbuf, qpbuf, egl, *, group):
    n_lat, n_ctx = ql_ref.shape[1], qc_ref.shape[1]
    c = DN_CHUNK
    tbs = 256

    def prep(src_ref, n_rows, off, cw_ref, dst, mode):
        def emit(r0, y):
            y = _silu(y)
            if mode != "v":
                y = y * lax.rsqrt(jnp.sum(y * y, axis=-1, keepdims=True) + EPS)
            if mode == "q":
                y = y * (DN_DK ** -0.5)
            dst[off + r0:off + r0 + y.shape[0], :] = y
        _conv_rows(src_ref, n_rows, xpad, cw_ref[...], None, emit, tbs)

    for src_c, src_l, cw_ref, dst, mode in ((qc_ref, ql_ref, cwq_ref, qs, "q"),
                                            (kc_ref, kl_ref, cwk_ref, ks, "k"),
                                            (vc_ref, vl_ref, cwv_ref, vs, "v")):
        prep(src_c, n_ctx, 0, cw_ref, dst, mode)
        prep(src_l, n_lat, n_ctx, cw_ref, dst, mode)

    ii = lax.broadcasted_iota(jnp.int32, (c, c), 0)
    jj = lax.broadcasted_iota(jnp.int32, (c, c), 1)
    n_apply = int(math.log2(c))
    segments = ((0, n_ctx, colc_ref, rowc_ref), (n_ctx, n_lat, coll_ref, rowl_ref))

    def phase1_group(it, grp, off, col_ref, row_ref):
        def chunk_ids(g):
            ci = it * grp + g
            return ci, pl.multiple_of(ci * c, c), pl.multiple_of(off + ci * c, c), off // c + ci

        def gates(col_ref, lr0, d):
            gcb = col_ref[0, 0, pl.ds(lr0, c), :]
            return gcb[:, d:d + 1], gcb[:, 2 + d:3 + d]

        for g in range(grp):
            ci, lr0, r0, _ = chunk_ids(g)
            q = qs[pl.ds(r0, c), :]
            k = ks[pl.ds(r0, c), :]
            v = vs[pl.ds(r0, c), :]
            kbf = k.astype(BF16)
            kk = _dot_nt(kbf, kbf)
            qk = _dot_nt(q.astype(BF16), kbf)
            rows4 = row_ref[0, 0, ci]
            for d in range(2):
                causal = (ii >= jj) if d == 0 else (ii <= jj)
                strict = (ii > jj) if d == 0 else (ii < jj)
                gc_col, beta = gates(col_ref, lr0, d)
                gc_row = rows4[d:d + 1, :]
                gam = jnp.where(causal, jnp.exp(jnp.where(causal, gc_col - gc_row, 0.0)), 0.0)
                pbuf[2 * g + d] = jnp.where(strict, -(beta * kk) * gam, 0.0).astype(BF16)
                xbuf[2 * g + d] = jnp.concatenate([v * beta, (k * beta) * jnp.exp(gc_col)], axis=1)
                atb[2 * g + d] = (qk * gam).astype(BF16)

        for j in range(n_apply):
            for ch in range(2 * grp):
                pb = pbuf[ch]
                xv = xbuf[ch]
                xbuf[ch] = xv + _dot(pb, xv.astype(BF16))
                if j < n_apply - 1:
                    pbuf[ch] = _dot(pb, pb).astype(BF16)

        for g in range(grp):
            ci, lr0, r0, cg = chunk_ids(g)
            q = qs[pl.ds(r0, c), :]
            k = ks[pl.ds(r0, c), :]
            oloc = None
            for d in range(2):
                gc_col, _ = gates(col_ref, lr0, d)
                g_last = gc_col[c - 1:c, :] if d == 0 else gc_col[0:1, :]
                kdt = (k * jnp.exp(g_last - gc_col)).T.astype(BF16)
                xv = xbuf[2 * g + d]
                ub = xv[:, 0:LANES].astype(BF16)
                wb = xv[:, LANES:2 * LANES].astype(BF16)
                at = atb[2 * g + d]
                abuf[d, cg] = _dot(kdt, wb).astype(BF16)
                bbuf[d, cg] = _dot(kdt, ub).astype(BF16)
                qpbuf[d, pl.ds(r0, c), :] = (q * jnp.exp(gc_col) - _dot(at, wb)).astype(BF16)
                part = _dot(at, ub)
                oloc = part if oloc is None else oloc + part
                egl[d, pl.ds(cg, 1), :] = jnp.broadcast_to(jnp.exp(g_last), (1, LANES))
            oacc[pl.ds(r0, c), :] = oloc

    for off, n_rows, col_ref, row_ref in segments:
        nch = n_rows // c
        grp = min(group, nch)

        def p1_body(it, carry, off=off, col_ref=col_ref, row_ref=row_ref, grp=grp):
            phase1_group(it, grp, off, col_ref, row_ref)
            return carry
        lax.fori_loop(0, nch // grp, p1_body, 0)

    def phase2(d, ci, state, off):
        r0 = pl.multiple_of(off + ci * c, c)
        cg = off // c + ci
        sb = state.astype(BF16)
        oacc[pl.ds(r0, c), :] += _dot(qpbuf[d, pl.ds(r0, c), :], sb)
        return state * egl[d, pl.ds(cg, 1), :] - _dot(abuf[d, cg], sb) + bbuf[d, cg].astype(F32)

    states = (jnp.zeros((DN_DK, LANES), F32), jnp.zeros((DN_DK, LANES), F32))
    for off, n_rows, _, _ in segments:
        nch = n_rows // c

        def p2_body(i, st, off=off, nch=nch):
            return (phase2(0, i, st[0], off), phase2(1, nch - 1 - i, st[1], off))
        states = lax.fori_loop(0, nch, p2_body, states)

    ng = ng_ref[...]
    for off, n_rows, g_ref, o_ref in ((0, n_ctx, gc_ref, oc_ref), (n_ctx, n_lat, gl_ref, ol_ref)):
        t = min(tbs, n_rows)
        for r0 in range(0, n_rows, t):
            o = oacc[off + r0:off + r0 + t, :]
            y = o * lax.rsqrt(jnp.mean(o * o, axis=-1, keepdims=True) + EPS) * ng
            o_ref[0, r0:r0 + t, :] = (y * _silu(g_ref[0, r0:r0 + t, :].astype(F32))).astype(BF16)


def _dn_call(main_l, main_c, col_l, row_l, col_c, row_c, conv_w, norm_g, col0):
    b, l, _ = main_l.shape
    lc = main_c.shape[1]
    heads = col_l.shape[1]
    lt = l + lc
    nch = lt // DN_CHUNK
    cwt = conv_w.T
    group = 8

    def blk(n_rows, which):
        return pl.BlockSpec((1, n_rows, LANES), lambda bi, h, which=which: (bi, 0, col0 + which * heads + h))

    def cw(which):
        return pl.BlockSpec((SHORT_CONV, LANES), lambda bi, h, which=which: (0, which * heads + h))

    in_specs = ([blk(l, w) for w in range(4)] + [blk(lc, w) for w in range(4)] + [cw(0), cw(1), cw(2)] +
                [pl.BlockSpec((1, 1, l, 4), lambda bi, h: (bi, h, 0, 0)),
                 pl.BlockSpec((1, 1, l // DN_CHUNK, 4, DN_CHUNK), lambda bi, h: (bi, h, 0, 0, 0)),
                 pl.BlockSpec((1, 1, lc, 4), lambda bi, h: (bi, h, 0, 0)),
                 pl.BlockSpec((1, 1, lc // DN_CHUNK, 4, DN_CHUNK), lambda bi, h: (bi, h, 0, 0, 0)),
                 pl.BlockSpec((1, LANES), lambda bi, h: (0, 0))])
    return pl.pallas_call(
        functools.partial(_dn_kernel, group=group),
        grid=(b, heads),
        in_specs=in_specs,
        out_specs=[pl.BlockSpec((1, l, LANES), lambda bi, h: (bi, 0, h)),
                   pl.BlockSpec((1, lc, LANES), lambda bi, h: (bi, 0, h))],
        out_shape=[jax.ShapeDtypeStruct((b, l, heads * LANES), BF16),
                   jax.ShapeDtypeStruct((b, lc, heads * LANES), BF16)],
        scratch_shapes=[pltpu.VMEM((l + 2 * SUBLANES, LANES), F32),
                        pltpu.VMEM((lt, LANES), F32),
                        pltpu.VMEM((lt, LANES), F32),
                        pltpu.VMEM((lt, LANES), F32),
                        pltpu.VMEM((lt, LANES), F32),
                        pltpu.VMEM((2 * group, DN_CHUNK, DN_CHUNK), BF16),
                        pltpu.VMEM((2 * group, DN_CHUNK, 2 * LANES), F32),
                        pltpu.VMEM((2 * group, DN_CHUNK, DN_CHUNK), BF16),
                        pltpu.VMEM((2, nch, DN_DK, LANES), BF16),
                        pltpu.VMEM((2, nch, DN_DK, LANES), BF16),
                        pltpu.VMEM((2, lt, LANES), BF16),
                        pltpu.VMEM((2, nch, LANES), F32)],
        compiler_params=_params("parallel", "parallel"),
        name="gated_deltanet",
    )(main_l, main_l, main_l, main_l, main_c, main_c, main_c, main_c, cwt, cwt, cwt,
      col_l, row_l, col_c, row_c, norm_g.reshape(1, LANES))


def _ret_kernel(th_ref, ql_ref, kl_ref, vl_ref, qc_ref, kc_ref, vc_ref, cos_ref, sin_ref, o_ref,
                qs, kts, s_ref, dec_ref, xz_ref):
    n_lat, n_ctx = ql_ref.shape[1], qc_ref.shape[1]
    c = RET_CHUNK
    half = RET_DK // 2
    ncc, nlc = n_ctx // c, n_lat // c
    scale = RET_DK ** -0.5

    icol = lax.broadcasted_iota(jnp.int32, (c, RET_DV), 0)
    ii = lax.broadcasted_iota(jnp.int32, (c, c), 0)
    jj = lax.broadcasted_iota(jnp.int32, (c, c), 1)
    gch = []
    for d in range(2):
        lg = -jnp.exp(th_ref[d, 0])
        lg1 = lg[:, 0:1]
        fidx = (icol if d == 0 else c - 1 - icol).astype(F32)
        rel = (ii - jj) if d == 0 else (jj - ii)
        mask = rel >= 0
        dec_ref[d] = jnp.where(mask, jnp.exp(jnp.where(mask, rel, 0).astype(F32) * lg), 0.0)
        xz_ref[d, 0] = jnp.exp((fidx + 1.0) * lg1)
        xz_ref[d, 1] = jnp.exp((c - 1.0 - fidx) * lg1)
        gch.append(jnp.exp(c * lg1))
    s_ref[...] = jnp.zeros_like(s_ref)

    def prep_ctx(ci, carry):
        r0 = pl.multiple_of(ci * c, c)
        qs[pl.ds(r0, c), :] = qc_ref[0, pl.ds(r0, c), :].astype(BF16)
        k = kc_ref[0, pl.ds(r0, c), :].astype(F32) * scale
        kts[ci] = k.T.astype(BF16)
        return carry

    def prep_lat(ci, carry):
        r0 = pl.multiple_of(ci * c, c)
        cos = cos_ref[pl.ds(r0, c), :]
        sin = sin_ref[pl.ds(r0, c), :]

        def rope(t):
            t1, t2 = t[:, 0:half], t[:, half:2 * half]
            return jnp.concatenate([t1 * cos - t2 * sin, t1 * sin + t2 * cos], axis=-1)

        q = rope(ql_ref[0, pl.ds(r0, c), :].astype(F32))
        k = rope(kl_ref[0, pl.ds(r0, c), :].astype(F32)) * scale
        qs[pl.ds(pl.multiple_of(n_ctx + ci * c, c), c), :] = q.astype(BF16)
        kts[ncc + ci] = k.T.astype(BF16)
        return carry

    lax.fori_loop(0, ncc, prep_ctx, 0)
    lax.fori_loop(0, nlc, prep_lat, 0)

    def one(d, ci, row_off, ch_off, v_ref, write, first):
        r0 = pl.multiple_of(ci * c, c)
        qb = qs[pl.ds(pl.multiple_of(row_off + ci * c, c), c), :]
        kt = kts[ch_off + ci]
        v = v_ref[0, pl.ds(r0, c), :].astype(BF16)
        state = s_ref[d]
        if write:
            inner = _dot(qb, kt) * dec_ref[d]
            o = _dot(inner.astype(BF16), v) + xz_ref[d, 0] * _dot(qb, state.astype(BF16))
            if first:
                o_ref[0, pl.ds(r0, c), :] = o
            else:
                o_ref[0, pl.ds(r0, c), :] += o
        vz = (v.astype(F32) * xz_ref[d, 1]).astype(BF16)
        s_ref[d] = state * gch[d] + _dot(kt, vz)

    def ctx_body(i, carry):
        one(0, i, 0, 0, vc_ref, False, False)
        one(1, ncc - 1 - i, 0, 0, vc_ref, False, False)
        return carry

    lax.fori_loop(0, ncc, ctx_body, 0)
    for lo, hi, first in ((0, nlc // 2, True), (nlc // 2, nlc, False)):
        def lat_body(i, carry, first=first):
            one(0, i, n_ctx, ncc, vl_ref, True, first)
            one(1, nlc - 1 - i, n_ctx, ncc, vl_ref, True, first)
            return carry
        lax.fori_loop(lo, hi, lat_body, 0)


def _ret_call(main_l, main_c, theta, cos, sin, q_col0):
    b, l, _ = main_l.shape
    lc = main_c.shape[1]
    heads = theta.shape[1]
    c = RET_CHUNK
    qb0 = q_col0 // RET_DK
    kb0 = qb0 + heads
    vb0 = (q_col0 + 2 * heads * RET_DK) // RET_DV
    th = jnp.broadcast_to(theta.reshape(2, heads, 1, 1), (2, heads, 1, LANES))
    return pl.pallas_call(
        _ret_kernel,
        grid=(b, heads),
        in_specs=[pl.BlockSpec((2, 1, 1, LANES), lambda bi, h: (0, h, 0, 0)),
                  pl.BlockSpec((1, l, RET_DK), lambda bi, h: (bi, 0, qb0 + h)),
                  pl.BlockSpec((1, l, RET_DK), lambda bi, h: (bi, 0, kb0 + h)),
                  pl.BlockSpec((1, l, RET_DV), lambda bi, h: (bi, 0, vb0 + h)),
                  pl.BlockSpec((1, lc, RET_DK), lambda bi, h: (bi, 0, qb0 + h)),
                  pl.BlockSpec((1, lc, RET_DK), lambda bi, h: (bi, 0, kb0 + h)),
                  pl.BlockSpec((1, lc, RET_DV), lambda bi, h: (bi, 0, vb0 + h)),
                  pl.BlockSpec((l, RET_DK // 2), lambda bi, h: (0, 0)),
                  pl.BlockSpec((l, RET_DK // 2), lambda bi, h: (0, 0))],
        out_specs=pl.BlockSpec((1, l, RET_DV), lambda bi, h: (bi, 0, h)),
        out_shape=jax.ShapeDtypeStruct((b, l, heads * RET_DV), F32),
        scratch_shapes=[pltpu.VMEM((lc + l, RET_DK), BF16),
                        pltpu.VMEM(((lc + l) // c, RET_DK, c), BF16),
                        pltpu.VMEM((2, RET_DK, RET_DV), F32),
                        pltpu.VMEM((2, c, c), F32),
                        pltpu.VMEM((2, 2, c, RET_DV), F32)],
        compiler_params=_params("parallel", "parallel"),
        name="retention",
    )(th, main_l, main_l, main_l, main_c, main_c, main_c, cos, sin)


def _ret_finish_kernel(o_ref, r_ref, y_ref):
    o = o_ref[0]
    y = o * lax.rsqrt(jnp.mean(o * o, axis=-1, keepdims=True) + EPS)
    y_ref[0] = (y * _silu(r_ref[0].astype(F32))).astype(BF16)


def _ret_finish_call(o, main_l, r_col0):
    b, l, w = o.shape
    heads = w // RET_DV
    tm = min(l, 1024)
    rb0 = r_col0 // RET_DV
    return pl.pallas_call(
        _ret_finish_kernel,
        grid=(b, l // tm, heads),
        in_specs=[pl.BlockSpec((1, tm, RET_DV), lambda bi, i, h: (bi, i, h)),
                  pl.BlockSpec((1, tm, RET_DV), lambda bi, i, h: (bi, i, rb0 + h))],
        out_specs=pl.BlockSpec((1, tm, RET_DV), lambda bi, i, h: (bi, i, h)),
        out_shape=jax.ShapeDtypeStruct((b, l, w), BF16),
        compiler_params=_params("parallel", "parallel", "parallel"),
        name="retention_finish",
    )(o, main_l)


def _rope_tables(n_tokens):
    rows = n_tokens // GRID_W
    r, col = jnp.meshgrid(jnp.arange(rows), jnp.arange(GRID_W), indexing='ij')
    n_freq = RET_DK // 4
    inv = ROPE_BASE ** (-jnp.arange(n_freq, dtype=F32) / n_freq)
    ang = jnp.concatenate([r.reshape(-1, 1) * inv, col.reshape(-1, 1) * inv], axis=-1)
    return jnp.cos(ang), jnp.sin(ang)


def _even_layer(xl, xc, mods_l, mods_c, norm1_g, norm2_g, w_in, w_out, s5p, glu_w, glu_b,
                dn_conv_w, dn_a_log, dn_dt_bias, dn_norm_g, ffn, final_g, final_norm):
    s5_width = glu_w.shape[0]
    dn_width = dn_conv_w.shape[0] // 3
    heads = dn_width // LANES
    n_main = s5_width + 4 * dn_width
    w_main = w_in[:, :n_main].astype(BF16)
    w_small = jnp.pad(w_in[:, n_main:], ((0, 0), (0, LANES - (w_in.shape[1] - n_main)))).astype(BF16)

    main_l, ab_l = _inproj_call(xl, norm1_g, mods_l[0], mods_l[1], w_main, w_small)
    main_c, ab_c = _inproj_call(xc, norm1_g, mods_c[0], mods_c[1], w_main, w_small)

    y_l, y_c = _s5_scan_call(main_l, main_c, *s5p)
    glu_wb = glu_w.astype(BF16)
    s5_l = _s5_glu_call(y_l, glu_wb, glu_b)
    s5_c = _s5_glu_call(y_c, glu_wb, glu_b)

    col_l, row_l = _dn_gate_layouts(_dn_gate_call(ab_l, dn_a_log, dn_dt_bias), heads)
    col_c, row_c = _dn_gate_layouts(_dn_gate_call(ab_c, dn_a_log, dn_dt_bias), heads)
    dn_l, dn_c = _dn_call(main_l, main_c, col_l, row_l, col_c, row_c, dn_conv_w, dn_norm_g,
                          s5_width // LANES)

    w_o1 = w_out[:s5_width].astype(BF16)
    w_o2 = w_out[s5_width:].astype(BF16)
    xl = _outproj_call(xl, mods_l[2], s5_l, dn_l, w_o1, w_o2)
    xc = _outproj_call(xc, mods_c[2], s5_c, dn_c, w_o1, w_o2)
    w1, w3, w2 = ffn
    xl = _ffn_call(xl, norm2_g, mods_l[3], mods_l[4], mods_l[5], w1, w3, w2, final_g, final_norm)
    xc = _ffn_call(xc, norm2_g, mods_c[3], mods_c[4], mods_c[5], w1, w3, w2, final_g, False)
    return xl, xc


def _odd_layer(xl, xc, mods_l, mods_c, norm1_g, norm2_g, w_in, w_out, conv_w, conv_b, wa, ba, wx, bx, lam,
               theta, cos, sin, ffn, final_g, final_norm):
    lru_width = conv_b.shape[0]
    heads = theta.shape[1]
    w_inb = w_in.astype(BF16)
    main_l = _inproj_call(xl, norm1_g, mods_l[0], mods_l[1], w_inb)
    main_c = _inproj_call(xc, norm1_g, mods_c[0], mods_c[1], w_inb)

    lru_l = _lru_call(main_l, main_c, conv_w, conv_b, wa.astype(BF16), ba, wx.astype(BF16), bx, lam)
    q_col0 = 2 * lru_width
    o = _ret_call(main_l, main_c, theta, cos, sin, q_col0)
    ret_l = _ret_finish_call(o, main_l, q_col0 + 2 * heads * RET_DK + heads * RET_DV)

    w_o1 = w_out[:lru_width].astype(BF16)
    w_o2 = w_out[lru_width:].astype(BF16)
    xl = _outproj_call(xl, mods_l[2], lru_l, ret_l, w_o1, w_o2)
    w1, w3, w2 = ffn
    return _ffn_call(xl, norm2_g, mods_l[3], mods_l[4], mods_l[5], w1, w3, w2, final_g, final_norm)


def kernel(x, c, ctx, c_ctx, mod_w, mod_b, norm1_g, norm2_g, ffn_w1, ffn_w3, ffn_w2, final_g, even_w_in, even_w_out, s5_lam_re, s5_lam_im, s5_log_step, s5_b_re, s5_b_im, s5_c_re, s5_c_im, s5_d, s5_glu_w, s5_glu_b, dn_conv_w, dn_a_log, dn_dt_bias, dn_norm_g, odd_w_in, odd_w_out, lru_conv_w, lru_conv_b, lru_wa, lru_ba, lru_wx, lru_bx, lru_lam, ret_theta):
    bsz, n_tok, d = x.shape
    depth = mod_w.shape[0]
    assert depth == 2 and bsz + 1 <= SUBLANES
    cos, sin = _rope_tables(n_tok)

    rows = jnp.concatenate([c, c_ctx[None, :], jnp.zeros((SUBLANES - bsz - 1, d), F32)], axis=0)
    mods = _mod_call(rows, mod_w, mod_b)

    def split_mods(i):
        m = mods[i].reshape(SUBLANES, 6, d)
        ml = [m[:bsz, k][:, None, :] for k in range(6)]
        mc = [jnp.broadcast_to(m[bsz, k][None, None, :], (bsz, 1, d)) for k in range(6)]
        return ml, mc

    xl, xc = x, ctx
    ml, mc = split_mods(0)
    s5p = _s5_weights(s5_lam_re[0], s5_lam_im[0], s5_log_step[0], s5_b_re[0], s5_b_im[0],
                      s5_c_re[0], s5_c_im[0]) + (s5_d[0],)
    ffn0 = (ffn_w1[0].astype(BF16), ffn_w3[0].astype(BF16), ffn_w2[0].astype(BF16))
    xl, xc = _even_layer(xl, xc, ml, mc, norm1_g[0], norm2_g[0], even_w_in[0], even_w_out[0],
                         s5p, s5_glu_w[0], s5_glu_b[0],
                         dn_conv_w[0], dn_a_log[0], dn_dt_bias[0], dn_norm_g[0], ffn0, final_g, False)
    ml, mc = split_mods(1)
    ffn1 = (ffn_w1[1].astype(BF16), ffn_w3[1].astype(BF16), ffn_w2[1].astype(BF16))
    return _odd_layer(xl, xc, ml, mc, norm1_g[1], norm2_g[1], odd_w_in[0], odd_w_out[0],
                      lru_conv_w[0], lru_conv_b[0], lru_wa[0], lru_ba[0], lru_wx[0], lru_bx[0], lru_lam[0],
                      ret_theta[0], cos, sin, ffn1, final_g, True)
```

```python
import functools
import math

import jax
import jax.numpy as jnp
from jax import lax
from jax.experimental import pallas as pl
from jax.experimental.pallas import tpu as pltpu

F32 = jnp.float32
BF16 = jnp.bfloat16

EPS = 1e-6
GRID_W = 64
ROPE_BASE = 10000.0

S5_GROUP = 16
S5_STATE = 64
S5_BLOCK = 8

DN_DK = 128
DN_CHUNK = 64
SHORT_CONV = 4

LRU_BLOCK = 128
LRU_C = 8.0

RET_DK = 256
RET_DV = 512
RET_CHUNK = 128

LANES = 128
SUBLANES = 8
VMEM_LIMIT = 56 * 1024 * 1024


def _params(*sem):
    return pltpu.CompilerParams(dimension_semantics=sem, vmem_limit_bytes=VMEM_LIMIT)


def _silu(x):
    return x * jax.nn.sigmoid(x)


def _gelu(x):
    return 0.5 * x * (1.0 + jnp.tanh(0.7978845608028654 * (x + 0.044715 * (x * x * x))))


def _softplus(x):
    return jnp.maximum(x, 0.0) + jnp.log1p(jnp.exp(-jnp.abs(x)))


def _sigmoid(x):
    return 0.5 * (1.0 + jnp.tanh(0.5 * x))


def _expm1_2x(x):
    t = jnp.tanh(x)
    return 2.0 * t / (1.0 - t)


def _adaln(x, g, shift, scale):
    y = x * lax.rsqrt(jnp.mean(x * x, axis=-1, keepdims=True) + EPS)
    return (y * g) * (1.0 + scale) + shift


def _dot(a, b):
    return jnp.dot(a, b, preferred_element_type=F32)


def _dot_nt(a, b):
    return lax.dot_general(a, b, (((1,), (1,)), ((), ())), preferred_element_type=F32)


def _mod_kernel(s_ref, w_ref, b_ref, o_ref):
    s = _silu(s_ref[...])
    o_ref[0] = _dot(s.astype(BF16), w_ref[0].astype(BF16)) + b_ref[0]


def _mod_call(rows, mod_w, mod_b):
    depth, d, n = mod_w.shape
    tn = 1024
    return pl.pallas_call(
        _mod_kernel,
        grid=(depth, n // tn),
        in_specs=[pl.BlockSpec((SUBLANES, d), lambda i, j: (0, 0)),
                  pl.BlockSpec((1, d, tn), lambda i, j: (i, 0, j)),
                  pl.BlockSpec((1, 1, tn), lambda i, j: (i, 0, j))],
        out_specs=pl.BlockSpec((1, SUBLANES, tn), lambda i, j: (i, 0, j)),
        out_shape=jax.ShapeDtypeStruct((depth, SUBLANES, n), F32),
        compiler_params=_params("parallel", "parallel"),
        name="mod_proj",
    )(rows, mod_w, mod_b.reshape(depth, 1, n))


def _inproj_kernel(x_ref, g_ref, sh_ref, sc_ref, w_ref, *rest, has_small):
    if has_small:
        ws_ref, o_ref, os_ref, h_ref = rest
    else:
        o_ref, h_ref = rest

    @pl.when(pl.program_id(2) == 0)
    def _():
        h = _adaln(x_ref[0], g_ref[...], sh_ref[0], sc_ref[0]).astype(BF16)
        h_ref[...] = h
        if has_small:
            os_ref[0] = _dot(h, ws_ref[...])

    o_ref[0] = _dot(h_ref[...], w_ref[...]).astype(o_ref.dtype)


def _inproj_call(x, g, shift, scale, w, w_small=None):
    b, l, d = x.shape
    n = w.shape[1]
    tm = min(l, 1024)
    tn = 512
    has_small = w_small is not None
    in_specs = [pl.BlockSpec((1, tm, d), lambda bi, i, j: (bi, i, 0)),
                pl.BlockSpec((1, d), lambda bi, i, j: (0, 0)),
                pl.BlockSpec((1, 1, d), lambda bi, i, j: (bi, 0, 0)),
                pl.BlockSpec((1, 1, d), lambda bi, i, j: (bi, 0, 0)),
                pl.BlockSpec((d, tn), lambda bi, i, j: (0, j))]
    out_specs = [pl.BlockSpec((1, tm, tn), lambda bi, i, j: (bi, i, j))]
    out_shape = [jax.ShapeDtypeStruct((b, l, n), BF16)]
    args = [x, g.reshape(1, d), shift, scale, w]
    if has_small:
        in_specs.append(pl.BlockSpec((d, LANES), lambda bi, i, j: (0, 0)))
        out_specs.append(pl.BlockSpec((1, tm, LANES), lambda bi, i, j: (bi, i, 0)))
        out_shape.append(jax.ShapeDtypeStruct((b, l, LANES), F32))
        args.append(w_small)
    outs = pl.pallas_call(
        functools.partial(_inproj_kernel, has_small=has_small),
        grid=(b, l // tm, n // tn),
        in_specs=in_specs, out_specs=out_specs, out_shape=out_shape,
        scratch_shapes=[pltpu.VMEM((tm, d), BF16)],
        compiler_params=_params("parallel", "parallel", "arbitrary"),
        name="adaln_inproj",
    )(*args)
    return outs if has_small else outs[0]


def _outproj_kernel(x_ref, gate_ref, a1_ref, a2_ref, w1_ref, w2_ref, o_ref):
    y = _dot(a1_ref[0], w1_ref[...]) + _dot(a2_ref[0], w2_ref[...])
    o_ref[0] = x_ref[0] + gate_ref[0] * y


def _outproj_call(x, gate, a1, a2, w1, w2):
    b, l, d = x.shape
    k1, k2 = a1.shape[2], a2.shape[2]
    tm = min(l, 1024)
    tn = 512
    return pl.pallas_call(
        _outproj_kernel,
        grid=(b, l // tm, d // tn),
        in_specs=[pl.BlockSpec((1, tm, tn), lambda bi, i, j: (bi, i, j)),
                  pl.BlockSpec((1, 1, tn), lambda bi, i, j: (bi, 0, j)),
                  pl.BlockSpec((1, tm, k1), lambda bi, i, j: (bi, i, 0)),
                  pl.BlockSpec((1, tm, k2), lambda bi, i, j: (bi, i, 0)),
                  pl.BlockSpec((k1, tn), lambda bi, i, j: (0, j)),
                  pl.BlockSpec((k2, tn), lambda bi, i, j: (0, j))],
        out_specs=pl.BlockSpec((1, tm, tn), lambda bi, i, j: (bi, i, j)),
        out_shape=jax.ShapeDtypeStruct((b, l, d), F32),
        compiler_params=_params("parallel", "parallel", "arbitrary"),
        name="outproj_residual",
    )(x, gate, a1, a2, w1, w2)


def _ffn_kernel(x_ref, g_ref, sh_ref, sc_ref, gate_ref, w1_ref, w3_ref, w2_ref, fg_ref, o_ref,
                h_ref, acc_ref, *, final_norm):
    f = pl.program_id(2)

    @pl.when(f == 0)
    def _():
        h_ref[...] = _adaln(x_ref[0], g_ref[...], sh_ref[0], sc_ref[0]).astype(BF16)
        acc_ref[...] = jnp.zeros_like(acc_ref)

    h = h_ref[...]
    a = _dot(h, w1_ref[...])
    b = _dot(h, w3_ref[...])
    acc_ref[...] += _dot((_silu(a) * b).astype(BF16), w2_ref[...])

    @pl.when(f == pl.num_programs(2) - 1)
    def _():
        y = x_ref[0] + gate_ref[0] * acc_ref[...]
        if final_norm:
            y = y * lax.rsqrt(jnp.mean(y * y, axis=-1, keepdims=True) + EPS) * fg_ref[...]
        o_ref[0] = y


def _ffn_call(x, g, shift, scale, gate, w1, w3, w2, final_g, final_norm):
    b, l, d = x.shape
    dff = w1.shape[1]
    tm = min(l, 512)
    tf = 512
    vec = pl.BlockSpec((1, 1, d), lambda bi, i, f: (bi, 0, 0))
    return pl.pallas_call(
        functools.partial(_ffn_kernel, final_norm=final_norm),
        grid=(b, l // tm, dff // tf),
        in_specs=[pl.BlockSpec((1, tm, d), lambda bi, i, f: (bi, i, 0)),
                  pl.BlockSpec((1, d), lambda bi, i, f: (0, 0)),
                  vec, vec, vec,
                  pl.BlockSpec((d, tf), lambda bi, i, f: (0, f)),
                  pl.BlockSpec((d, tf), lambda bi, i, f: (0, f)),
                  pl.BlockSpec((tf, d), lambda bi, i, f: (f, 0)),
                  pl.BlockSpec((1, d), lambda bi, i, f: (0, 0))],
        out_specs=pl.BlockSpec((1, tm, d), lambda bi, i, f: (bi, i, 0)),
        out_shape=jax.ShapeDtypeStruct((b, l, d), F32),
        scratch_shapes=[pltpu.VMEM((tm, d), BF16), pltpu.VMEM((tm, d), F32)],
        compiler_params=_params("parallel", "parallel", "arbitrary"),
        name="ffn_swiglu",
    )(x, g.reshape(1, d), shift, scale, gate, w1, w3, w2, final_g.reshape(1, d))


def _s5_disc_kernel(lre_ref, lim_ref, ls_ref, bre_ref, bim_ref, are_ref, aim_ref, bbre_ref, bbim_ref):
    lre = jnp.minimum(lre_ref[...], -1e-4)
    lim = lim_ref[...]
    dt = jnp.exp(ls_ref[...])
    mag = jnp.exp(lre * dt)
    ar = mag * jnp.cos(lim * dt)
    ai = mag * jnp.sin(lim * dt)
    nr, ni = ar - 1.0, ai
    den = lre * lre + lim * lim
    cr = (nr * lre + ni * lim) / den
    ci = (ni * lre - nr * lim) / den
    bre, bim = bre_ref[...], bim_ref[...]
    are_ref[...] = ar
    aim_ref[...] = ai
    bbre_ref[...] = cr * bre - ci * bim
    bbim_ref[...] = cr * bim + ci * bre


def _s5_taps_kernel(ar_ref, ai_ref, br_ref, bi_ref, cr_ref, ci_ref,
                    kt_ref, dr_ref, di_ref, gr_ref, gi_ref, apw_ref, ccr, cci, *, reverse_from):
    t_blk = S5_BLOCK
    ar, ai = ar_ref[0], ai_ref[0]
    br, bi = br_ref[0], bi_ref[0]
    cr, ci = cr_ref[0], ci_ref[0]
    rows = ar.shape[0]
    backward = pl.program_id(0) >= reverse_from
    pw = [(jnp.ones_like(ar), jnp.zeros_like(ar))]
    for _ in range(t_blk):
        pr, pi = pw[-1]
        pw.append((pr * ar - pi * ai, pr * ai + pi * ar))
    for t in range(t_blk):
        sl = slice(t * rows, (t + 1) * rows)
        pr, pi = pw[t]
        ccr[sl, :] = cr * pr - ci * pi
        cci[sl, :] = cr * pi + ci * pr
        dfr, dfi = pw[t_blk - 1 - t]
        dpr = jnp.where(backward, pr, dfr)
        dpi = jnp.where(backward, pi, dfi)
        dr_ref[0, sl, :] = br * dpr - bi * dpi
        di_ref[0, sl, :] = br * dpi + bi * dpr
        gfr, gfi = pw[t + 1]
        gbr, gbi = pw[t_blk - t]
        gpr = jnp.where(backward, gbr, gfr)
        gpi = jnp.where(backward, gbi, gfi)
        gr_ref[0, sl, :] = cr * gpr - ci * gpi
        gi_ref[0, sl, :] = -(cr * gpi + ci * gpr)
    hi = lax.Precision.HIGHEST
    nt = (((1,), (1,)), ((), ()))
    kt_ref[0] = (lax.dot_general(br, ccr[...], nt, precision=hi, preferred_element_type=F32)
                 - lax.dot_general(bi, cci[...], nt, precision=hi, preferred_element_type=F32))
    qr, qi = pw[t_blk]
    er, ei = jnp.ones_like(qr), jnp.zeros_like(qr)
    for kk in range(9):
        apw_ref[0, 3 * kk] = er
        apw_ref[0, 3 * kk + 1] = ei
        apw_ref[0, 3 * kk + 2] = -ei
        er, ei = er * qr - ei * qi, er * qi + ei * qr


def _s5_weights(lam_re, lam_im, log_step, b_re, b_im, c_re, c_im):
    nd, g, p = lam_re.shape
    s = b_re.shape[-1]
    t_blk = S5_BLOCK
    rows = nd * g * s
    rep = lambda t: jnp.repeat(t.reshape(nd * g, p), s, axis=0)
    ls = jnp.broadcast_to(log_step.reshape(nd * g, 1), (nd * g, p))
    tb = lambda t: jnp.transpose(t, (0, 1, 3, 2)).reshape(rows, p)
    shp = jax.ShapeDtypeStruct((rows, p), F32)
    are, aim, bbre, bbim = pl.pallas_call(
        _s5_disc_kernel, out_shape=[shp, shp, shp, shp], name="s5_discretise",
    )(rep(lam_re), rep(lam_im), rep(ls), tb(b_re), tb(b_im))
    gs = LANES // s
    nslab = g // gs
    n = nd * nslab
    tl = t_blk * LANES
    mat = pl.BlockSpec((1, LANES, p), lambda i: (i, 0, 0))
    big = pl.BlockSpec((1, tl, p), lambda i: (i, 0, 0))
    bigshape = jax.ShapeDtypeStruct((n, tl, p), F32)
    slab = lambda t: t.reshape(n, LANES, p)
    kt, dr, di, gr, gi, apw = pl.pallas_call(
        functools.partial(_s5_taps_kernel, reverse_from=nslab),
        grid=(n,),
        in_specs=[mat] * 6,
        out_specs=[pl.BlockSpec((1, LANES, tl), lambda i: (i, 0, 0)), big, big, big, big,
                   pl.BlockSpec((1, 27, LANES, p), lambda i: (i, 0, 0, 0))],
        out_shape=[jax.ShapeDtypeStruct((n, LANES, tl), F32), bigshape, bigshape, bigshape, bigshape,
                   jax.ShapeDtypeStruct((n, 27, LANES, p), F32)],
        scratch_shapes=[pltpu.VMEM((tl, p), F32), pltpu.VMEM((tl, p), F32)],
        compiler_params=_params("parallel"),
        name="s5_block_taps",
    )(slab(are), slab(aim), slab(bbre), slab(bbim), slab(c_re), slab(c_im))

    same = jnp.repeat(jnp.repeat(jnp.eye(gs, dtype=F32), s, axis=0), s, axis=1)
    ti = jnp.arange(t_blk)
    lag = ti[None, :] - ti[:, None]
    kt = kt.reshape(nd, nslab, LANES, t_blk, LANES)

    def toeplitz(tp, lag):
        m = tp[:, :, jnp.clip(lag, 0, t_blk - 1)]
        m = jnp.where((lag >= 0)[None, None, :, :, None], m, 0.0)
        return jnp.transpose(m, (0, 2, 1, 3, 4))

    ktot = ((toeplitz(kt[0], lag) + toeplitz(kt[1], -lag)) * same[None, None, :, None, :]).astype(BF16)
    ktot = ktot.reshape(nslab, tl, tl)
    gmask = jnp.repeat(jnp.eye(gs, dtype=F32), s, axis=0)

    def drive(d):
        dd = jnp.stack([dr.reshape(nd, nslab, t_blk, LANES, p)[d], di.reshape(nd, nslab, t_blk, LANES, p)[d]], 3)
        dd = dd[:, :, :, :, None, :] * gmask[None, None, :, None, :, None]
        return dd.astype(BF16).reshape(nslab, tl, 2 * gs * p)

    w1 = jnp.concatenate([ktot, drive(0), drive(1)], axis=-1)
    gg = jnp.stack([gr.reshape(nd, nslab, tl, p), gi.reshape(nd, nslab, tl, p)], 2)
    gg = jnp.transpose(gg, (1, 0, 2, 4, 3)).reshape(nslab, nd, 2, 1, p, t_blk, gs, s)
    w2 = (gg * jnp.eye(gs, dtype=F32)[None, None, None, :, None, None, :, None]).astype(BF16)
    w2 = w2.reshape(nslab, nd * 2 * gs * p, tl)
    ap = apw.reshape(nd, nslab, 9, 3, gs, s, p)[:, :, :, :, :, 0, :]
    ap = jnp.transpose(ap, (1, 0, 2, 3, 4, 5)).reshape(nslab, nd, 9, 3, gs * p)
    p1 = jnp.concatenate([ap[:, :, :, 0], ap[:, :, :, 0]], axis=-1)
    p2 = jnp.concatenate([ap[:, :, :, 2], ap[:, :, :, 1]], axis=-1)
    return w1, w2, jnp.stack([p1, p2], axis=3).reshape(nslab, nd, 18, 2 * gs * p)


def _s5_kernel(ul_ref, uc_ref, w1_ref, w2_ref, apw_ref, dsk_ref, yl_ref, yc_ref,
               uf, ubuf, zbuf, hbuf, sbuf):
    t_blk = S5_BLOCK
    n_lat, n_ctx = ul_ref.shape[1], uc_ref.shape[1]
    nb_ctx, nb_lat = n_ctx // t_blk, n_lat // t_blk
    nb = nb_ctx + nb_lat
    ts = t_blk * LANES
    hw = apw_ref.shape[3]
    rows = 512

    for src, off, n in ((uc_ref, 0, n_ctx), (ul_ref, n_ctx, n_lat)):
        for r0 in range(0, n, min(rows, n)):
            r1 = min(r0 + rows, n)
            uf[off + r0:off + r1, :] = src[0, r0:r1, :].astype(F32)
    for t in range(t_blk):
        ubuf[:, t * LANES:(t + 1) * LANES] = uf[pl.ds(t, nb, stride=t_blk), :].astype(BF16)
    zbuf[...] = _dot(ubuf[...], w1_ref[0])

    row = lax.broadcasted_iota(jnp.int32, (SUBLANES, hw), 0)

    def cmul(x, p1, p2):
        return x * p1 + pltpu.roll(x, hw // 2, 1) * p2

    for d in range(2):
        col = ts + d * hw
        pw = lambda k, d=d: (apw_ref[0, d, 2 * k:2 * k + 1, :], apw_ref[0, d, 2 * k + 1:2 * k + 2, :])

        def local(i, carry, d=d, col=col, pw=pw):
            r0 = pl.multiple_of(i * SUBLANES, SUBLANES)
            x = zbuf[pl.ds(r0, SUBLANES), col:col + hw]
            for k in (1, 2, 4):
                keep = (row >= k) if d == 0 else (row <= SUBLANES - 1 - k)
                sh = jnp.where(keep, pltpu.roll(x, k if d == 0 else SUBLANES - k, 0), 0.0)
                p1, p2 = pw(k)
                x = x + cmul(sh, p1, p2)
            sbuf[pl.ds(r0, SUBLANES), :] = x
            return carry
        lax.fori_loop(0, nb // SUBLANES, local, 0)

        pex1 = jnp.zeros((SUBLANES, hw), F32)
        pex2 = jnp.zeros((SUBLANES, hw), F32)
        for r in range(SUBLANES):
            p1, p2 = pw(r if d == 0 else SUBLANES - 1 - r)
            pex1 = jnp.where(row == r, p1, pex1)
            pex2 = jnp.where(row == r, p2, pex2)
        p81, p82 = pw(SUBLANES)
        keep1 = (row >= 1) if d == 0 else (row <= SUBLANES - 2)
        last = SUBLANES - 1 if d == 0 else 0
        carry = jnp.zeros((SUBLANES, hw), F32)
        for seg0, ntile in ((0, nb_ctx // SUBLANES), (nb_ctx // SUBLANES, nb_lat // SUBLANES)):
            def chain(i, carry, seg0=seg0, ntile=ntile, d=d):
                ti = seg0 + (i if d == 0 else ntile - 1 - i)
                r0 = pl.multiple_of(ti * SUBLANES, SUBLANES)
                s_t = sbuf[pl.ds(r0, SUBLANES), :]
                excl = jnp.where(keep1, pltpu.roll(s_t, 1 if d == 0 else SUBLANES - 1, 0), 0.0)
                hbuf[pl.ds(r0, SUBLANES), d * hw:(d + 1) * hw] = excl + cmul(carry, pex1, pex2)
                return jnp.broadcast_to(s_t[last:last + 1], (SUBLANES, hw)) + cmul(carry, p81, p82)
            carry = lax.fori_loop(0, ntile, chain, carry)

    zbuf[:, 0:ts] += _dot(hbuf[...].astype(BF16), w2_ref[0])
    dsk = dsk_ref[...]
    for t in range(t_blk):
        y_t = zbuf[:, t * LANES:(t + 1) * LANES] + dsk * uf[pl.ds(t, nb, stride=t_blk), :]
        yc_ref[0, pl.ds(t, nb_ctx, stride=t_blk), :] = y_t[0:nb_ctx]
        yl_ref[0, pl.ds(t, nb_lat, stride=t_blk), :] = y_t[nb_ctx:nb]


def _s5_scan_call(main_l, main_c, w1, w2, apw, d_skip):
    b, l, _ = main_l.shape
    lc = main_c.shape[1]
    nslab, ts, n1 = w1.shape
    hw = apw.shape[3]
    width = d_skip.shape[0]
    nb = (l + lc) // S5_BLOCK
    return pl.pallas_call(
        _s5_kernel,
        grid=(nslab, b),
        in_specs=[pl.BlockSpec((1, l, LANES), lambda s, bi: (bi, 0, s)),
                  pl.BlockSpec((1, lc, LANES), lambda s, bi: (bi, 0, s)),
                  pl.BlockSpec((1, ts, n1), lambda s, bi: (s, 0, 0)),
                  pl.BlockSpec((1, 2 * hw, ts), lambda s, bi: (s, 0, 0)),
                  pl.BlockSpec((1, 2, 18, hw), lambda s, bi: (s, 0, 0, 0)),
                  pl.BlockSpec((1, LANES), lambda s, bi: (0, s))],
        out_specs=[pl.BlockSpec((1, l, LANES), lambda s, bi: (bi, 0, s)),
                   pl.BlockSpec((1, lc, LANES), lambda s, bi: (bi, 0, s))],
        out_shape=[jax.ShapeDtypeStruct((b, l, width), F32),
                   jax.ShapeDtypeStruct((b, lc, width), F32)],
        scratch_shapes=[pltpu.VMEM((l + lc, LANES), F32),
                        pltpu.VMEM((nb, ts), BF16),
                        pltpu.VMEM((nb, n1), F32),
                        pltpu.VMEM((nb, 2 * hw), F32),
                        pltpu.VMEM((nb, hw), F32)],
        compiler_params=_params("parallel", "parallel"),
        name="s5_scan",
    )(main_l, main_c, w1, w2, apw, d_skip.reshape(1, width))


def _s5_glu_kernel(y_ref, w_ref, b_ref, o_ref):
    g = _gelu(y_ref[0])
    o_ref[0] = (g * jax.nn.sigmoid(_dot(g.astype(BF16), w_ref[...]) + b_ref[...])).astype(BF16)


def _s5_glu_call(y, glu_w, glu_b):
    b, l, w = y.shape
    tm = min(l, 1024)
    return pl.pallas_call(
        _s5_glu_kernel,
        grid=(b, l // tm),
        in_specs=[pl.BlockSpec((1, tm, w), lambda bi, i: (bi, i, 0)),
                  pl.BlockSpec((w, w), lambda bi, i: (0, 0)),
                  pl.BlockSpec((1, w), lambda bi, i: (0, 0))],
        out_specs=pl.BlockSpec((1, tm, w), lambda bi, i: (bi, i, 0)),
        out_shape=jax.ShapeDtypeStruct((b, l, w), BF16),
        compiler_params=_params("parallel", "parallel"),
        name="s5_glu",
    )(y, glu_w, glu_b.reshape(1, w))


def _conv_rows(src_ref, n_rows, xpad, cw, bias, emit, tbs):
    zeros = jnp.zeros((SUBLANES, LANES), F32)
    xpad[0:SUBLANES, :] = zeros
    xpad[SUBLANES + n_rows:2 * SUBLANES + n_rows, :] = zeros
    tbs = min(tbs, n_rows)
    for r0 in range(0, n_rows, tbs):
        xpad[SUBLANES + r0:SUBLANES + r0 + tbs, :] = src_ref[0, r0:r0 + tbs, :].astype(F32)
    for r0 in range(0, n_rows, tbs):
        acc = None
        for j in range(SHORT_CONV):
            off = SUBLANES + r0 + j - SHORT_CONV // 2
            term = cw[j:j + 1, :] * xpad[off:off + tbs, :]
            acc = term if acc is None else acc + term
        if bias is not None:
            acc = acc + bias
        emit(r0, acc)


def _lru_kernel(xl_ref, xc_ref, gl_ref, cw_ref, cb_ref, wa_ref, wx_ref, ba_ref, bx_ref, lam_ref, o_ref,
                xpad, xconv, abuf, bbuf, hsum, *, tb):
    n_lat, n_ctx = xl_ref.shape[1], xc_ref.shape[1]
    cw = cw_ref[...]
    cb = cb_ref[...]

    def put(off):
        def emit(r0, y):
            xconv[off + r0:off + r0 + y.shape[0], :] = y
        return emit

    _conv_rows(xc_ref, n_ctx, xpad, cw, cb, put(0), tb)
    _conv_rows(xl_ref, n_lat, xpad, cw, cb, put(n_ctx), tb)

    row = lax.broadcasted_iota(jnp.int32, (SUBLANES, LANES), 0)

    def tile_body(d, ntiles, t, carry):
        ti = t if d == 0 else ntiles - 1 - t
        r0 = pl.multiple_of(ti * SUBLANES, SUBLANES)
        h = bbuf[pl.ds(r0, SUBLANES), :] + abuf[pl.ds(r0, SUBLANES), :] * carry
        bbuf[pl.ds(r0, SUBLANES), :] = h
        last = SUBLANES - 1 if d == 0 else 0
        return jnp.broadcast_to(h[last:last + 1], (SUBLANES, LANES))

    def local_scans(d, a, b):
        nt = a.shape[0] // SUBLANES
        a3 = a.reshape(nt, SUBLANES, LANES)
        b3 = b.reshape(nt, SUBLANES, LANES)
        row3 = lax.broadcasted_iota(jnp.int32, a3.shape, 1)
        for k in (1, 2, 4):
            sh = k if d == 0 else SUBLANES - k
            keep = (row3 >= k) if d == 0 else (row3 <= SUBLANES - 1 - k)
            sa = jnp.where(keep, pltpu.roll(a3, sh, 1), 1.0)
            sb = jnp.where(keep, pltpu.roll(b3, sh, 1), 0.0)
            b3 = b3 + a3 * sb
            a3 = a3 * sa
        return a3.reshape(a.shape), b3.reshape(b.shape)

    for d in range(2):
        sp = _softplus(-lam_ref[d])
        wa, wx = wa_ref[d, 0], wx_ref[d, 0]
        ba, bx = ba_ref[d], bx_ref[d]

        def run_segment(off, n_rows, is_lat, carry, d=d, sp=sp, wa=wa, wx=wx, ba=ba, bx=bx):
            tbs = min(tb, n_rows)
            nblk = n_rows // tbs

            def blk_body(i, carry):
                bi = i if d == 0 else nblk - 1 - i
                lr0 = pl.multiple_of(bi * tbs, tbs)
                xc = xconv[pl.ds(off + lr0, tbs), :]
                xb = xc.astype(BF16)
                r = _sigmoid(_dot(xb, wa) + ba)
                ig = _sigmoid(_dot(xb, wx) + bx)
                log_a = -LRU_C * r * sp
                gain = jnp.sqrt(jnp.maximum(-_expm1_2x(log_a), 0.0))
                acum, hloc = local_scans(d, jnp.exp(log_a), gain * (ig * xc))
                abuf[0:tbs, :] = acum
                bbuf[0:tbs, :] = hloc
                carry = lax.fori_loop(0, tbs // SUBLANES,
                                      functools.partial(tile_body, d, tbs // SUBLANES), carry)
                if is_lat:
                    if d == 0:
                        hsum[pl.ds(lr0, tbs), :] = bbuf[0:tbs, :]
                    else:
                        y = (hsum[pl.ds(lr0, tbs), :] + bbuf[0:tbs, :]) * _gelu(gl_ref[0, pl.ds(lr0, tbs), :].astype(F32))
                        o_ref[0, pl.ds(lr0, tbs), :] = y.astype(BF16)
                return carry

            return lax.fori_loop(0, nblk, blk_body, carry)

        carry = run_segment(0, n_ctx, False, jnp.zeros((SUBLANES, LANES), F32))
        run_segment(n_ctx, n_lat, True, carry)


def _lru_call(main_l, main_c, conv_w, conv_b, wa, ba, wx, bx, lam):
    b, l, _ = main_l.shape
    lc = main_c.shape[1]
    width = conv_b.shape[0]
    nb = width // LRU_BLOCK
    tb = 256
    vec = pl.BlockSpec((2, 1, LANES), lambda bi, n: (0, 0, n))
    wspec = pl.BlockSpec((2, 1, LRU_BLOCK, LRU_BLOCK), lambda bi, n: (0, n, 0, 0))
    return pl.pallas_call(
        functools.partial(_lru_kernel, tb=tb),
        grid=(b, nb),
        in_specs=[pl.BlockSpec((1, l, LANES), lambda bi, n: (bi, 0, n)),
                  pl.BlockSpec((1, lc, LANES), lambda bi, n: (bi, 0, n)),
                  pl.BlockSpec((1, l, LANES), lambda bi, n, nb=nb: (bi, 0, nb + n)),
                  pl.BlockSpec((SHORT_CONV, LANES), lambda bi, n: (0, n)),
                  pl.BlockSpec((1, LANES), lambda bi, n: (0, n)),
                  wspec, wspec, vec, vec, vec],
        out_specs=pl.BlockSpec((1, l, LANES), lambda bi, n: (bi, 0, n)),
        out_shape=jax.ShapeDtypeStruct((b, l, width), BF16),
        scratch_shapes=[pltpu.VMEM((l + 2 * SUBLANES, LANES), F32),
                        pltpu.VMEM((lc + l, LANES), F32),
                        pltpu.VMEM((tb, LANES), F32),
                        pltpu.VMEM((tb, LANES), F32),
                        pltpu.VMEM((l, LANES), F32)],
        compiler_params=_params("parallel", "parallel"),
        name="rglru",
    )(main_l, main_c, main_l, conv_w.T, conv_b.reshape(1, width), wa, wx,
      ba.reshape(2, 1, width), bx.reshape(2, 1, width), lam.reshape(2, 1, width))


def _dn_gate_kernel(ab_ref, alog_ref, dtb_ref, o_ref, *, heads):
    x = ab_ref[0]
    tm = x.shape[0]
    lane = lax.broadcasted_iota(jnp.int32, x.shape, 1)
    g = -jnp.exp(alog_ref[...]) * _softplus(x + dtb_ref[...])
    ii = lax.broadcasted_iota(jnp.int32, (tm, tm), 0)
    jj = lax.broadcasted_iota(jnp.int32, (tm, tm), 1)
    same = (ii // DN_CHUNK) == (jj // DN_CHUNK)
    lower = jnp.where(same & (jj <= ii), 1.0, 0.0).astype(F32)
    upper = jnp.where(same & (jj >= ii), 1.0, 0.0).astype(F32)
    pre = jnp.dot(lower, g, preferred_element_type=F32, precision=lax.Precision.HIGHEST)
    suf = jnp.dot(upper, g, preferred_element_type=F32, precision=lax.Precision.HIGHEST)
    gc = jnp.where(lane < heads, pre, suf)
    o_ref[0] = jnp.where(lane < 2 * heads, gc, jax.nn.sigmoid(x))


def _dn_gate_call(ab, a_log, dt_bias):
    b, l, _ = ab.shape
    tm = min(l, 256)
    pad = lambda t: jnp.pad(t.reshape(1, -1), ((0, 0), (0, LANES - t.size)))
    return pl.pallas_call(
        functools.partial(_dn_gate_kernel, heads=a_log.shape[1]),
        grid=(b, l // tm),
        in_specs=[pl.BlockSpec((1, tm, LANES), lambda bi, i: (bi, i, 0)),
                  pl.BlockSpec((1, LANES), lambda bi, i: (0, 0)),
                  pl.BlockSpec((1, LANES), lambda bi, i: (0, 0))],
        out_specs=pl.BlockSpec((1, tm, LANES), lambda bi, i: (bi, i, 0)),
        out_shape=jax.ShapeDtypeStruct((b, l, LANES), F32),
        compiler_params=_params("parallel", "parallel"),
        name="deltanet_gates",
    )(ab, pad(a_log), pad(dt_bias))


def _dn_gate_layouts(gates, heads):
    b, l, _ = gates.shape
    t = gates[:, :, :4 * heads].reshape(b, l, 2, 2, heads)
    col = jnp.transpose(t, (0, 4, 1, 2, 3)).reshape(b, heads, l, 4)
    row = jnp.transpose(col.reshape(b, heads, l // DN_CHUNK, DN_CHUNK, 4), (0, 1, 2, 4, 3))
    return col, row


def _dn_kernel(ql_ref, kl_ref, vl_ref, gl_ref, qc_ref, kc_ref, vc_ref, gc_ref,
               cwq_ref, cwk_ref, cwv_ref, coll_ref, rowl_ref, colc_ref, rowc_ref, ng_ref,
               ol_ref, oc_ref,
               xpad, qs, ks, vs, oacc, pbuf, xbuf, atb, abuf, bbuf, qpbuf, egl, *, group):
    n_lat, n_ctx = ql_ref.shape[1], qc_ref.shape[1]
    c = DN_CHUNK
    tbs = 256

    def prep(src_ref, n_rows, off, cw_ref, dst, mode):
        def emit(r0, y):
            y = _silu(y)
            if mode != "v":
                y = y * lax.rsqrt(jnp.sum(y * y, axis=-1, keepdims=True) + EPS)
            if mode == "q":
                y = y * (DN_DK ** -0.5)
            dst[off + r0:off + r0 + y.shape[0], :] = y
        _conv_rows(src_ref, n_rows, xpad, cw_ref[...], None, emit, tbs)

    for src_c, src_l, cw_ref, dst, mode in ((qc_ref, ql_ref, cwq_ref, qs, "q"),
                                            (kc_ref, kl_ref, cwk_ref, ks, "k"),
                                            (vc_ref, vl_ref, cwv_ref, vs, "v")):
        prep(src_c, n_ctx, 0, cw_ref, dst, mode)
        prep(src_l, n_lat, n_ctx, cw_ref, dst, mode)

    ii = lax.broadcasted_iota(jnp.int32, (c, c), 0)
    jj = lax.broadcasted_iota(jnp.int32, (c, c), 1)
    n_apply = int(math.log2(c))
    segments = ((0, n_ctx, colc_ref, rowc_ref), (n_ctx, n_lat, coll_ref, rowl_ref))

    def phase1_group(it, grp, off, col_ref, row_ref):
        def chunk_ids(g):
            ci = it * grp + g
            return ci, pl.multiple_of(ci * c, c), pl.multiple_of(off + ci * c, c), off // c + ci

        def gates(col_ref, lr0, d):
            gcb = col_ref[0, 0, pl.ds(lr0, c), :]
            return gcb[:, d:d + 1], gcb[:, 2 + d:3 + d]

        for g in range(grp):
            ci, lr0, r0, _ = chunk_ids(g)
            q = qs[pl.ds(r0, c), :]
            k = ks[pl.ds(r0, c), :]
            v = vs[pl.ds(r0, c), :]
            kbf = k.astype(BF16)
            kk = _dot_nt(kbf, kbf)
            qk = _dot_nt(q.astype(BF16), kbf)
            rows4 = row_ref[0, 0, ci]
            for d in range(2):
                causal = (ii >= jj) if d == 0 else (ii <= jj)
                strict = (ii > jj) if d == 0 else (ii < jj)
                gc_col, beta = gates(col_ref, lr0, d)
                gc_row = rows4[d:d + 1, :]
                gam = jnp.where(causal, jnp.exp(jnp.where(causal, gc_col - gc_row, 0.0)), 0.0)
                pbuf[0, 2 * g + d] = jnp.where(strict, -(beta * kk) * gam, 0.0).astype(BF16)
                xbuf[2 * g + d] = jnp.concatenate([v * beta, (k * beta) * jnp.exp(gc_col)], axis=1)
                atb[2 * g + d] = (qk * gam).astype(BF16)

        for j in range(1, n_apply):
            for ch in range(2 * grp):
                pb = pbuf[j - 1, ch]
                pbuf[j, ch] = _dot(pb, pb).astype(BF16)
        for j in reversed(range(n_apply)):
            for ch in range(2 * grp):
                xv = xbuf[ch]
                xbuf[ch] = xv + _dot(pbuf[j, ch], xv.astype(BF16))

        for g in range(grp):
            ci, lr0, r0, cg = chunk_ids(g)
            q = qs[pl.ds(r0, c), :]
            k = ks[pl.ds(r0, c), :]
            oloc = None
            for d in range(2):
                gc_col, _ = gates(col_ref, lr0, d)
                g_last = gc_col[c - 1:c, :] if d == 0 else gc_col[0:1, :]
                kdt = (k * jnp.exp(g_last - gc_col)).T.astype(BF16)
                xv = xbuf[2 * g + d]
                ub = xv[:, 0:LANES].astype(BF16)
                wb = xv[:, LANES:2 * LANES].astype(BF16)
                at = atb[2 * g + d]
                abuf[d, cg] = _dot(kdt, wb).astype(BF16)
                bbuf[d, cg] = _dot(kdt, ub).astype(BF16)
                qpbuf[d, pl.ds(r0, c), :] = (q * jnp.exp(gc_col) - _dot(at, wb)).astype(BF16)
                part = _dot(at, ub)
                oloc = part if oloc is None else oloc + part
                egl[d, pl.ds(cg, 1), :] = jnp.broadcast_to(jnp.exp(g_last), (1, LANES))
            oacc[pl.ds(r0, c), :] = oloc

    for off, n_rows, col_ref, row_ref in segments:
        nch = n_rows // c
        grp = min(group, nch)

        def p1_body(it, carry, off=off, col_ref=col_ref, row_ref=row_ref, grp=grp):
            phase1_group(it, grp, off, col_ref, row_ref)
            return carry
        lax.fori_loop(0, nch // grp, p1_body, 0)

    def phase2(d, ci, state, off):
        r0 = pl.multiple_of(off + ci * c, c)
        cg = off // c + ci
        sb = state.astype(BF16)
        oacc[pl.ds(r0, c), :] += _dot(qpbuf[d, pl.ds(r0, c), :], sb)
        return state * egl[d, pl.ds(cg, 1), :] - _dot(abuf[d, cg], sb) + bbuf[d, cg].astype(F32)

    states = (jnp.zeros((DN_DK, LANES), F32), jnp.zeros((DN_DK, LANES), F32))
    for off, n_rows, _, _ in segments:
        nch = n_rows // c

        def p2_body(i, st, off=off, nch=nch):
            return (phase2(0, i, st[0], off), phase2(1, nch - 1 - i, st[1], off))
        states = lax.fori_loop(0, nch, p2_body, states)

    ng = ng_ref[...]
    for off, n_rows, g_ref, o_ref in ((0, n_ctx, gc_ref, oc_ref), (n_ctx, n_lat, gl_ref, ol_ref)):
        t = min(tbs, n_rows)
        for r0 in range(0, n_rows, t):
            o = oacc[off + r0:off + r0 + t, :]
            y = o * lax.rsqrt(jnp.mean(o * o, axis=-1, keepdims=True) + EPS) * ng
            o_ref[0, r0:r0 + t, :] = (y * _silu(g_ref[0, r0:r0 + t, :].astype(F32))).astype(BF16)


def _dn_call(main_l, main_c, col_l, row_l, col_c, row_c, conv_w, norm_g, col0):
    b, l, _ = main_l.shape
    lc = main_c.shape[1]
    heads = col_l.shape[1]
    lt = l + lc
    nch = lt // DN_CHUNK
    cwt = conv_w.T
    group = 8

    def blk(n_rows, which):
        return pl.BlockSpec((1, n_rows, LANES), lambda bi, h, which=which: (bi, 0, col0 + which * heads + h))

    def cw(which):
        return pl.BlockSpec((SHORT_CONV, LANES), lambda bi, h, which=which: (0, which * heads + h))

    in_specs = ([blk(l, w) for w in range(4)] + [blk(lc, w) for w in range(4)] + [cw(0), cw(1), cw(2)] +
                [pl.BlockSpec((1, 1, l, 4), lambda bi, h: (bi, h, 0, 0)),
                 pl.BlockSpec((1, 1, l // DN_CHUNK, 4, DN_CHUNK), lambda bi, h: (bi, h, 0, 0, 0)),
                 pl.BlockSpec((1, 1, lc, 4), lambda bi, h: (bi, h, 0, 0)),
                 pl.BlockSpec((1, 1, lc // DN_CHUNK, 4, DN_CHUNK), lambda bi, h: (bi, h, 0, 0, 0)),
                 pl.BlockSpec((1, LANES), lambda bi, h: (0, 0))])
    return pl.pallas_call(
        functools.partial(_dn_kernel, group=group),
        grid=(b, heads),
        in_specs=in_specs,
        out_specs=[pl.BlockSpec((1, l, LANES), lambda bi, h: (bi, 0, h)),
                   pl.BlockSpec((1, lc, LANES), lambda bi, h: (bi, 0, h))],
        out_shape=[jax.ShapeDtypeStruct((b, l, heads * LANES), BF16),
                   jax.ShapeDtypeStruct((b, lc, heads * LANES), BF16)],
        scratch_shapes=[pltpu.VMEM((l + 2 * SUBLANES, LANES), F32),
                        pltpu.VMEM((lt, LANES), F32),
                        pltpu.VMEM((lt, LANES), F32),
                        pltpu.VMEM((lt, LANES), F32),
                        pltpu.VMEM((lt, LANES), F32),
                        pltpu.VMEM((6, 2 * group, DN_CHUNK, DN_CHUNK), BF16),
                        pltpu.VMEM((2 * group, DN_CHUNK, 2 * LANES), F32),
                        pltpu.VMEM((2 * group, DN_CHUNK, DN_CHUNK), BF16),
                        pltpu.VMEM((2, nch, DN_DK, LANES), BF16),
                        pltpu.VMEM((2, nch, DN_DK, LANES), BF16),
                        pltpu.VMEM((2, lt, LANES), BF16),
                        pltpu.VMEM((2, nch, LANES), F32)],
        compiler_params=_params("parallel", "parallel"),
        name="gated_deltanet",
    )(main_l, main_l, main_l, main_l, main_c, main_c, main_c, main_c, cwt, cwt, cwt,
      col_l, row_l, col_c, row_c, norm_g.reshape(1, LANES))


def _ret_kernel(th_ref, ql_ref, kl_ref, vl_ref, qc_ref, kc_ref, vc_ref, cos_ref, sin_ref, o_ref,
                qs, kts, s_ref, dec_ref, xz_ref):
    n_lat, n_ctx = ql_ref.shape[1], qc_ref.shape[1]
    c = RET_CHUNK
    half = RET_DK // 2
    ncc, nlc = n_ctx // c, n_lat // c
    scale = RET_DK ** -0.5

    icol = lax.broadcasted_iota(jnp.int32, (c, RET_DV), 0)
    ii = lax.broadcasted_iota(jnp.int32, (c, c), 0)
    jj = lax.broadcasted_iota(jnp.int32, (c, c), 1)
    gch = []
    for d in range(2):
        lg = -jnp.exp(th_ref[d, 0])
        lg1 = lg[:, 0:1]
        fidx = (icol if d == 0 else c - 1 - icol).astype(F32)
        rel = (ii - jj) if d == 0 else (jj - ii)
        mask = rel >= 0
        dec_ref[d] = jnp.where(mask, jnp.exp(jnp.where(mask, rel, 0).astype(F32) * lg), 0.0)
        xz_ref[d, 0] = jnp.exp((fidx + 1.0) * lg1)
        xz_ref[d, 1] = jnp.exp((c - 1.0 - fidx) * lg1)
        gch.append(jnp.exp(c * lg1))
    s_ref[...] = jnp.zeros_like(s_ref)

    def prep_ctx(ci, carry):
        r0 = pl.multiple_of(ci * c, c)
        qs[pl.ds(r0, c), :] = qc_ref[0, pl.ds(r0, c), :].astype(BF16)
        k = kc_ref[0, pl.ds(r0, c), :].astype(F32) * scale
        kts[ci] = k.T.astype(BF16)
        return carry

    def prep_lat(ci, carry):
        r0 = pl.multiple_of(ci * c, c)
        cos = cos_ref[pl.ds(r0, c), :]
        sin = sin_ref[pl.ds(r0, c), :]

        def rope(t):
            t1, t2 = t[:, 0:half], t[:, half:2 * half]
            return jnp.concatenate([t1 * cos - t2 * sin, t1 * sin + t2 * cos], axis=-1)

        q = rope(ql_ref[0, pl.ds(r0, c), :].astype(F32))
        k = rope(kl_ref[0, pl.ds(r0, c), :].astype(F32)) * scale
        qs[pl.ds(pl.multiple_of(n_ctx + ci * c, c), c), :] = q.astype(BF16)
        kts[ncc + ci] = k.T.astype(BF16)
        return carry

    lax.fori_loop(0, ncc, prep_ctx, 0)
    lax.fori_loop(0, nlc, prep_lat, 0)

    def one(d, ci, row_off, ch_off, v_ref, write, first):
        r0 = pl.multiple_of(ci * c, c)
        qb = qs[pl.ds(pl.multiple_of(row_off + ci * c, c), c), :]
        kt = kts[ch_off + ci]
        v = v_ref[0, pl.ds(r0, c), :].astype(BF16)
        state = s_ref[d]
        if write:
            inner = _dot(qb, kt) * dec_ref[d]
            o = _dot(inner.astype(BF16), v) + xz_ref[d, 0] * _dot(qb, state.astype(BF16))
            if first:
                o_ref[0, pl.ds(r0, c), :] = o
            else:
                o_ref[0, pl.ds(r0, c), :] += o
        vz = (v.astype(F32) * xz_ref[d, 1]).astype(BF16)
        s_ref[d] = state * gch[d] + _dot(kt, vz)

    def ctx_body(i, carry):
        one(0, i, 0, 0, vc_ref, False, False)
        one(1, ncc - 1 - i, 0, 0, vc_ref, False, False)
        return carry

    lax.fori_loop(0, ncc, ctx_body, 0)
    for lo, hi, first in ((0, nlc // 2, True), (nlc // 2, nlc, False)):
        def lat_body(i, carry, first=first):
            one(0, i, n_ctx, ncc, vl_ref, True, first)
            one(1, nlc - 1 - i, n_ctx, ncc, vl_ref, True, first)
            return carry
        lax.fori_loop(lo, hi, lat_body, 0)


def _ret_call(main_l, main_c, theta, cos, sin, q_col0):
    b, l, _ = main_l.shape
    lc = main_c.shape[1]
    heads = theta.shape[1]
    c = RET_CHUNK
    qb0 = q_col0 // RET_DK
    kb0 = qb0 + heads
    vb0 = (q_col0 + 2 * heads * RET_DK) // RET_DV
    th = jnp.broadcast_to(theta.reshape(2, heads, 1, 1), (2, heads, 1, LANES))
    return pl.pallas_call(
        _ret_kernel,
        grid=(b, heads),
        in_specs=[pl.BlockSpec((2, 1, 1, LANES), lambda bi, h: (0, h, 0, 0)),
                  pl.BlockSpec((1, l, RET_DK), lambda bi, h: (bi, 0, qb0 + h)),
                  pl.BlockSpec((1, l, RET_DK), lambda bi, h: (bi, 0, kb0 + h)),
                  pl.BlockSpec((1, l, RET_DV), lambda bi, h: (bi, 0, vb0 + h)),
                  pl.BlockSpec((1, lc, RET_DK), lambda bi, h: (bi, 0, qb0 + h)),
                  pl.BlockSpec((1, lc, RET_DK), lambda bi, h: (bi, 0, kb0 + h)),
                  pl.BlockSpec((1, lc, RET_DV), lambda bi, h: (bi, 0, vb0 + h)),
                  pl.BlockSpec((l, RET_DK // 2), lambda bi, h: (0, 0)),
                  pl.BlockSpec((l, RET_DK // 2), lambda bi, h: (0, 0))],
        out_specs=pl.BlockSpec((1, l, RET_DV), lambda bi, h: (bi, 0, h)),
        out_shape=jax.ShapeDtypeStruct((b, l, heads * RET_DV), F32),
        scratch_shapes=[pltpu.VMEM((lc + l, RET_DK), BF16),
                        pltpu.VMEM(((lc + l) // c, RET_DK, c), BF16),
                        pltpu.VMEM((2, RET_DK, RET_DV), F32),
                        pltpu.VMEM((2, c, c), F32),
                        pltpu.VMEM((2, 2, c, RET_DV), F32)],
        compiler_params=_params("parallel", "parallel"),
        name="retention",
    )(th, main_l, main_l, main_l, main_c, main_c, main_c, cos, sin)


def _ret_finish_kernel(o_ref, r_ref, y_ref):
    o = o_ref[0]
    y = o * lax.rsqrt(jnp.mean(o * o, axis=-1, keepdims=True) + EPS)
    y_ref[0] = (y * _silu(r_ref[0].astype(F32))).astype(BF16)


def _ret_finish_call(o, main_l, r_col0):
    b, l, w = o.shape
    heads = w // RET_DV
    tm = min(l, 1024)
    rb0 = r_col0 // RET_DV
    return pl.pallas_call(
        _ret_finish_kernel,
        grid=(b, l // tm, heads),
        in_specs=[pl.BlockSpec((1, tm, RET_DV), lambda bi, i, h: (bi, i, h)),
                  pl.BlockSpec((1, tm, RET_DV), lambda bi, i, h: (bi, i, rb0 + h))],
        out_specs=pl.BlockSpec((1, tm, RET_DV), lambda bi, i, h: (bi, i, h)),
        out_shape=jax.ShapeDtypeStruct((b, l, w), BF16),
        compiler_params=_params("parallel", "parallel", "parallel"),
        name="retention_finish",
    )(o, main_l)


def _rope_tables(n_tokens):
    rows = n_tokens // GRID_W
    r, col = jnp.meshgrid(jnp.arange(rows), jnp.arange(GRID_W), indexing='ij')
    n_freq = RET_DK // 4
    inv = ROPE_BASE ** (-jnp.arange(n_freq, dtype=F32) / n_freq)
    ang = jnp.concatenate([r.reshape(-1, 1) * inv, col.reshape(-1, 1) * inv], axis=-1)
    return jnp.cos(ang), jnp.sin(ang)


def _even_layer(xl, xc, mods_l, mods_c, norm1_g, norm2_g, w_in, w_out, s5p, glu_w, glu_b,
                dn_conv_w, dn_a_log, dn_dt_bias, dn_norm_g, ffn, final_g, final_norm):
    s5_width = glu_w.shape[0]
    dn_width = dn_conv_w.shape[0] // 3
    heads = dn_width // LANES
    n_main = s5_width + 4 * dn_width
    w_main = w_in[:, :n_main].astype(BF16)
    w_small = jnp.pad(w_in[:, n_main:], ((0, 0), (0, LANES - (w_in.shape[1] - n_main)))).astype(BF16)

    main_l, ab_l = _inproj_call(xl, norm1_g, mods_l[0], mods_l[1], w_main, w_small)
    main_c, ab_c = _inproj_call(xc, norm1_g, mods_c[0], mods_c[1], w_main, w_small)

    y_l, y_c = _s5_scan_call(main_l, main_c, *s5p)
    glu_wb = glu_w.astype(BF16)
    s5_l = _s5_glu_call(y_l, glu_wb, glu_b)
    s5_c = _s5_glu_call(y_c, glu_wb, glu_b)

    col_l, row_l = _dn_gate_layouts(_dn_gate_call(ab_l, dn_a_log, dn_dt_bias), heads)
    col_c, row_c = _dn_gate_layouts(_dn_gate_call(ab_c, dn_a_log, dn_dt_bias), heads)
    dn_l, dn_c = _dn_call(main_l, main_c, col_l, row_l, col_c, row_c, dn_conv_w, dn_norm_g,
                          s5_width // LANES)

    w_o1 = w_out[:s5_width].astype(BF16)
    w_o2 = w_out[s5_width:].astype(BF16)
    xl = _outproj_call(xl, mods_l[2], s5_l, dn_l, w_o1, w_o2)
    xc = _outproj_call(xc, mods_c[2], s5_c, dn_c, w_o1, w_o2)
    w1, w3, w2 = ffn
    xl = _ffn_call(xl, norm2_g, mods_l[3], mods_l[4], mods_l[5], w1, w3, w2, final_g, final_norm)
    xc = _ffn_call(xc, norm2_g, mods_c[3], mods_c[4], mods_c[5], w1, w3, w2, final_g, False)
    return xl, xc


def _odd_layer(xl, xc, mods_l, mods_c, norm1_g, norm2_g, w_in, w_out, conv_w, conv_b, wa, ba, wx, bx, lam,
               theta, cos, sin, ffn, final_g, final_norm):
    lru_width = conv_b.shape[0]
    heads = theta.shape[1]
    w_inb = w_in.astype(BF16)
    main_l = _inproj_call(xl, norm1_g, mods_l[0], mods_l[1], w_inb)
    main_c = _inproj_call(xc, norm1_g, mods_c[0], mods_c[1], w_inb)

    lru_l = _lru_call(main_l, main_c, conv_w, conv_b, wa.astype(BF16), ba, wx.astype(BF16), bx, lam)
    q_col0 = 2 * lru_width
    o = _ret_call(main_l, main_c, theta, cos, sin, q_col0)
    ret_l = _ret_finish_call(o, main_l, q_col0 + 2 * heads * RET_DK + heads * RET_DV)

    w_o1 = w_out[:lru_width].astype(BF16)
    w_o2 = w_out[lru_width:].astype(BF16)
    xl = _outproj_call(xl, mods_l[2], lru_l, ret_l, w_o1, w_o2)
    w1, w3, w2 = ffn
    return _ffn_call(xl, norm2_g, mods_l[3], mods_l[4], mods_l[5], w1, w3, w2, final_g, final_norm)


def kernel(x, c, ctx, c_ctx, mod_w, mod_b, norm1_g, norm2_g, ffn_w1, ffn_w3, ffn_w2, final_g, even_w_in, even_w_out, s5_lam_re, s5_lam_im, s5_log_step, s5_b_re, s5_b_im, s5_c_re, s5_c_im, s5_d, s5_glu_w, s5_glu_b, dn_conv_w, dn_a_log, dn_dt_bias, dn_norm_g, odd_w_in, odd_w_out, lru_conv_w, lru_conv_b, lru_wa, lru_ba, lru_wx, lru_bx, lru_lam, ret_theta):
    bsz, n_tok, d = x.shape
    depth = mod_w.shape[0]
    assert depth == 2 and bsz + 1 <= SUBLANES
    cos, sin = _rope_tables(n_tok)

    rows = jnp.concatenate([c, c_ctx[None, :], jnp.zeros((SUBLANES - bsz - 1, d), F32)], axis=0)
    mods = _mod_call(rows, mod_w, mod_b)

    def split_mods(i):
        m = mods[i].reshape(SUBLANES, 6, d)
        ml = [m[:bsz, k][:, None, :] for k in range(6)]
        mc = [jnp.broadcast_to(m[bsz, k][None, None, :], (bsz, 1, d)) for k in range(6)]
        return ml, mc

    xl, xc = x, ctx
    ml, mc = split_mods(0)
    s5p = _s5_weights(s5_lam_re[0], s5_lam_im[0], s5_log_step[0], s5_b_re[0], s5_b_im[0],
                      s5_c_re[0], s5_c_im[0]) + (s5_d[0],)
    ffn0 = (ffn_w1[0].astype(BF16), ffn_w3[0].astype(BF16), ffn_w2[0].astype(BF16))
    xl, xc = _even_layer(xl, xc, ml, mc, norm1_g[0], norm2_g[0], even_w_in[0], even_w_out[0],
                         s5p, s5_glu_w[0], s5_glu_b[0],
                         dn_conv_w[0], dn_a_log[0], dn_dt_bias[0], dn_norm_g[0], ffn0, final_g, False)
    ml, mc = split_mods(1)
    ffn1 = (ffn_w1[1].astype(BF16), ffn_w3[1].astype(BF16), ffn_w2[1].astype(BF16))
    return _odd_layer(xl, xc, ml, mc, norm1_g[1], norm2_g[1], odd_w_in[0], odd_w_out[0],
                      lru_conv_w[0], lru_conv_b[0], lru_wa[0], lru_ba[0], lru_wx[0], lru_bx[0], lru_lam[0],
                      ret_theta[0], cos, sin, ffn1, final_g, True)
```

```python
import functools
import math

import jax
import jax.numpy as jnp
from jax import lax
from jax.experimental import pallas as pl
from jax.experimental.pallas import tpu as pltpu

F32 = jnp.float32
BF16 = jnp.bfloat16

EPS = 1e-6
GRID_W = 64
ROPE_BASE = 10000.0

S5_GROUP = 16
S5_STATE = 64
S5_BLOCK = 8

DN_DK = 128
DN_CHUNK = 64
SHORT_CONV = 4

LRU_BLOCK = 128
LRU_C = 8.0

RET_DK = 256
RET_DV = 512
RET_CHUNK = 128

LANES = 128
SUBLANES = 8
VMEM_LIMIT = 56 * 1024 * 1024


def _params(*sem):
    return pltpu.CompilerParams(dimension_semantics=sem, vmem_limit_bytes=VMEM_LIMIT)


def _silu(x):
    return x * jax.nn.sigmoid(x)


def _gelu(x):
    return 0.5 * x * (1.0 + jnp.tanh(0.7978845608028654 * (x + 0.044715 * (x * x * x))))


def _softplus(x):
    return jnp.maximum(x, 0.0) + jnp.log1p(jnp.exp(-jnp.abs(x)))


def _sigmoid(x):
    return 0.5 * (1.0 + jnp.tanh(0.5 * x))


def _expm1_2x(x):
    t = jnp.tanh(x)
    return 2.0 * t / (1.0 - t)


def _adaln(x, g, shift, scale):
    y = x * lax.rsqrt(jnp.mean(x * x, axis=-1, keepdims=True) + EPS)
    return (y * g) * (1.0 + scale) + shift


def _dot(a, b):
    return jnp.dot(a, b, preferred_element_type=F32)


def _dot_nt(a, b):
    return lax.dot_general(a, b, (((1,), (1,)), ((), ())), preferred_element_type=F32)


def _mod_kernel(s_ref, w_ref, b_ref, o_ref):
    s = _silu(s_ref[...])
    o_ref[0] = _dot(s.astype(BF16), w_ref[0].astype(BF16)) + b_ref[0]


def _mod_call(rows, mod_w, mod_b):
    depth, d, n = mod_w.shape
    tn = 1024
    return pl.pallas_call(
        _mod_kernel,
        grid=(depth, n // tn),
        in_specs=[pl.BlockSpec((SUBLANES, d), lambda i, j: (0, 0)),
                  pl.BlockSpec((1, d, tn), lambda i, j: (i, 0, j)),
                  pl.BlockSpec((1, 1, tn), lambda i, j: (i, 0, j))],
        out_specs=pl.BlockSpec((1, SUBLANES, tn), lambda i, j: (i, 0, j)),
        out_shape=jax.ShapeDtypeStruct((depth, SUBLANES, n), F32),
        compiler_params=_params("parallel", "parallel"),
        name="mod_proj",
    )(rows, mod_w, mod_b.reshape(depth, 1, n))


def _inproj_kernel(x_ref, g_ref, sh_ref, sc_ref, w_ref, *rest, has_small):
    if has_small:
        ws_ref, o_ref, os_ref, h_ref = rest
    else:
        o_ref, h_ref = rest

    @pl.when(pl.program_id(2) == 0)
    def _():
        h = _adaln(x_ref[0], g_ref[...], sh_ref[0], sc_ref[0]).astype(BF16)
        h_ref[...] = h
        if has_small:
            os_ref[0] = _dot(h, ws_ref[...])

    o_ref[0] = _dot(h_ref[...], w_ref[...]).astype(o_ref.dtype)


def _inproj_call(x, g, shift, scale, w, w_small=None):
    b, l, d = x.shape
    n = w.shape[1]
    tm = min(l, 1024)
    tn = 512
    has_small = w_small is not None
    in_specs = [pl.BlockSpec((1, tm, d), lambda bi, i, j: (bi, i, 0)),
                pl.BlockSpec((1, d), lambda bi, i, j: (0, 0)),
                pl.BlockSpec((1, 1, d), lambda bi, i, j: (bi, 0, 0)),
                pl.BlockSpec((1, 1, d), lambda bi, i, j: (bi, 0, 0)),
                pl.BlockSpec((d, tn), lambda bi, i, j: (0, j))]
    out_specs = [pl.BlockSpec((1, tm, tn), lambda bi, i, j: (bi, i, j))]
    out_shape = [jax.ShapeDtypeStruct((b, l, n), BF16)]
    args = [x, g.reshape(1, d), shift, scale, w]
    if has_small:
        in_specs.append(pl.BlockSpec((d, LANES), lambda bi, i, j: (0, 0)))
        out_specs.append(pl.BlockSpec((1, tm, LANES), lambda bi, i, j: (bi, i, 0)))
        out_shape.append(jax.ShapeDtypeStruct((b, l, LANES), F32))
        args.append(w_small)
    outs = pl.pallas_call(
        functools.partial(_inproj_kernel, has_small=has_small),
        grid=(b, l // tm, n // tn),
        in_specs=in_specs, out_specs=out_specs, out_shape=out_shape,
        scratch_shapes=[pltpu.VMEM((tm, d), BF16)],
        compiler_params=_params("parallel", "parallel", "arbitrary"),
        name="adaln_inproj",
    )(*args)
    return outs if has_small else outs[0]


def _outproj_kernel(x_ref, gate_ref, a1_ref, a2_ref, w1_ref, w2_ref, o_ref):
    y = _dot(a1_ref[0], w1_ref[...]) + _dot(a2_ref[0], w2_ref[...])
    o_ref[0] = x_ref[0] + gate_ref[0] * y


def _outproj_call(x, gate, a1, a2, w1, w2):
    b, l, d = x.shape
    k1, k2 = a1.shape[2], a2.shape[2]
    tm = min(l, 1024)
    tn = 512
    return pl.pallas_call(
        _outproj_kernel,
        grid=(b, l // tm, d // tn),
        in_specs=[pl.BlockSpec((1, tm, tn), lambda bi, i, j: (bi, i, j)),
                  pl.BlockSpec((1, 1, tn), lambda bi, i, j: (bi, 0, j)),
                  pl.BlockSpec((1, tm, k1), lambda bi, i, j: (bi, i, 0)),
                  pl.BlockSpec((1, tm, k2), lambda bi, i, j: (bi, i, 0)),
                  pl.BlockSpec((k1, tn), lambda bi, i, j: (0, j)),
                  pl.BlockSpec((k2, tn), lambda bi, i, j: (0, j))],
        out_specs=pl.BlockSpec((1, tm, tn), lambda bi, i, j: (bi, i, j)),
        out_shape=jax.ShapeDtypeStruct((b, l, d), F32),
        compiler_params=_params("parallel", "parallel", "arbitrary"),
        name="outproj_residual",
    )(x, gate, a1, a2, w1, w2)


def _ffn_kernel(x_ref, g_ref, sh_ref, sc_ref, gate_ref, w1_ref, w3_ref, w2_ref, fg_ref, o_ref,
                h_ref, acc_ref, *, final_norm):
    f = pl.program_id(2)

    @pl.when(f == 0)
    def _():
        h_ref[...] = _adaln(x_ref[0], g_ref[...], sh_ref[0], sc_ref[0]).astype(BF16)
        acc_ref[...] = jnp.zeros_like(acc_ref)

    h = h_ref[...]
    a = _dot(h, w1_ref[...])
    b = _dot(h, w3_ref[...])
    acc_ref[...] += _dot((_silu(a) * b).astype(BF16), w2_ref[...])

    @pl.when(f == pl.num_programs(2) - 1)
    def _():
        y = x_ref[0] + gate_ref[0] * acc_ref[...]
        if final_norm:
            y = y * lax.rsqrt(jnp.mean(y * y, axis=-1, keepdims=True) + EPS) * fg_ref[...]
        o_ref[0] = y


def _ffn_call(x, g, shift, scale, gate, w1, w3, w2, final_g, final_norm):
    b, l, d = x.shape
    dff = w1.shape[1]
    tm = min(l, 512)
    tf = 512
    vec = pl.BlockSpec((1, 1, d), lambda bi, i, f: (bi, 0, 0))
    return pl.pallas_call(
        functools.partial(_ffn_kernel, final_norm=final_norm),
        grid=(b, l // tm, dff // tf),
        in_specs=[pl.BlockSpec((1, tm, d), lambda bi, i, f: (bi, i, 0)),
                  pl.BlockSpec((1, d), lambda bi, i, f: (0, 0)),
                  vec, vec, vec,
                  pl.BlockSpec((d, tf), lambda bi, i, f: (0, f)),
                  pl.BlockSpec((d, tf), lambda bi, i, f: (0, f)),
                  pl.BlockSpec((tf, d), lambda bi, i, f: (f, 0)),
                  pl.BlockSpec((1, d), lambda bi, i, f: (0, 0))],
        out_specs=pl.BlockSpec((1, tm, d), lambda bi, i, f: (bi, i, 0)),
        out_shape=jax.ShapeDtypeStruct((b, l, d), F32),
        scratch_shapes=[pltpu.VMEM((tm, d), BF16), pltpu.VMEM((tm, d), F32)],
        compiler_params=_params("parallel", "parallel", "arbitrary"),
        name="ffn_swiglu",
    )(x, g.reshape(1, d), shift, scale, gate, w1, w3, w2, final_g.reshape(1, d))


def _s5_disc_kernel(lre_ref, lim_ref, ls_ref, bre_ref, bim_ref, are_ref, aim_ref, bbre_ref, bbim_ref):
    lre = jnp.minimum(lre_ref[...], -1e-4)
    lim = lim_ref[...]
    dt = jnp.exp(ls_ref[...])
    mag = jnp.exp(lre * dt)
    ar = mag * jnp.cos(lim * dt)
    ai = mag * jnp.sin(lim * dt)
    nr, ni = ar - 1.0, ai
    den = lre * lre + lim * lim
    cr = (nr * lre + ni * lim) / den
    ci = (ni * lre - nr * lim) / den
    bre, bim = bre_ref[...], bim_ref[...]
    are_ref[...] = ar
    aim_ref[...] = ai
    bbre_ref[...] = cr * bre - ci * bim
    bbim_ref[...] = cr * bim + ci * bre


def _s5_taps_kernel(ar_ref, ai_ref, br_ref, bi_ref, cr_ref, ci_ref, w1_ref, w2_ref, apw_ref, ccr, cci, ktb):
    t_blk = S5_BLOCK
    rows, p = ar_ref.shape[2], ar_ref.shape[3]
    s = S5_GROUP
    gs = rows // s
    tl = t_blk * rows
    hi = lax.Precision.HIGHEST
    nt = (((1,), (1,)), ((), ()))
    iota = lambda shape, ax: lax.broadcasted_iota(jnp.int32, shape, ax)
    same = (iota((rows, rows), 0) // s) == (iota((rows, rows), 1) // s)
    dmask = (iota((rows, gs * p), 0) // s) == (iota((rows, gs * p), 1) // p)
    gmask = (iota((gs * p, rows), 0) // p) == (iota((gs * p, rows), 1) // s)

    for d in range(2):
        ar, ai = ar_ref[d, 0], ai_ref[d, 0]
        br, bi = br_ref[d, 0], bi_ref[d, 0]
        cr, ci = cr_ref[d, 0], ci_ref[d, 0]
        pw = [(jnp.ones_like(ar), jnp.zeros_like(ar))]
        for _ in range(t_blk):
            pr, pi = pw[-1]
            pw.append((pr * ar - pi * ai, pr * ai + pi * ar))
        for t in range(t_blk):
            sl = slice(t * rows, (t + 1) * rows)
            pr, pi = pw[t]
            ccr[sl, :] = cr * pr - ci * pi
            cci[sl, :] = cr * pi + ci * pr
            dpr, dpi = pw[t_blk - 1 - t] if d == 0 else pw[t]
            for c, piece in enumerate((br * dpr - bi * dpi, br * dpi + bi * dpr)):
                wide = jnp.concatenate([piece] * gs, axis=1)
                c0 = tl + (2 * d + c) * gs * p
                w1_ref[0, sl, c0:c0 + gs * p] = jnp.where(dmask, wide, 0.0).astype(BF16)
            gpr, gpi = pw[t + 1] if d == 0 else pw[t_blk - t]
            for c, piece in enumerate((cr * gpr - ci * gpi, -(cr * gpi + ci * gpr))):
                tall = jnp.concatenate([piece.T] * gs, axis=0)
                r0 = (2 * d + c) * gs * p
                w2_ref[0, r0:r0 + gs * p, sl] = jnp.where(gmask, tall, 0.0).astype(BF16)
        ktb[d] = (lax.dot_general(br, ccr[...], nt, precision=hi, preferred_element_type=F32)
                  - lax.dot_general(bi, cci[...], nt, precision=hi, preferred_element_type=F32))
        qr, qi = pw[t_blk]
        er, ei = jnp.ones_like(qr), jnp.zeros_like(qr)
        for kk in range(9):
            apw_ref[0, d, 3 * kk] = er
            apw_ref[0, d, 3 * kk + 1] = ei
            apw_ref[0, d, 3 * kk + 2] = -ei
            er, ei = er * qr - ei * qi, er * qi + ei * qr

    for t_in in range(t_blk):
        for t_out in range(t_blk):
            if t_out >= t_in:
                lag = t_out - t_in
                piece = ktb[0, :, lag * rows:(lag + 1) * rows]
            if t_out <= t_in:
                lag = t_in - t_out
                back = ktb[1, :, lag * rows:(lag + 1) * rows]
                piece = back if t_out < t_in else piece + back
            w1_ref[0, t_in * rows:(t_in + 1) * rows, t_out * rows:(t_out + 1) * rows] = (
                jnp.where(same, piece, 0.0).astype(BF16))


def _s5_weights(lam_re, lam_im, log_step, b_re, b_im, c_re, c_im):
    nd, g, p = lam_re.shape
    s = b_re.shape[-1]
    t_blk = S5_BLOCK
    rows = nd * g * s
    rep = lambda t: jnp.repeat(t.reshape(nd * g, p), s, axis=0)
    ls = jnp.broadcast_to(log_step.reshape(nd * g, 1), (nd * g, p))
    tb = lambda t: jnp.transpose(t, (0, 1, 3, 2)).reshape(rows, p)
    shp = jax.ShapeDtypeStruct((rows, p), F32)
    are, aim, bbre, bbim = pl.pallas_call(
        _s5_disc_kernel, out_shape=[shp, shp, shp, shp], name="s5_discretise",
    )(rep(lam_re), rep(lam_im), rep(ls), tb(b_re), tb(b_im))
    gs = LANES // s
    nslab = g // gs
    tl = t_blk * LANES
    hw = 2 * gs * p
    mat = pl.BlockSpec((nd, 1, LANES, p), lambda i: (0, i, 0, 0))
    slab = lambda t: t.reshape(nd, nslab, LANES, p)
    w1, w2, apw = pl.pallas_call(
        _s5_taps_kernel,
        grid=(nslab,),
        in_specs=[mat] * 6,
        out_specs=[pl.BlockSpec((1, tl, tl + nd * hw), lambda i: (i, 0, 0)),
                   pl.BlockSpec((1, nd * hw, tl), lambda i: (i, 0, 0)),
                   pl.BlockSpec((1, nd, 27, LANES, p), lambda i: (i, 0, 0, 0, 0))],
        out_shape=[jax.ShapeDtypeStruct((nslab, tl, tl + nd * hw), BF16),
                   jax.ShapeDtypeStruct((nslab, nd * hw, tl), BF16),
                   jax.ShapeDtypeStruct((nslab, nd, 27, LANES, p), F32)],
        scratch_shapes=[pltpu.VMEM((tl, p), F32), pltpu.VMEM((tl, p), F32), pltpu.VMEM((nd, LANES, tl), F32)],
        compiler_params=_params("parallel"),
        name="s5_block_taps",
    )(slab(are), slab(aim), slab(bbre), slab(bbim), slab(c_re), slab(c_im))
    ap = apw.reshape(nslab, nd, 9, 3, gs, s, p)[:, :, :, :, :, 0, :].reshape(nslab, nd, 9, 3, gs * p)
    p1 = jnp.concatenate([ap[:, :, :, 0], ap[:, :, :, 0]], axis=-1)
    p2 = jnp.concatenate([ap[:, :, :, 2], ap[:, :, :, 1]], axis=-1)
    return w1, w2, jnp.stack([p1, p2], axis=3).reshape(nslab, nd, 18, hw)


def _s5_kernel(ul_ref, uc_ref, w1_ref, w2_ref, apw_ref, dsk_ref, yl_ref, yc_ref,
               uf, ubuf, zbuf, hbuf, sbuf):
    t_blk = S5_BLOCK
    n_lat, n_ctx = ul_ref.shape[1], uc_ref.shape[1]
    nb_ctx, nb_lat = n_ctx // t_blk, n_lat // t_blk
    nb = nb_ctx + nb_lat
    ts = t_blk * LANES
    hw = apw_ref.shape[3]
    rows = 512

    for src, off, n in ((uc_ref, 0, n_ctx), (ul_ref, n_ctx, n_lat)):
        for r0 in range(0, n, min(rows, n)):
            r1 = min(r0 + rows, n)
            uf[off + r0:off + r1, :] = src[0, r0:r1, :].astype(F32)
    for t in range(t_blk):
        ubuf[:, t * LANES:(t + 1) * LANES] = uf[pl.ds(t, nb, stride=t_blk), :].astype(BF16)
    zbuf[...] = _dot(ubuf[...], w1_ref[0])

    row = lax.broadcasted_iota(jnp.int32, (SUBLANES, hw), 0)

    def cmul(x, p1, p2):
        return x * p1 + pltpu.roll(x, hw // 2, 1) * p2

    for d in range(2):
        col = ts + d * hw
        pw = lambda k, d=d: (apw_ref[0, d, 2 * k:2 * k + 1, :], apw_ref[0, d, 2 * k + 1:2 * k + 2, :])

        def local(i, carry, d=d, col=col, pw=pw):
            r0 = pl.multiple_of(i * SUBLANES, SUBLANES)
            x = zbuf[pl.ds(r0, SUBLANES), col:col + hw]
            for k in (1, 2, 4):
                keep = (row >= k) if d == 0 else (row <= SUBLANES - 1 - k)
                sh = jnp.where(keep, pltpu.roll(x, k if d == 0 else SUBLANES - k, 0), 0.0)
                p1, p2 = pw(k)
                x = x + cmul(sh, p1, p2)
            sbuf[pl.ds(r0, SUBLANES), :] = x
            return carry
        lax.fori_loop(0, nb // SUBLANES, local, 0)

        pex1 = jnp.zeros((SUBLANES, hw), F32)
        pex2 = jnp.zeros((SUBLANES, hw), F32)
        for r in range(SUBLANES):
            p1, p2 = pw(r if d == 0 else SUBLANES - 1 - r)
            pex1 = jnp.where(row == r, p1, pex1)
            pex2 = jnp.where(row == r, p2, pex2)
        p81, p82 = pw(SUBLANES)
        keep1 = (row >= 1) if d == 0 else (row <= SUBLANES - 2)
        last = SUBLANES - 1 if d == 0 else 0
        carry = jnp.zeros((SUBLANES, hw), F32)
        for seg0, ntile in ((0, nb_ctx // SUBLANES), (nb_ctx // SUBLANES, nb_lat // SUBLANES)):
            def chain(i, carry, seg0=seg0, ntile=ntile, d=d):
                ti = seg0 + (i if d == 0 else ntile - 1 - i)
                r0 = pl.multiple_of(ti * SUBLANES, SUBLANES)
                s_t = sbuf[pl.ds(r0, SUBLANES), :]
                excl = jnp.where(keep1, pltpu.roll(s_t, 1 if d == 0 else SUBLANES - 1, 0), 0.0)
                hbuf[pl.ds(r0, SUBLANES), d * hw:(d + 1) * hw] = excl + cmul(carry, pex1, pex2)
                return jnp.broadcast_to(s_t[last:last + 1], (SUBLANES, hw)) + cmul(carry, p81, p82)
            carry = lax.fori_loop(0, ntile, chain, carry)

    zbuf[:, 0:ts] += _dot(hbuf[...].astype(BF16), w2_ref[0])
    dsk = dsk_ref[...]
    for t in range(t_blk):
        y_t = zbuf[:, t * LANES:(t + 1) * LANES] + dsk * uf[pl.ds(t, nb, stride=t_blk), :]
        yc_ref[0, pl.ds(t, nb_ctx, stride=t_blk), :] = y_t[0:nb_ctx]
        yl_ref[0, pl.ds(t, nb_lat, stride=t_blk), :] = y_t[nb_ctx:nb]


def _s5_scan_call(main_l, main_c, w1, w2, apw, d_skip):
    b, l, _ = main_l.shape
    lc = main_c.shape[1]
    nslab, ts, n1 = w1.shape
    hw = apw.shape[3]
    width = d_skip.shape[0]
    nb = (l + lc) // S5_BLOCK
    return pl.pallas_call(
        _s5_kernel,
        grid=(nslab, b),
        in_specs=[pl.BlockSpec((1, l, LANES), lambda s, bi: (bi, 0, s)),
                  pl.BlockSpec((1, lc, LANES), lambda s, bi: (bi, 0, s)),
                  pl.BlockSpec((1, ts, n1), lambda s, bi: (s, 0, 0)),
                  pl.BlockSpec((1, 2 * hw, ts), lambda s, bi: (s, 0, 0)),
                  pl.BlockSpec((1, 2, 18, hw), lambda s, bi: (s, 0, 0, 0)),
                  pl.BlockSpec((1, LANES), lambda s, bi: (0, s))],
        out_specs=[pl.BlockSpec((1, l, LANES), lambda s, bi: (bi, 0, s)),
                   pl.BlockSpec((1, lc, LANES), lambda s, bi: (bi, 0, s))],
        out_shape=[jax.ShapeDtypeStruct((b, l, width), F32),
                   jax.ShapeDtypeStruct((b, lc, width), F32)],
        scratch_shapes=[pltpu.VMEM((l + lc, LANES), F32),
                        pltpu.VMEM((nb, ts), BF16),
                        pltpu.VMEM((nb, n1), F32),
                        pltpu.VMEM((nb, 2 * hw), F32),
                        pltpu.VMEM((nb, hw), F32)],
        compiler_params=_params("parallel", "parallel"),
        name="s5_scan",
    )(main_l, main_c, w1, w2, apw, d_skip.reshape(1, width))


def _s5_glu_kernel(y_ref, w_ref, b_ref, o_ref):
    g = _gelu(y_ref[0])
    o_ref[0] = (g * jax.nn.sigmoid(_dot(g.astype(BF16), w_ref[...]) + b_ref[...])).astype(BF16)


def _s5_glu_call(y, glu_w, glu_b):
    b, l, w = y.shape
    tm = min(l, 1024)
    return pl.pallas_call(
        _s5_glu_kernel,
        grid=(b, l // tm),
        in_specs=[pl.BlockSpec((1, tm, w), lambda bi, i: (bi, i, 0)),
                  pl.BlockSpec((w, w), lambda bi, i: (0, 0)),
                  pl.BlockSpec((1, w), lambda bi, i: (0, 0))],
        out_specs=pl.BlockSpec((1, tm, w), lambda bi, i: (bi, i, 0)),
        out_shape=jax.ShapeDtypeStruct((b, l, w), BF16),
        compiler_params=_params("parallel", "parallel"),
        name="s5_glu",
    )(y, glu_w, glu_b.reshape(1, w))


def _conv_rows(src_ref, n_rows, xpad, cw, bias, emit, tbs):
    zeros = jnp.zeros((SUBLANES, LANES), F32)
    xpad[0:SUBLANES, :] = zeros
    xpad[SUBLANES + n_rows:2 * SUBLANES + n_rows, :] = zeros
    tbs = min(tbs, n_rows)
    for r0 in range(0, n_rows, tbs):
        xpad[SUBLANES + r0:SUBLANES + r0 + tbs, :] = src_ref[0, r0:r0 + tbs, :].astype(F32)
    for r0 in range(0, n_rows, tbs):
        acc = None
        for j in range(SHORT_CONV):
            off = SUBLANES + r0 + j - SHORT_CONV // 2
            term = cw[j:j + 1, :] * xpad[off:off + tbs, :]
            acc = term if acc is None else acc + term
        if bias is not None:
            acc = acc + bias
        emit(r0, acc)


def _lru_kernel(xl_ref, xc_ref, gl_ref, cw_ref, cb_ref, wa_ref, wx_ref, ba_ref, bx_ref, lam_ref, o_ref,
                xpad, xconv, abuf, bbuf, hsum, *, tb):
    n_lat, n_ctx = xl_ref.shape[1], xc_ref.shape[1]
    cw = cw_ref[...]
    cb = cb_ref[...]

    def put(off):
        def emit(r0, y):
            xconv[off + r0:off + r0 + y.shape[0], :] = y
        return emit

    _conv_rows(xc_ref, n_ctx, xpad, cw, cb, put(0), tb)
    _conv_rows(xl_ref, n_lat, xpad, cw, cb, put(n_ctx), tb)

    row = lax.broadcasted_iota(jnp.int32, (SUBLANES, LANES), 0)

    def tile_body(d, ntiles, t, carry):
        ti = t if d == 0 else ntiles - 1 - t
        r0 = pl.multiple_of(ti * SUBLANES, SUBLANES)
        h = bbuf[pl.ds(r0, SUBLANES), :] + abuf[pl.ds(r0, SUBLANES), :] * carry
        bbuf[pl.ds(r0, SUBLANES), :] = h
        last = SUBLANES - 1 if d == 0 else 0
        return jnp.broadcast_to(h[last:last + 1], (SUBLANES, LANES))

    def local_scans(d, a, b):
        nt = a.shape[0] // SUBLANES
        a3 = a.reshape(nt, SUBLANES, LANES)
        b3 = b.reshape(nt, SUBLANES, LANES)
        row3 = lax.broadcasted_iota(jnp.int32, a3.shape, 1)
        for k in (1, 2, 4):
            sh = k if d == 0 else SUBLANES - k
            keep = (row3 >= k) if d == 0 else (row3 <= SUBLANES - 1 - k)
            sa = jnp.where(keep, pltpu.roll(a3, sh, 1), 1.0)
            sb = jnp.where(keep, pltpu.roll(b3, sh, 1), 0.0)
            b3 = b3 + a3 * sb
            a3 = a3 * sa
        return a3.reshape(a.shape), b3.reshape(b.shape)

    for d in range(2):
        sp = _softplus(-lam_ref[d])
        wa, wx = wa_ref[d, 0], wx_ref[d, 0]
        ba, bx = ba_ref[d], bx_ref[d]

        def run_segment(off, n_rows, is_lat, carry, d=d, sp=sp, wa=wa, wx=wx, ba=ba, bx=bx):
            tbs = min(tb, n_rows)
            nblk = n_rows // tbs

            def blk_body(i, carry):
                bi = i if d == 0 else nblk - 1 - i
                lr0 = pl.multiple_of(bi * tbs, tbs)
                xc = xconv[pl.ds(off + lr0, tbs), :]
                xb = xc.astype(BF16)
                r = _sigmoid(_dot(xb, wa) + ba)
                ig = _sigmoid(_dot(xb, wx) + bx)
                log_a = -LRU_C * r * sp
                gain = jnp.sqrt(jnp.maximum(-_expm1_2x(log_a), 0.0))
                acum, hloc = local_scans(d, jnp.exp(log_a), gain * (ig * xc))
                abuf[0:tbs, :] = acum
                bbuf[0:tbs, :] = hloc
                carry = lax.fori_loop(0, tbs // SUBLANES,
                                      functools.partial(tile_body, d, tbs // SUBLANES), carry)
                if is_lat:
                    if d == 0:
                        hsum[pl.ds(lr0, tbs), :] = bbuf[0:tbs, :]
                    else:
                        y = (hsum[pl.ds(lr0, tbs), :] + bbuf[0:tbs, :]) * _gelu(gl_ref[0, pl.ds(lr0, tbs), :].astype(F32))
                        o_ref[0, pl.ds(lr0, tbs), :] = y.astype(BF16)
                return carry

            return lax.fori_loop(0, nblk, blk_body, carry)

        carry = run_segment(0, n_ctx, False, jnp.zeros((SUBLANES, LANES), F32))
        run_segment(n_ctx, n_lat, True, carry)


def _lru_call(main_l, main_c, conv_w, conv_b, wa, ba, wx, bx, lam):
    b, l, _ = main_l.shape
    lc = main_c.shape[1]
    width = conv_b.shape[0]
    nb = width // LRU_BLOCK
    tb = 256
    vec = pl.BlockSpec((2, 1, LANES), lambda bi, n: (0, 0, n))
    wspec = pl.BlockSpec((2, 1, LRU_BLOCK, LRU_BLOCK), lambda bi, n: (0, n, 0, 0))
    return pl.pallas_call(
        functools.partial(_lru_kernel, tb=tb),
        grid=(b, nb),
        in_specs=[pl.BlockSpec((1, l, LANES), lambda bi, n: (bi, 0, n)),
                  pl.BlockSpec((1, lc, LANES), lambda bi, n: (bi, 0, n)),
                  pl.BlockSpec((1, l, LANES), lambda bi, n, nb=nb: (bi, 0, nb + n)),
                  pl.BlockSpec((SHORT_CONV, LANES), lambda bi, n: (0, n)),
                  pl.BlockSpec((1, LANES), lambda bi, n: (0, n)),
                  wspec, wspec, vec, vec, vec],
        out_specs=pl.BlockSpec((1, l, LANES), lambda bi, n: (bi, 0, n)),
        out_shape=jax.ShapeDtypeStruct((b, l, width), BF16),
        scratch_shapes=[pltpu.VMEM((l + 2 * SUBLANES, LANES), F32),
                        pltpu.VMEM((lc + l, LANES), F32),
                        pltpu.VMEM((tb, LANES), F32),
                        pltpu.VMEM((tb, LANES), F32),
                        pltpu.VMEM((l, LANES), F32)],
        compiler_params=_params("parallel", "parallel"),
        name="rglru",
    )(main_l, main_c, main_l, conv_w.T, conv_b.reshape(1, width), wa, wx,
      ba.reshape(2, 1, width), bx.reshape(2, 1, width), lam.reshape(2, 1, width))


def _dn_gate_kernel(ab_ref, alog_ref, dtb_ref, o_ref, *, heads):
    x = ab_ref[0]
    tm = x.shape[0]
    lane = lax.broadcasted_iota(jnp.int32, x.shape, 1)
    g = -jnp.exp(alog_ref[...]) * _softplus(x + dtb_ref[...])
    ii = lax.broadcasted_iota(jnp.int32, (tm, tm), 0)
    jj = lax.broadcasted_iota(jnp.int32, (tm, tm), 1)
    same = (ii // DN_CHUNK) == (jj // DN_CHUNK)
    lower = jnp.where(same & (jj <= ii), 1.0, 0.0).astype(F32)
    upper = jnp.where(same & (jj >= ii), 1.0, 0.0).astype(F32)
    pre = jnp.dot(lower, g, preferred_element_type=F32, precision=lax.Precision.HIGHEST)
    suf = jnp.dot(upper, g, preferred_element_type=F32, precision=lax.Precision.HIGHEST)
    gc = jnp.where(lane < heads, pre, suf)
    o_ref[0] = jnp.where(lane < 2 * heads, gc, jax.nn.sigmoid(x))


def _dn_gate_call(ab, a_log, dt_bias):
    b, l, _ = ab.shape
    tm = min(l, 256)
    pad = lambda t: jnp.pad(t.reshape(1, -1), ((0, 0), (0, LANES - t.size)))
    return pl.pallas_call(
        functools.partial(_dn_gate_kernel, heads=a_log.shape[1]),
        grid=(b, l // tm),
        in_specs=[pl.BlockSpec((1, tm, LANES), lambda bi, i: (bi, i, 0)),
                  pl.BlockSpec((1, LANES), lambda bi, i: (0, 0)),
                  pl.BlockSpec((1, LANES), lambda bi, i: (0, 0))],
        out_specs=pl.BlockSpec((1, tm, LANES), lambda bi, i: (bi, i, 0)),
        out_shape=jax.ShapeDtypeStruct((b, l, LANES), F32),
        compiler_params=_params("parallel", "parallel"),
        name="deltanet_gates",
    )(ab, pad(a_log), pad(dt_bias))


def _dn_gate_layouts(gates, heads):
    b, l, _ = gates.shape
    t = gates[:, :, :4 * heads].reshape(b, l, 2, 2, heads)
    col = jnp.transpose(t, (0, 4, 1, 2, 3)).reshape(b, heads, l, 4)
    row = jnp.transpose(col.reshape(b, heads, l // DN_CHUNK, DN_CHUNK, 4), (0, 1, 2, 4, 3))
    return col, row


def _dn_kernel(ql_ref, kl_ref, vl_ref, gl_ref, qc_ref, kc_ref, vc_ref, gc_ref,
               cwq_ref, cwk_ref, cwv_ref, coll_ref, rowl_ref, colc_ref, rowc_ref, ng_ref,
               ol_ref, oc_ref,
               xpad, qs, ks, vs, oacc, pbuf, xbuf, atb, abuf, bbuf, qpbuf, egl, *, group):
    n_lat, n_ctx = ql_ref.shape[1], qc_ref.shape[1]
    c = DN_CHUNK
    tbs = 256

    def prep(src_ref, n_rows, off, cw_ref, dst, mode):
        def emit(r0, y):
            y = _silu(y)
            if mode != "v":
                y = y * lax.rsqrt(jnp.sum(y * y, axis=-1, keepdims=True) + EPS)
            if mode == "q":
                y = y * (DN_DK ** -0.5)
            dst[off + r0:off + r0 + y.shape[0], :] = y
        _conv_rows(src_ref, n_rows, xpad, cw_ref[...], None, emit, tbs)

    for src_c, src_l, cw_ref, dst, mode in ((qc_ref, ql_ref, cwq_ref, qs, "q"),
                                            (kc_ref, kl_ref, cwk_ref, ks, "k"),
                                            (vc_ref, vl_ref, cwv_ref, vs, "v")):
        prep(src_c, n_ctx, 0, cw_ref, dst, mode)
        prep(src_l, n_lat, n_ctx, cw_ref, dst, mode)

    ii = lax.broadcasted_iota(jnp.int32, (c, c), 0)
    jj = lax.broadcasted_iota(jnp.int32, (c, c), 1)
    n_apply = int(math.log2(c))
    segments = ((0, n_ctx, colc_ref, rowc_ref), (n_ctx, n_lat, coll_ref, rowl_ref))

    def phase1_group(it, grp, off, col_ref, row_ref):
        def chunk_ids(g):
            ci = it * grp + g
            return ci, pl.multiple_of(ci * c, c), pl.multiple_of(off + ci * c, c), off // c + ci

        def gates(col_ref, lr0, d):
            gcb = col_ref[0, 0, pl.ds(lr0, c), :]
            return gcb[:, d:d + 1], gcb[:, 2 + d:3 + d]

        for g in range(grp):
            ci, lr0, r0, _ = chunk_ids(g)
            q = qs[pl.ds(r0, c), :]
            k = ks[pl.ds(r0, c), :]
            v = vs[pl.ds(r0, c), :]
            kbf = k.astype(BF16)
            kk = _dot_nt(kbf, kbf)
            qk = _dot_nt(q.astype(BF16), kbf)
            rows4 = row_ref[0, 0, ci]
            for d in range(2):
                causal = (ii >= jj) if d == 0 else (ii <= jj)
                strict = (ii > jj) if d == 0 else (ii < jj)
                gc_col, beta = gates(col_ref, lr0, d)
                gc_row = rows4[d:d + 1, :]
                gam = jnp.where(causal, jnp.exp(jnp.where(causal, gc_col - gc_row, 0.0)), 0.0)
                pbuf[0, 2 * g + d] = jnp.where(strict, -(beta * kk) * gam, 0.0).astype(BF16)
                xbuf[2 * g + d] = jnp.concatenate([v * beta, (k * beta) * jnp.exp(gc_col)], axis=1)
                atb[2 * g + d] = (qk * gam).astype(BF16)

        for j in range(1, n_apply):
            for ch in range(2 * grp):
                pb = pbuf[j - 1, ch]
                pbuf[j, ch] = _dot(pb, pb).astype(BF16)
        for j in reversed(range(n_apply)):
            for ch in range(2 * grp):
                xv = xbuf[ch]
                xbuf[ch] = xv + _dot(pbuf[j, ch], xv.astype(BF16))

        for g in range(grp):
            ci, lr0, r0, cg = chunk_ids(g)
            q = qs[pl.ds(r0, c), :]
            k = ks[pl.ds(r0, c), :]
            oloc = None
            for d in range(2):
                gc_col, _ = gates(col_ref, lr0, d)
                g_last = gc_col[c - 1:c, :] if d == 0 else gc_col[0:1, :]
                kdt = (k * jnp.exp(g_last - gc_col)).T.astype(BF16)
                xv = xbuf[2 * g + d]
                ub = xv[:, 0:LANES].astype(BF16)
                wb = xv[:, LANES:2 * LANES].astype(BF16)
                at = atb[2 * g + d]
                abuf[d, cg] = _dot(kdt, wb).astype(BF16)
                bbuf[d, cg] = _dot(kdt, ub).astype(BF16)
                qpbuf[d, pl.ds(r0, c), :] = (q * jnp.exp(gc_col) - _dot(at, wb)).astype(BF16)
                part = _dot(at, ub)
                oloc = part if oloc is None else oloc + part
                egl[d, pl.ds(cg, 1), :] = jnp.broadcast_to(jnp.exp(g_last), (1, LANES))
            oacc[pl.ds(r0, c), :] = oloc

    for off, n_rows, col_ref, row_ref in segments:
        nch = n_rows // c
        grp = min(group, nch)

        def p1_body(it, carry, off=off, col_ref=col_ref, row_ref=row_ref, grp=grp):
            phase1_group(it, grp, off, col_ref, row_ref)
            return carry
        lax.fori_loop(0, nch // grp, p1_body, 0)

    def phase2(d, ci, state, off):
        r0 = pl.multiple_of(off + ci * c, c)
        cg = off // c + ci
        sb = state.astype(BF16)
        oacc[pl.ds(r0, c), :] += _dot(qpbuf[d, pl.ds(r0, c), :], sb)
        return state * egl[d, pl.ds(cg, 1), :] - _dot(abuf[d, cg], sb) + bbuf[d, cg].astype(F32)

    states = (jnp.zeros((DN_DK, LANES), F32), jnp.zeros((DN_DK, LANES), F32))
    for off, n_rows, _, _ in segments:
        nch = n_rows // c

        def p2_body(i, st, off=off, nch=nch):
            return (phase2(0, i, st[0], off), phase2(1, nch - 1 - i, st[1], off))
        states = lax.fori_loop(0, nch, p2_body, states)

    ng = ng_ref[...]
    for off, n_rows, g_ref, o_ref in ((0, n_ctx, gc_ref, oc_ref), (n_ctx, n_lat, gl_ref, ol_ref)):
        t = min(tbs, n_rows)
        for r0 in range(0, n_rows, t):
            o = oacc[off + r0:off + r0 + t, :]
            y = o * lax.rsqrt(jnp.mean(o * o, axis=-1, keepdims=True) + EPS) * ng
            o_ref[0, r0:r0 + t, :] = (y * _silu(g_ref[0, r0:r0 + t, :].astype(F32))).astype(BF16)


def _dn_call(main_l, main_c, col_l, row_l, col_c, row_c, conv_w, norm_g, col0):
    b, l, _ = main_l.shape
    lc = main_c.shape[1]
    heads = col_l.shape[1]
    lt = l + lc
    nch = lt // DN_CHUNK
    cwt = conv_w.T
    group = 8

    def blk(n_rows, which):
        return pl.BlockSpec((1, n_rows, LANES), lambda bi, h, which=which: (bi, 0, col0 + which * heads + h))

    def cw(which):
        return pl.BlockSpec((SHORT_CONV, LANES), lambda bi, h, which=which: (0, which * heads + h))

    in_specs = ([blk(l, w) for w in range(4)] + [blk(lc, w) for w in range(4)] + [cw(0), cw(1), cw(2)] +
                [pl.BlockSpec((1, 1, l, 4), lambda bi, h: (bi, h, 0, 0)),
                 pl.BlockSpec((1, 1, l // DN_CHUNK, 4, DN_CHUNK), lambda bi, h: (bi, h, 0, 0, 0)),
                 pl.BlockSpec((1, 1, lc, 4), lambda bi, h: (bi, h, 0, 0)),
                 pl.BlockSpec((1, 1, lc // DN_CHUNK, 4, DN_CHUNK), lambda bi, h: (bi, h, 0, 0, 0)),
                 pl.BlockSpec((1, LANES), lambda bi, h: (0, 0))])
    return pl.pallas_call(
        functools.partial(_dn_kernel, group=group),
        grid=(b, heads),
        in_specs=in_specs,
        out_specs=[pl.BlockSpec((1, l, LANES), lambda bi, h: (bi, 0, h)),
                   pl.BlockSpec((1, lc, LANES), lambda bi, h: (bi, 0, h))],
        out_shape=[jax.ShapeDtypeStruct((b, l, heads * LANES), BF16),
                   jax.ShapeDtypeStruct((b, lc, heads * LANES), BF16)],
        scratch_shapes=[pltpu.VMEM((l + 2 * SUBLANES, LANES), F32),
                        pltpu.VMEM((lt, LANES), F32),
                        pltpu.VMEM((lt, LANES), F32),
                        pltpu.VMEM((lt, LANES), F32),
                        pltpu.VMEM((lt, LANES), F32),
                        pltpu.VMEM((6, 2 * group, DN_CHUNK, DN_CHUNK), BF16),
                        pltpu.VMEM((2 * group, DN_CHUNK, 2 * LANES), F32),
                        pltpu.VMEM((2 * group, DN_CHUNK, DN_CHUNK), BF16),
                        pltpu.VMEM((2, nch, DN_DK, LANES), BF16),
                        pltpu.VMEM((2, nch, DN_DK, LANES), BF16),
                        pltpu.VMEM((2, lt, LANES), BF16),
                        pltpu.VMEM((2, nch, LANES), F32)],
        compiler_params=_params("parallel", "parallel"),
        name="gated_deltanet",
    )(main_l, main_l, main_l, main_l, main_c, main_c, main_c, main_c, cwt, cwt, cwt,
      col_l, row_l, col_c, row_c, norm_g.reshape(1, LANES))


def _ret_kernel(th_ref, ql_ref, kl_ref, vl_ref, qc_ref, kc_ref, vc_ref, cos_ref, sin_ref, o_ref,
                qs, kts, s_ref, dec_ref, xz_ref):
    n_lat, n_ctx = ql_ref.shape[1], qc_ref.shape[1]
    c = RET_CHUNK
    half = RET_DK // 2
    ncc, nlc = n_ctx // c, n_lat // c
    scale = RET_DK ** -0.5

    icol = lax.broadcasted_iota(jnp.int32, (c, RET_DV), 0)
    ii = lax.broadcasted_iota(jnp.int32, (c, c), 0)
    jj = lax.broadcasted_iota(jnp.int32, (c, c), 1)
    gch = []
    for d in range(2):
        lg = -jnp.exp(th_ref[d, 0])
        lg1 = lg[:, 0:1]
        fidx = (icol if d == 0 else c - 1 - icol).astype(F32)
        rel = (ii - jj) if d == 0 else (jj - ii)
        mask = rel >= 0
        dec_ref[d] = jnp.where(mask, jnp.exp(jnp.where(mask, rel, 0).astype(F32) * lg), 0.0)
        xz_ref[d, 0] = jnp.exp((fidx + 1.0) * lg1)
        xz_ref[d, 1] = jnp.exp((c - 1.0 - fidx) * lg1)
        gch.append(jnp.exp(c * lg1))
    s_ref[...] = jnp.zeros_like(s_ref)

    def prep_ctx(ci, carry):
        r0 = pl.multiple_of(ci * c, c)
        qs[pl.ds(r0, c), :] = qc_ref[0, pl.ds(r0, c), :].astype(BF16)
        k = kc_ref[0, pl.ds(r0, c), :].astype(F32) * scale
        kts[ci] = k.T.astype(BF16)
        return carry

    def prep_lat(ci, carry):
        r0 = pl.multiple_of(ci * c, c)
        cos = cos_ref[pl.ds(r0, c), :]
        sin = sin_ref[pl.ds(r0, c), :]

        def rope(t):
            t1, t2 = t[:, 0:half], t[:, half:2 * half]
            return jnp.concatenate([t1 * cos - t2 * sin, t1 * sin + t2 * cos], axis=-1)

        q = rope(ql_ref[0, pl.ds(r0, c), :].astype(F32))
        k = rope(kl_ref[0, pl.ds(r0, c), :].astype(F32)) * scale
        qs[pl.ds(pl.multiple_of(n_ctx + ci * c, c), c), :] = q.astype(BF16)
        kts[ncc + ci] = k.T.astype(BF16)
        return carry

    lax.fori_loop(0, ncc, prep_ctx, 0)
    lax.fori_loop(0, nlc, prep_lat, 0)

    def one(d, ci, row_off, ch_off, v_ref, write, first):
        r0 = pl.multiple_of(ci * c, c)
        qb = qs[pl.ds(pl.multiple_of(row_off + ci * c, c), c), :]
        kt = kts[ch_off + ci]
        v = v_ref[0, pl.ds(r0, c), :].astype(BF16)
        state = s_ref[d]
        if write:
            inner = _dot(qb, kt) * dec_ref[d]
            o = _dot(inner.astype(BF16), v) + xz_ref[d, 0] * _dot(qb, state.astype(BF16))
            if first:
                o_ref[0, pl.ds(r0, c), :] = o
            else:
                o_ref[0, pl.ds(r0, c), :] += o
        vz = (v.astype(F32) * xz_ref[d, 1]).astype(BF16)
        s_ref[d] = state * gch[d] + _dot(kt, vz)

    def ctx_body(i, carry):
        one(0, i, 0, 0, vc_ref, False, False)
        one(1, ncc - 1 - i, 0, 0, vc_ref, False, False)
        return carry

    lax.fori_loop(0, ncc, ctx_body, 0)
    for lo, hi, first in ((0, nlc // 2, True), (nlc // 2, nlc, False)):
        def lat_body(i, carry, first=first):
            one(0, i, n_ctx, ncc, vl_ref, True, first)
            one(1, nlc - 1 - i, n_ctx, ncc, vl_ref, True, first)
            return carry
        lax.fori_loop(lo, hi, lat_body, 0)


def _ret_call(main_l, main_c, theta, cos, sin, q_col0):
    b, l, _ = main_l.shape
    lc = main_c.shape[1]
    heads = theta.shape[1]
    c = RET_CHUNK
    qb0 = q_col0 // RET_DK
    kb0 = qb0 + heads
    vb0 = (q_col0 + 2 * heads * RET_DK) // RET_DV
    th = jnp.broadcast_to(theta.reshape(2, heads, 1, 1), (2, heads, 1, LANES))
    return pl.pallas_call(
        _ret_kernel,
        grid=(b, heads),
        in_specs=[pl.BlockSpec((2, 1, 1, LANES), lambda bi, h: (0, h, 0, 0)),
                  pl.BlockSpec((1, l, RET_DK), lambda bi, h: (bi, 0, qb0 + h)),
                  pl.BlockSpec((1, l, RET_DK), lambda bi, h: (bi, 0, kb0 + h)),
                  pl.BlockSpec((1, l, RET_DV), lambda bi, h: (bi, 0, vb0 + h)),
                  pl.BlockSpec((1, lc, RET_DK), lambda bi, h: (bi, 0, qb0 + h)),
                  pl.BlockSpec((1, lc, RET_DK), lambda bi, h: (bi, 0, kb0 + h)),
                  pl.BlockSpec((1, lc, RET_DV), lambda bi, h: (bi, 0, vb0 + h)),
                  pl.BlockSpec((l, RET_DK // 2), lambda bi, h: (0, 0)),
                  pl.BlockSpec((l, RET_DK // 2), lambda bi, h: (0, 0))],
        out_specs=pl.BlockSpec((1, l, RET_DV), lambda bi, h: (bi, 0, h)),
        out_shape=jax.ShapeDtypeStruct((b, l, heads * RET_DV), F32),
        scratch_shapes=[pltpu.VMEM((lc + l, RET_DK), BF16),
                        pltpu.VMEM(((lc + l) // c, RET_DK, c), BF16),
                        pltpu.VMEM((2, RET_DK, RET_DV), F32),
                        pltpu.VMEM((2, c, c), F32),
                        pltpu.VMEM((2, 2, c, RET_DV), F32)],
        compiler_params=_params("parallel", "parallel"),
        name="retention",
    )(th, main_l, main_l, main_l, main_c, main_c, main_c, cos, sin)


def _ret_finish_kernel(o_ref, r_ref, y_ref):
    o = o_ref[0]
    y = o * lax.rsqrt(jnp.mean(o * o, axis=-1, keepdims=True) + EPS)
    y_ref[0] = (y * _silu(r_ref[0].astype(F32))).astype(BF16)


def _ret_finish_call(o, main_l, r_col0):
    b, l, w = o.shape
    heads = w // RET_DV
    tm = min(l, 1024)
    rb0 = r_col0 // RET_DV
    return pl.pallas_call(
        _ret_finish_kernel,
        grid=(b, l // tm, heads),
        in_specs=[pl.BlockSpec((1, tm, RET_DV), lambda bi, i, h: (bi, i, h)),
                  pl.BlockSpec((1, tm, RET_DV), lambda bi, i, h: (bi, i, rb0 + h))],
        out_specs=pl.BlockSpec((1, tm, RET_DV), lambda bi, i, h: (bi, i, h)),
        out_shape=jax.ShapeDtypeStruct((b, l, w), BF16),
        compiler_params=_params("parallel", "parallel", "parallel"),
        name="retention_finish",
    )(o, main_l)


def _rope_tables(n_tokens):
    rows = n_tokens // GRID_W
    r, col = jnp.meshgrid(jnp.arange(rows), jnp.arange(GRID_W), indexing='ij')
    n_freq = RET_DK // 4
    inv = ROPE_BASE ** (-jnp.arange(n_freq, dtype=F32) / n_freq)
    ang = jnp.concatenate([r.reshape(-1, 1) * inv, col.reshape(-1, 1) * inv], axis=-1)
    return jnp.cos(ang), jnp.sin(ang)


def _even_layer(xl, xc, mods_l, mods_c, norm1_g, norm2_g, w_in, w_out, s5p, glu_w, glu_b,
                dn_conv_w, dn_a_log, dn_dt_bias, dn_norm_g, ffn, final_g, final_norm):
    s5_width = glu_w.shape[0]
    dn_width = dn_conv_w.shape[0] // 3
    heads = dn_width // LANES
    n_main = s5_width + 4 * dn_width
    w_main = w_in[:, :n_main].astype(BF16)
    w_small = jnp.pad(w_in[:, n_main:], ((0, 0), (0, LANES - (w_in.shape[1] - n_main)))).astype(BF16)

    main_l, ab_l = _inproj_call(xl, norm1_g, mods_l[0], mods_l[1], w_main, w_small)
    main_c, ab_c = _inproj_call(xc, norm1_g, mods_c[0], mods_c[1], w_main, w_small)

    y_l, y_c = _s5_scan_call(main_l, main_c, *s5p)
    glu_wb = glu_w.astype(BF16)
    s5_l = _s5_glu_call(y_l, glu_wb, glu_b)
    s5_c = _s5_glu_call(y_c, glu_wb, glu_b)

    col_l, row_l = _dn_gate_layouts(_dn_gate_call(ab_l, dn_a_log, dn_dt_bias), heads)
    col_c, row_c = _dn_gate_layouts(_dn_gate_call(ab_c, dn_a_log, dn_dt_bias), heads)
    dn_l, dn_c = _dn_call(main_l, main_c, col_l, row_l, col_c, row_c, dn_conv_w, dn_norm_g,
                          s5_width // LANES)

    w_o1 = w_out[:s5_width].astype(BF16)
    w_o2 = w_out[s5_width:].astype(BF16)
    xl = _outproj_call(xl, mods_l[2], s5_l, dn_l, w_o1, w_o2)
    xc = _outproj_call(xc, mods_c[2], s5_c, dn_c, w_o1, w_o2)
    w1, w3, w2 = ffn
    xl = _ffn_call(xl, norm2_g, mods_l[3], mods_l[4], mods_l[5], w1, w3, w2, final_g, final_norm)
    xc = _ffn_call(xc, norm2_g, mods_c[3], mods_c[4], mods_c[5], w1, w3, w2, final_g, False)
    return xl, xc


def _odd_layer(xl, xc, mods_l, mods_c, norm1_g, norm2_g, w_in, w_out, conv_w, conv_b, wa, ba, wx, bx, lam,
               theta, cos, sin, ffn, final_g, final_norm):
    lru_width = conv_b.shape[0]
    heads = theta.shape[1]
    w_inb = w_in.astype(BF16)
    main_l = _inproj_call(xl, norm1_g, mods_l[0], mods_l[1], w_inb)
    main_c = _inproj_call(xc, norm1_g, mods_c[0], mods_c[1], w_inb)

    lru_l = _lru_call(main_l, main_c, conv_w, conv_b, wa.astype(BF16), ba, wx.astype(BF16), bx, lam)
    q_col0 = 2 * lru_width
    o = _ret_call(main_l, main_c, theta, cos, sin, q_col0)
    ret_l = _ret_finish_call(o, main_l, q_col0 + 2 * heads * RET_DK + heads * RET_DV)

    w_o1 = w_out[:lru_width].astype(BF16)
    w_o2 = w_out[lru_width:].astype(BF16)
    xl = _outproj_call(xl, mods_l[2], lru_l, ret_l, w_o1, w_o2)
    w1, w3, w2 = ffn
    return _ffn_call(xl, norm2_g, mods_l[3], mods_l[4], mods_l[5], w1, w3, w2, final_g, final_norm)


def kernel(x, c, ctx, c_ctx, mod_w, mod_b, norm1_g, norm2_g, ffn_w1, ffn_w3, ffn_w2, final_g, even_w_in, even_w_out, s5_lam_re, s5_lam_im, s5_log_step, s5_b_re, s5_b_im, s5_c_re, s5_c_im, s5_d, s5_glu_w, s5_glu_b, dn_conv_w, dn_a_log, dn_dt_bias, dn_norm_g, odd_w_in, odd_w_out, lru_conv_w, lru_conv_b, lru_wa, lru_ba, lru_wx, lru_bx, lru_lam, ret_theta):
    bsz, n_tok, d = x.shape
    depth = mod_w.shape[0]
    assert depth == 2 and bsz + 1 <= SUBLANES
    cos, sin = _rope_tables(n_tok)

    rows = jnp.concatenate([c, c_ctx[None, :], jnp.zeros((SUBLANES - bsz - 1, d), F32)], axis=0)
    mods = _mod_call(rows, mod_w, mod_b)

    def split_mods(i):
        m = mods[i].reshape(SUBLANES, 6, d)
        ml = [m[:bsz, k][:, None, :] for k in range(6)]
        mc = [jnp.broadcast_to(m[bsz, k][None, None, :], (bsz, 1, d)) for k in range(6)]
        return ml, mc

    xl, xc = x, ctx
    ml, mc = split_mods(0)
    s5p = _s5_weights(s5_lam_re[0], s5_lam_im[0], s5_log_step[0], s5_b_re[0], s5_b_im[0],
                      s5_c_re[0], s5_c_im[0]) + (s5_d[0],)
    ffn0 = (ffn_w1[0].astype(BF16), ffn_w3[0].astype(BF16), ffn_w2[0].astype(BF16))
    xl, xc = _even_layer(xl, xc, ml, mc, norm1_g[0], norm2_g[0], even_w_in[0], even_w_out[0],
                         s5p, s5_glu_w[0], s5_glu_b[0],
                         dn_conv_w[0], dn_a_log[0], dn_dt_bias[0], dn_norm_g[0], ffn0, final_g, False)
    ml, mc = split_mods(1)
    ffn1 = (ffn_w1[1].astype(BF16), ffn_w3[1].astype(BF16), ffn_w2[1].astype(BF16))
    return _odd_layer(xl, xc, ml, mc, norm1_g[1], norm2_g[1], odd_w_in[0], odd_w_out[0],
                      lru_conv_w[0], lru_conv_b[0], lru_wa[0], lru_ba[0], lru_wx[0], lru_bx[0], lru_lam[0],
                      ret_theta[0], cos, sin, ffn1, final_g, True)
```

```python
import functools
import math

import jax
import jax.numpy as jnp
from jax import lax
from jax.experimental import pallas as pl
from jax.experimental.pallas import tpu as pltpu

F32 = jnp.float32
BF16 = jnp.bfloat16

EPS = 1e-6
GRID_W = 64
ROPE_BASE = 10000.0

S5_GROUP = 16
S5_STATE = 64
S5_BLOCK = 8

DN_DK = 128
DN_CHUNK = 64
SHORT_CONV = 4

LRU_BLOCK = 128
LRU_C = 8.0

RET_DK = 256
RET_DV = 512
RET_CHUNK = 128

LANES = 128
SUBLANES = 8
VMEM_LIMIT = 56 * 1024 * 1024


def _params(*sem):
    return pltpu.CompilerParams(dimension_semantics=sem, vmem_limit_bytes=VMEM_LIMIT)


def _silu(x):
    return x * jax.nn.sigmoid(x)


def _gelu(x):
    return 0.5 * x * (1.0 + jnp.tanh(0.7978845608028654 * (x + 0.044715 * (x * x * x))))


def _softplus(x):
    return jnp.maximum(x, 0.0) + jnp.log1p(jnp.exp(-jnp.abs(x)))


def _sigmoid(x):
    return 0.5 * (1.0 + jnp.tanh(0.5 * x))


def _expm1_2x(x):
    t = jnp.tanh(x)
    return 2.0 * t / (1.0 - t)


def _adaln(x, g, shift, scale):
    y = x * lax.rsqrt(jnp.mean(x * x, axis=-1, keepdims=True) + EPS)
    return (y * g) * (1.0 + scale) + shift


def _dot(a, b):
    return jnp.dot(a, b, preferred_element_type=F32)


def _dot_nt(a, b):
    return lax.dot_general(a, b, (((1,), (1,)), ((), ())), preferred_element_type=F32)


def _mod_kernel(s_ref, w_ref, b_ref, o_ref):
    s = _silu(s_ref[...])
    o_ref[0] = _dot(s.astype(BF16), w_ref[0].astype(BF16)) + b_ref[0]


def _mod_call(rows, mod_w, mod_b):
    depth, d, n = mod_w.shape
    tn = 1024
    return pl.pallas_call(
        _mod_kernel,
        grid=(depth, n // tn),
        in_specs=[pl.BlockSpec((SUBLANES, d), lambda i, j: (0, 0)),
                  pl.BlockSpec((1, d, tn), lambda i, j: (i, 0, j)),
                  pl.BlockSpec((1, 1, tn), lambda i, j: (i, 0, j))],
        out_specs=pl.BlockSpec((1, SUBLANES, tn), lambda i, j: (i, 0, j)),
        out_shape=jax.ShapeDtypeStruct((depth, SUBLANES, n), F32),
        compiler_params=_params("parallel", "parallel"),
        name="mod_proj",
    )(rows, mod_w, mod_b.reshape(depth, 1, n))


def _next_tile(nb, ni):
    def nxt(bi, i):
        n = jnp.minimum(bi * ni + i + 1, nb * ni - 1)
        return n // ni, n % ni
    return nxt


def _inproj_kernel(x_ref, xn_ref, g_ref, sh_ref, sc_ref, shn_ref, scn_ref, w_ref, *rest, has_small):
    if has_small:
        ws_ref, o_ref, os_ref, h_ref = rest
    else:
        o_ref, h_ref = rest
    tile = pl.program_id(0) * pl.num_programs(1) + pl.program_id(1)
    j = pl.program_id(2)
    slot = tile % 2

    @pl.when((tile == 0) & (j == 0))
    def _():
        h_ref[0] = _adaln(x_ref[0], g_ref[...], sh_ref[0], sc_ref[0]).astype(BF16)

    if has_small:
        @pl.when(j == 0)
        def _():
            os_ref[0] = _dot(h_ref[slot], ws_ref[...])

    @pl.when(j < pl.num_programs(2) - 1)
    def _():
        o_ref[0] = _dot(h_ref[slot], w_ref[...]).astype(o_ref.dtype)

    @pl.when(j == pl.num_programs(2) - 1)
    def _():
        o_ref[0] = _dot(h_ref[slot], w_ref[...]).astype(o_ref.dtype)
        h_ref[1 - slot] = _adaln(xn_ref[0], g_ref[...], shn_ref[0], scn_ref[0]).astype(BF16)


def _inproj_call(x, g, shift, scale, w, w_small=None):
    b, l, d = x.shape
    n = w.shape[1]
    tm = min(l, 1024)
    tn = 512
    has_small = w_small is not None
    nxt = _next_tile(b, l // tm)
    vec_next = pl.BlockSpec((1, 1, d), lambda bi, i, j: (nxt(bi, i)[0], 0, 0))
    vec = pl.BlockSpec((1, 1, d), lambda bi, i, j: (bi, 0, 0))
    in_specs = [pl.BlockSpec((1, tm, d), lambda bi, i, j: (bi, i, 0)),
                pl.BlockSpec((1, tm, d), lambda bi, i, j: (*nxt(bi, i), 0)),
                pl.BlockSpec((1, d), lambda bi, i, j: (0, 0)),
                vec, vec, vec_next, vec_next,
                pl.BlockSpec((d, tn), lambda bi, i, j: (0, j))]
    out_specs = [pl.BlockSpec((1, tm, tn), lambda bi, i, j: (bi, i, j))]
    out_shape = [jax.ShapeDtypeStruct((b, l, n), BF16)]
    args = [x, x, g.reshape(1, d), shift, scale, shift, scale, w]
    if has_small:
        in_specs.append(pl.BlockSpec((d, LANES), lambda bi, i, j: (0, 0)))
        out_specs.append(pl.BlockSpec((1, tm, LANES), lambda bi, i, j: (bi, i, 0)))
        out_shape.append(jax.ShapeDtypeStruct((b, l, LANES), F32))
        args.append(w_small)
    outs = pl.pallas_call(
        functools.partial(_inproj_kernel, has_small=has_small),
        grid=(b, l // tm, n // tn),
        in_specs=in_specs, out_specs=out_specs, out_shape=out_shape,
        scratch_shapes=[pltpu.VMEM((2, tm, d), BF16)],
        compiler_params=_params("arbitrary", "arbitrary", "arbitrary"),
        name="adaln_inproj",
    )(*args)
    return outs if has_small else outs[0]


def _outproj_kernel(x_ref, gate_ref, a1_ref, a2_ref, w1_ref, w2_ref, o_ref):
    y = _dot(a1_ref[0], w1_ref[...]) + _dot(a2_ref[0], w2_ref[...])
    o_ref[0] = x_ref[0] + gate_ref[0] * y


def _outproj_call(x, gate, a1, a2, w1, w2):
    b, l, d = x.shape
    k1, k2 = a1.shape[2], a2.shape[2]
    tm = min(l, 1024)
    tn = 512
    return pl.pallas_call(
        _outproj_kernel,
        grid=(b, l // tm, d // tn),
        in_specs=[pl.BlockSpec((1, tm, tn), lambda bi, i, j: (bi, i, j)),
                  pl.BlockSpec((1, 1, tn), lambda bi, i, j: (bi, 0, j)),
                  pl.BlockSpec((1, tm, k1), lambda bi, i, j: (bi, i, 0)),
                  pl.BlockSpec((1, tm, k2), lambda bi, i, j: (bi, i, 0)),
                  pl.BlockSpec((k1, tn), lambda bi, i, j: (0, j)),
                  pl.BlockSpec((k2, tn), lambda bi, i, j: (0, j))],
        out_specs=pl.BlockSpec((1, tm, tn), lambda bi, i, j: (bi, i, j)),
        out_shape=jax.ShapeDtypeStruct((b, l, d), F32),
        compiler_params=_params("parallel", "parallel", "arbitrary"),
        name="outproj_residual",
    )(x, gate, a1, a2, w1, w2)


def _ffn_kernel(x_ref, xn_ref, g_ref, sh_ref, sc_ref, shn_ref, scn_ref, gate_ref, w1_ref, w3_ref, w2_ref, fg_ref,
                o_ref, h_ref, acc_ref, *, final_norm):
    tile = pl.program_id(0) * pl.num_programs(1) + pl.program_id(1)
    f = pl.program_id(2)
    slot = tile % 2

    @pl.when((tile == 0) & (f == 0))
    def _():
        h_ref[0] = _adaln(x_ref[0], g_ref[...], sh_ref[0], sc_ref[0]).astype(BF16)

    def step():
        h = h_ref[slot]
        a = _dot(h, w1_ref[...])
        b = _dot(h, w3_ref[...])
        return _dot((_silu(a) * b).astype(BF16), w2_ref[...])

    @pl.when(f == 0)
    def _():
        acc_ref[...] = step()

    @pl.when((f > 0) & (f < pl.num_programs(2) - 1))
    def _():
        acc_ref[...] += step()

    @pl.when(f == pl.num_programs(2) - 1)
    def _():
        y = x_ref[0] + gate_ref[0] * (acc_ref[...] + step())
        if final_norm:
            y = y * lax.rsqrt(jnp.mean(y * y, axis=-1, keepdims=True) + EPS) * fg_ref[...]
        o_ref[0] = y
        h_ref[1 - slot] = _adaln(xn_ref[0], g_ref[...], shn_ref[0], scn_ref[0]).astype(BF16)


def _ffn_call(x, g, shift, scale, gate, w1, w3, w2, final_g, final_norm):
    b, l, d = x.shape
    dff = w1.shape[1]
    tm = min(l, 512)
    tf = 512
    assert dff // tf >= 2
    nxt = _next_tile(b, l // tm)
    vec = pl.BlockSpec((1, 1, d), lambda bi, i, f: (bi, 0, 0))
    vec_next = pl.BlockSpec((1, 1, d), lambda bi, i, f: (nxt(bi, i)[0], 0, 0))
    return pl.pallas_call(
        functools.partial(_ffn_kernel, final_norm=final_norm),
        grid=(b, l // tm, dff // tf),
        in_specs=[pl.BlockSpec((1, tm, d), lambda bi, i, f: (bi, i, 0)),
                  pl.BlockSpec((1, tm, d), lambda bi, i, f: (*nxt(bi, i), 0)),
                  pl.BlockSpec((1, d), lambda bi, i, f: (0, 0)),
                  vec, vec, vec_next, vec_next, vec,
                  pl.BlockSpec((d, tf), lambda bi, i, f: (0, f)),
                  pl.BlockSpec((d, tf), lambda bi, i, f: (0, f)),
                  pl.BlockSpec((tf, d), lambda bi, i, f: (f, 0)),
                  pl.BlockSpec((1, d), lambda bi, i, f: (0, 0))],
        out_specs=pl.BlockSpec((1, tm, d), lambda bi, i, f: (bi, i, 0)),
        out_shape=jax.ShapeDtypeStruct((b, l, d), F32),
        scratch_shapes=[pltpu.VMEM((2, tm, d), BF16), pltpu.VMEM((tm, d), F32)],
        compiler_params=_params("arbitrary", "arbitrary", "arbitrary"),
        name="ffn_swiglu",
    )(x, x, g.reshape(1, d), shift, scale, shift, scale, gate, w1, w3, w2, final_g.reshape(1, d))


def _s5_disc_kernel(lre_ref, lim_ref, ls_ref, bre_ref, bim_ref, are_ref, aim_ref, bbre_ref, bbim_ref):
    lre = jnp.minimum(lre_ref[...], -1e-4)
    lim = lim_ref[...]
    dt = jnp.exp(ls_ref[...])
    mag = jnp.exp(lre * dt)
    ar = mag * jnp.cos(lim * dt)
    ai = mag * jnp.sin(lim * dt)
    nr, ni = ar - 1.0, ai
    den = lre * lre + lim * lim
    cr = (nr * lre + ni * lim) / den
    ci = (ni * lre - nr * lim) / den
    bre, bim = bre_ref[...], bim_ref[...]
    are_ref[...] = ar
    aim_ref[...] = ai
    bbre_ref[...] = cr * bre - ci * bim
    bbim_ref[...] = cr * bim + ci * bre


def _s5_taps_kernel(ar_ref, ai_ref, br_ref, bi_ref, cr_ref, ci_ref, w1_ref, w2_ref, apw_ref, ccr, cci, ktb):
    t_blk = S5_BLOCK
    rows, p = ar_ref.shape[2], ar_ref.shape[3]
    s = S5_GROUP
    gs = rows // s
    tl = t_blk * rows
    hi = lax.Precision.HIGHEST
    nt = (((1,), (1,)), ((), ()))
    iota = lambda shape, ax: lax.broadcasted_iota(jnp.int32, shape, ax)
    same = (iota((rows, rows), 0) // s) == (iota((rows, rows), 1) // s)
    dmask = (iota((rows, gs * p), 0) // s) == (iota((rows, gs * p), 1) // p)
    gmask = (iota((gs * p, rows), 0) // p) == (iota((gs * p, rows), 1) // s)

    for d in range(2):
        ar, ai = ar_ref[d, 0], ai_ref[d, 0]
        br, bi = br_ref[d, 0], bi_ref[d, 0]
        cr, ci = cr_ref[d, 0], ci_ref[d, 0]
        pw = [(jnp.ones_like(ar), jnp.zeros_like(ar))]
        for _ in range(t_blk):
            pr, pi = pw[-1]
            pw.append((pr * ar - pi * ai, pr * ai + pi * ar))
        for t in range(t_blk):
            sl = slice(t * rows, (t + 1) * rows)
            pr, pi = pw[t]
            ccr[sl, :] = cr * pr - ci * pi
            cci[sl, :] = cr * pi + ci * pr
            dpr, dpi = pw[t_blk - 1 - t] if d == 0 else pw[t]
            for c, piece in enumerate((br * dpr - bi * dpi, br * dpi + bi * dpr)):
                wide = jnp.concatenate([piece] * gs, axis=1)
                c0 = tl + (2 * d + c) * gs * p
                w1_ref[0, sl, c0:c0 + gs * p] = jnp.where(dmask, wide, 0.0).astype(BF16)
            gpr, gpi = pw[t + 1] if d == 0 else pw[t_blk - t]
            for c, piece in enumerate((cr * gpr - ci * gpi, -(cr * gpi + ci * gpr))):
                tall = jnp.concatenate([piece.T] * gs, axis=0)
                r0 = (2 * d + c) * gs * p
                w2_ref[0, r0:r0 + gs * p, sl] = jnp.where(gmask, tall, 0.0).astype(BF16)
        ktb[d] = (lax.dot_general(br, ccr[...], nt, precision=hi, preferred_element_type=F32)
                  - lax.dot_general(bi, cci[...], nt, precision=hi, preferred_element_type=F32))
        qr, qi = pw[t_blk]
        er, ei = jnp.ones_like(qr), jnp.zeros_like(qr)
        for kk in range(9):
            apw_ref[0, d, 3 * kk] = er
            apw_ref[0, d, 3 * kk + 1] = ei
            apw_ref[0, d, 3 * kk + 2] = -ei
            er, ei = er * qr - ei * qi, er * qi + ei * qr

    for t_in in range(t_blk):
        for t_out in range(t_blk):
            if t_out >= t_in:
                lag = t_out - t_in
                piece = ktb[0, :, lag * rows:(lag + 1) * rows]
            if t_out <= t_in:
                lag = t_in - t_out
                back = ktb[1, :, lag * rows:(lag + 1) * rows]
                piece = back if t_out < t_in else piece + back
            w1_ref[0, t_in * rows:(t_in + 1) * rows, t_out * rows:(t_out + 1) * rows] = (
                jnp.where(same, piece, 0.0).astype(BF16))


def _s5_weights(lam_re, lam_im, log_step, b_re, b_im, c_re, c_im):
    nd, g, p = lam_re.shape
    s = b_re.shape[-1]
    t_blk = S5_BLOCK
    rows = nd * g * s
    rep = lambda t: jnp.repeat(t.reshape(nd * g, p), s, axis=0)
    ls = jnp.broadcast_to(log_step.reshape(nd * g, 1), (nd * g, p))
    tb = lambda t: jnp.transpose(t, (0, 1, 3, 2)).reshape(rows, p)
    shp = jax.ShapeDtypeStruct((rows, p), F32)
    are, aim, bbre, bbim = pl.pallas_call(
        _s5_disc_kernel, out_shape=[shp, shp, shp, shp], name="s5_discretise",
    )(rep(lam_re), rep(lam_im), rep(ls), tb(b_re), tb(b_im))
    gs = LANES // s
    nslab = g // gs
    tl = t_blk * LANES
    hw = 2 * gs * p
    mat = pl.BlockSpec((nd, 1, LANES, p), lambda i: (0, i, 0, 0))
    slab = lambda t: t.reshape(nd, nslab, LANES, p)
    w1, w2, apw = pl.pallas_call(
        _s5_taps_kernel,
        grid=(nslab,),
        in_specs=[mat] * 6,
        out_specs=[pl.BlockSpec((1, tl, tl + nd * hw), lambda i: (i, 0, 0)),
                   pl.BlockSpec((1, nd * hw, tl), lambda i: (i, 0, 0)),
                   pl.BlockSpec((1, nd, 27, LANES, p), lambda i: (i, 0, 0, 0, 0))],
        out_shape=[jax.ShapeDtypeStruct((nslab, tl, tl + nd * hw), BF16),
                   jax.ShapeDtypeStruct((nslab, nd * hw, tl), BF16),
                   jax.ShapeDtypeStruct((nslab, nd, 27, LANES, p), F32)],
        scratch_shapes=[pltpu.VMEM((tl, p), F32), pltpu.VMEM((tl, p), F32), pltpu.VMEM((nd, LANES, tl), F32)],
        compiler_params=_params("parallel"),
        name="s5_block_taps",
    )(slab(are), slab(aim), slab(bbre), slab(bbim), slab(c_re), slab(c_im))
    ap = apw.reshape(nslab, nd, 9, 3, gs, s, p)[:, :, :, :, :, 0, :].reshape(nslab, nd, 9, 3, gs * p)
    p1 = jnp.concatenate([ap[:, :, :, 0], ap[:, :, :, 0]], axis=-1)
    p2 = jnp.concatenate([ap[:, :, :, 2], ap[:, :, :, 1]], axis=-1)
    return w1, w2, jnp.stack([p1, p2], axis=3).reshape(nslab, nd, 18, hw)


def _s5_kernel(ul_ref, uc_ref, w1_ref, w2_ref, apw_ref, dsk_ref, yl_ref, yc_ref,
               uf, ubuf, zbuf, hbuf, sbuf):
    t_blk = S5_BLOCK
    n_lat, n_ctx = ul_ref.shape[1], uc_ref.shape[1]
    nb_ctx, nb_lat = n_ctx // t_blk, n_lat // t_blk
    nb = nb_ctx + nb_lat
    ts = t_blk * LANES
    hw = apw_ref.shape[3]
    rows = 512

    for src, off, n in ((uc_ref, 0, n_ctx), (ul_ref, n_ctx, n_lat)):
        for r0 in range(0, n, min(rows, n)):
            r1 = min(r0 + rows, n)
            uf[off + r0:off + r1, :] = src[0, r0:r1, :].astype(F32)
    for t in range(t_blk):
        ubuf[:, t * LANES:(t + 1) * LANES] = uf[pl.ds(t, nb, stride=t_blk), :].astype(BF16)
    zbuf[...] = _dot(ubuf[...], w1_ref[0])

    row = lax.broadcasted_iota(jnp.int32, (SUBLANES, hw), 0)

    def cmul(x, p1, p2):
        return x * p1 + pltpu.roll(x, hw // 2, 1) * p2

    for d in range(2):
        col = ts + d * hw
        pw = lambda k, d=d: (apw_ref[0, d, 2 * k:2 * k + 1, :], apw_ref[0, d, 2 * k + 1:2 * k + 2, :])

        def local(i, carry, d=d, col=col, pw=pw):
            r0 = pl.multiple_of(i * SUBLANES, SUBLANES)
            x = zbuf[pl.ds(r0, SUBLANES), col:col + hw]
            for k in (1, 2, 4):
                keep = (row >= k) if d == 0 else (row <= SUBLANES - 1 - k)
                sh = jnp.where(keep, pltpu.roll(x, k if d == 0 else SUBLANES - k, 0), 0.0)
                p1, p2 = pw(k)
                x = x + cmul(sh, p1, p2)
            sbuf[pl.ds(r0, SUBLANES), :] = x
            return carry
        lax.fori_loop(0, nb // SUBLANES, local, 0)

        pex1 = jnp.zeros((SUBLANES, hw), F32)
        pex2 = jnp.zeros((SUBLANES, hw), F32)
        for r in range(SUBLANES):
            p1, p2 = pw(r if d == 0 else SUBLANES - 1 - r)
            pex1 = jnp.where(row == r, p1, pex1)
            pex2 = jnp.where(row == r, p2, pex2)
        p81, p82 = pw(SUBLANES)
        keep1 = (row >= 1) if d == 0 else (row <= SUBLANES - 2)
        last = SUBLANES - 1 if d == 0 else 0
        carry = jnp.zeros((SUBLANES, hw), F32)
        for seg0, ntile in ((0, nb_ctx // SUBLANES), (nb_ctx // SUBLANES, nb_lat // SUBLANES)):
            def chain(i, carry, seg0=seg0, ntile=ntile, d=d):
                ti = seg0 + (i if d == 0 else ntile - 1 - i)
                r0 = pl.multiple_of(ti * SUBLANES, SUBLANES)
                s_t = sbuf[pl.ds(r0, SUBLANES), :]
                excl = jnp.where(keep1, pltpu.roll(s_t, 1 if d == 0 else SUBLANES - 1, 0), 0.0)
                hbuf[pl.ds(r0, SUBLANES), d * hw:(d + 1) * hw] = excl + cmul(carry, pex1, pex2)
                return jnp.broadcast_to(s_t[last:last + 1], (SUBLANES, hw)) + cmul(carry, p81, p82)
            carry = lax.fori_loop(0, ntile, chain, carry)

    zbuf[:, 0:ts] += _dot(hbuf[...].astype(BF16), w2_ref[0])
    dsk = dsk_ref[...]
    for t in range(t_blk):
        y_t = zbuf[:, t * LANES:(t + 1) * LANES] + dsk * uf[pl.ds(t, nb, stride=t_blk), :]
        yc_ref[0, pl.ds(t, nb_ctx, stride=t_blk), :] = y_t[0:nb_ctx]
        yl_ref[0, pl.ds(t, nb_lat, stride=t_blk), :] = y_t[nb_ctx:nb]


def _s5_scan_call(main_l, main_c, w1, w2, apw, d_skip):
    b, l, _ = main_l.shape
    lc = main_c.shape[1]
    nslab, ts, n1 = w1.shape
    hw = apw.shape[3]
    width = d_skip.shape[0]
    nb = (l + lc) // S5_BLOCK
    return pl.pallas_call(
        _s5_kernel,
        grid=(nslab, b),
        in_specs=[pl.BlockSpec((1, l, LANES), lambda s, bi: (bi, 0, s)),
                  pl.BlockSpec((1, lc, LANES), lambda s, bi: (bi, 0, s)),
                  pl.BlockSpec((1, ts, n1), lambda s, bi: (s, 0, 0)),
                  pl.BlockSpec((1, 2 * hw, ts), lambda s, bi: (s, 0, 0)),
                  pl.BlockSpec((1, 2, 18, hw), lambda s, bi: (s, 0, 0, 0)),
                  pl.BlockSpec((1, LANES), lambda s, bi: (0, s))],
        out_specs=[pl.BlockSpec((1, l, LANES), lambda s, bi: (bi, 0, s)),
                   pl.BlockSpec((1, lc, LANES), lambda s, bi: (bi, 0, s))],
        out_shape=[jax.ShapeDtypeStruct((b, l, width), F32),
                   jax.ShapeDtypeStruct((b, lc, width), F32)],
        scratch_shapes=[pltpu.VMEM((l + lc, LANES), F32),
                        pltpu.VMEM((nb, ts), BF16),
                        pltpu.VMEM((nb, n1), F32),
                        pltpu.VMEM((nb, 2 * hw), F32),
                        pltpu.VMEM((nb, hw), F32)],
        compiler_params=_params("parallel", "parallel"),
        name="s5_scan",
    )(main_l, main_c, w1, w2, apw, d_skip.reshape(1, width))


def _s5_glu_kernel(y_ref, w_ref, b_ref, o_ref):
    g = _gelu(y_ref[0])
    o_ref[0] = (g * jax.nn.sigmoid(_dot(g.astype(BF16), w_ref[...]) + b_ref[...])).astype(BF16)


def _s5_glu_call(y, glu_w, glu_b):
    b, l, w = y.shape
    tm = min(l, 1024)
    return pl.pallas_call(
        _s5_glu_kernel,
        grid=(b, l // tm),
        in_specs=[pl.BlockSpec((1, tm, w), lambda bi, i: (bi, i, 0)),
                  pl.BlockSpec((w, w), lambda bi, i: (0, 0)),
                  pl.BlockSpec((1, w), lambda bi, i: (0, 0))],
        out_specs=pl.BlockSpec((1, tm, w), lambda bi, i: (bi, i, 0)),
        out_shape=jax.ShapeDtypeStruct((b, l, w), BF16),
        compiler_params=_params("parallel", "parallel"),
        name="s5_glu",
    )(y, glu_w, glu_b.reshape(1, w))


def _conv_rows(src_ref, n_rows, xpad, cw, bias, emit, tbs):
    zeros = jnp.zeros((SUBLANES, LANES), F32)
    xpad[0:SUBLANES, :] = zeros
    xpad[SUBLANES + n_rows:2 * SUBLANES + n_rows, :] = zeros
    tbs = min(tbs, n_rows)
    for r0 in range(0, n_rows, tbs):
        xpad[SUBLANES + r0:SUBLANES + r0 + tbs, :] = src_ref[0, r0:r0 + tbs, :].astype(F32)
    for r0 in range(0, n_rows, tbs):
        acc = None
        for j in range(SHORT_CONV):
            off = SUBLANES + r0 + j - SHORT_CONV // 2
            term = cw[j:j + 1, :] * xpad[off:off + tbs, :]
            acc = term if acc is None else acc + term
        if bias is not None:
            acc = acc + bias
        emit(r0, acc)


def _lru_kernel(xl_ref, xc_ref, gl_ref, cw_ref, cb_ref, wa_ref, wx_ref, ba_ref, bx_ref, lam_ref, o_ref,
                xpad, xconv, abuf, bbuf, hsum, *, tb):
    n_lat, n_ctx = xl_ref.shape[1], xc_ref.shape[1]
    cw = cw_ref[...]
    cb = cb_ref[...]

    def put(off):
        def emit(r0, y):
            xconv[off + r0:off + r0 + y.shape[0], :] = y
        return emit

    _conv_rows(xc_ref, n_ctx, xpad, cw, cb, put(0), tb)
    _conv_rows(xl_ref, n_lat, xpad, cw, cb, put(n_ctx), tb)

    row = lax.broadcasted_iota(jnp.int32, (SUBLANES, LANES), 0)

    def tile_body(d, ntiles, t, carry):
        ti = t if d == 0 else ntiles - 1 - t
        r0 = pl.multiple_of(ti * SUBLANES, SUBLANES)
        h = bbuf[pl.ds(r0, SUBLANES), :] + abuf[pl.ds(r0, SUBLANES), :] * carry
        bbuf[pl.ds(r0, SUBLANES), :] = h
        last = SUBLANES - 1 if d == 0 else 0
        return jnp.broadcast_to(h[last:last + 1], (SUBLANES, LANES))

    def local_scans(d, a, b):
        nt = a.shape[0] // SUBLANES
        a3 = a.reshape(nt, SUBLANES, LANES)
        b3 = b.reshape(nt, SUBLANES, LANES)
        row3 = lax.broadcasted_iota(jnp.int32, a3.shape, 1)
        for k in (1, 2, 4):
            sh = k if d == 0 else SUBLANES - k
            keep = (row3 >= k) if d == 0 else (row3 <= SUBLANES - 1 - k)
            sa = jnp.where(keep, pltpu.roll(a3, sh, 1), 1.0)
            sb = jnp.where(keep, pltpu.roll(b3, sh, 1), 0.0)
            b3 = b3 + a3 * sb
            a3 = a3 * sa
        return a3.reshape(a.shape), b3.reshape(b.shape)

    for d in range(2):
        sp = _softplus(-lam_ref[d])
        wa, wx = wa_ref[d, 0], wx_ref[d, 0]
        ba, bx = ba_ref[d], bx_ref[d]

        def run_segment(off, n_rows, is_lat, carry, d=d, sp=sp, wa=wa, wx=wx, ba=ba, bx=bx):
            tbs = min(tb, n_rows)
            nblk = n_rows // tbs

            def blk_body(i, carry):
                bi = i if d == 0 else nblk - 1 - i
                lr0 = pl.multiple_of(bi * tbs, tbs)
                xc = xconv[pl.ds(off + lr0, tbs), :]
                xb = xc.astype(BF16)
                r = _sigmoid(_dot(xb, wa) + ba)
                ig = _sigmoid(_dot(xb, wx) + bx)
                log_a = -LRU_C * r * sp
                gain = jnp.sqrt(jnp.maximum(-_expm1_2x(log_a), 0.0))
                acum, hloc = local_scans(d, jnp.exp(log_a), gain * (ig * xc))
                abuf[0:tbs, :] = acum
                bbuf[0:tbs, :] = hloc
                carry = lax.fori_loop(0, tbs // SUBLANES,
                                      functools.partial(tile_body, d, tbs // SUBLANES), carry)
                if is_lat:
                    if d == 0:
                        hsum[pl.ds(lr0, tbs), :] = bbuf[0:tbs, :]
                    else:
                        y = (hsum[pl.ds(lr0, tbs), :] + bbuf[0:tbs, :]) * _gelu(gl_ref[0, pl.ds(lr0, tbs), :].astype(F32))
                        o_ref[0, pl.ds(lr0, tbs), :] = y.astype(BF16)
                return carry

            return lax.fori_loop(0, nblk, blk_body, carry)

        carry = run_segment(0, n_ctx, False, jnp.zeros((SUBLANES, LANES), F32))
        run_segment(n_ctx, n_lat, True, carry)


def _lru_call(main_l, main_c, conv_w, conv_b, wa, ba, wx, bx, lam):
    b, l, _ = main_l.shape
    lc = main_c.shape[1]
    width = conv_b.shape[0]
    nb = width // LRU_BLOCK
    tb = 256
    vec = pl.BlockSpec((2, 1, LANES), lambda bi, n: (0, 0, n))
    wspec = pl.BlockSpec((2, 1, LRU_BLOCK, LRU_BLOCK), lambda bi, n: (0, n, 0, 0))
    return pl.pallas_call(
        functools.partial(_lru_kernel, tb=tb),
        grid=(b, nb),
        in_specs=[pl.BlockSpec((1, l, LANES), lambda bi, n: (bi, 0, n)),
                  pl.BlockSpec((1, lc, LANES), lambda bi, n: (bi, 0, n)),
                  pl.BlockSpec((1, l, LANES), lambda bi, n, nb=nb: (bi, 0, nb + n)),
                  pl.BlockSpec((SHORT_CONV, LANES), lambda bi, n: (0, n)),
                  pl.BlockSpec((1, LANES), lambda bi, n: (0, n)),
                  wspec, wspec, vec, vec, vec],
        out_specs=pl.BlockSpec((1, l, LANES), lambda bi, n: (bi, 0, n)),
        out_shape=jax.ShapeDtypeStruct((b, l, width), BF16),
        scratch_shapes=[pltpu.VMEM((l + 2 * SUBLANES, LANES), F32),
                        pltpu.VMEM((lc + l, LANES), F32),
                        pltpu.VMEM((tb, LANES), F32),
                        pltpu.VMEM((tb, LANES), F32),
                        pltpu.VMEM((l, LANES), F32)],
        compiler_params=_params("parallel", "parallel"),
        name="rglru",
    )(main_l, main_c, main_l, conv_w.T, conv_b.reshape(1, width), wa, wx,
      ba.reshape(2, 1, width), bx.reshape(2, 1, width), lam.reshape(2, 1, width))


def _dn_gate_kernel(ab_ref, alog_ref, dtb_ref, o_ref, *, heads):
    x = ab_ref[0]
    tm = x.shape[0]
    lane = lax.broadcasted_iota(jnp.int32, x.shape, 1)
    g = -jnp.exp(alog_ref[...]) * _softplus(x + dtb_ref[...])
    ii = lax.broadcasted_iota(jnp.int32, (tm, tm), 0)
    jj = lax.broadcasted_iota(jnp.int32, (tm, tm), 1)
    same = (ii // DN_CHUNK) == (jj // DN_CHUNK)
    lower = jnp.where(same & (jj <= ii), 1.0, 0.0).astype(F32)
    upper = jnp.where(same & (jj >= ii), 1.0, 0.0).astype(F32)
    pre = jnp.dot(lower, g, preferred_element_type=F32, precision=lax.Precision.HIGHEST)
    suf = jnp.dot(upper, g, preferred_element_type=F32, precision=lax.Precision.HIGHEST)
    gc = jnp.where(lane < heads, pre, suf)
    o_ref[0] = jnp.where(lane < 2 * heads, gc, jax.nn.sigmoid(x))


def _dn_gate_call(ab, a_log, dt_bias):
    b, l, _ = ab.shape
    tm = min(l, 256)
    pad = lambda t: jnp.pad(t.reshape(1, -1), ((0, 0), (0, LANES - t.size)))
    return pl.pallas_call(
        functools.partial(_dn_gate_kernel, heads=a_log.shape[1]),
        grid=(b, l // tm),
        in_specs=[pl.BlockSpec((1, tm, LANES), lambda bi, i: (bi, i, 0)),
                  pl.BlockSpec((1, LANES), lambda bi, i: (0, 0)),
                  pl.BlockSpec((1, LANES), lambda bi, i: (0, 0))],
        out_specs=pl.BlockSpec((1, tm, LANES), lambda bi, i: (bi, i, 0)),
        out_shape=jax.ShapeDtypeStruct((b, l, LANES), F32),
        compiler_params=_params("parallel", "parallel"),
        name="deltanet_gates",
    )(ab, pad(a_log), pad(dt_bias))


def _dn_gate_layouts(gates, heads):
    b, l, _ = gates.shape
    t = gates[:, :, :4 * heads].reshape(b, l, 2, 2, heads)
    col = jnp.transpose(t, (0, 4, 1, 2, 3)).reshape(b, heads, l, 4)
    row = jnp.transpose(col.reshape(b, heads, l // DN_CHUNK, DN_CHUNK, 4), (0, 1, 2, 4, 3))
    return col, row


def _dn_kernel(ql_ref, kl_ref, vl_ref, gl_ref, qc_ref, kc_ref, vc_ref, gc_ref,
               cwq_ref, cwk_ref, cwv_ref, coll_ref, rowl_ref, colc_ref, rowc_ref, ng_ref,
               ol_ref, oc_ref,
               xpad, qs, ks, vs, oacc, pbuf, xbuf, atb, abuf, bbuf, qpbuf, egl, *, group):
    n_lat, n_ctx = ql_ref.shape[1], qc_ref.shape[1]
    c = DN_CHUNK
    tbs = 256

    def prep(src_ref, n_rows, off, cw_ref, dst, mode):
        def emit(r0, y):
            y = _silu(y)
            if mode != "v":
                y = y * lax.rsqrt(jnp.sum(y * y, axis=-1, keepdims=True) + EPS)
            if mode == "q":
                y = y * (DN_DK ** -0.5)
            dst[off + r0:off + r0 + y.shape[0], :] = y
        _conv_rows(src_ref, n_rows, xpad, cw_ref[...], None, emit, tbs)

    for src_c, src_l, cw_ref, dst, mode in ((qc_ref, ql_ref, cwq_ref, qs, "q"),
                                            (kc_ref, kl_ref, cwk_ref, ks, "k"),
                                            (vc_ref, vl_ref, cwv_ref, vs, "v")):
        prep(src_c, n_ctx, 0, cw_ref, dst, mode)
        prep(src_l, n_lat, n_ctx, cw_ref, dst, mode)

    ii = lax.broadcasted_iota(jnp.int32, (c, c), 0)
    jj = lax.broadcasted_iota(jnp.int32, (c, c), 1)
    n_apply = int(math.log2(c))
    segments = ((0, n_ctx, colc_ref, rowc_ref), (n_ctx, n_lat, coll_ref, rowl_ref))

    def phase1_group(it, grp, off, col_ref, row_ref):
        def chunk_ids(g):
            ci = it * grp + g
            return ci, pl.multiple_of(ci * c, c), pl.multiple_of(off + ci * c, c), off // c + ci

        def gates(col_ref, lr0, d):
            gcb = col_ref[0, 0, pl.ds(lr0, c), :]
            return gcb[:, d:d + 1], gcb[:, 2 + d:3 + d]

        for g in range(grp):
            ci, lr0, r0, _ = chunk_ids(g)
            q = qs[pl.ds(r0, c), :]
            k = ks[pl.ds(r0, c), :]
            v = vs[pl.ds(r0, c), :]
            kbf = k.astype(BF16)
            kk = _dot_nt(kbf, kbf)
            qk = _dot_nt(q.astype(BF16), kbf)
            rows4 = row_ref[0, 0, ci]
            for d in range(2):
                causal = (ii >= jj) if d == 0 else (ii <= jj)
                strict = (ii > jj) if d == 0 else (ii < jj)
                gc_col, beta = gates(col_ref, lr0, d)
                gc_row = rows4[d:d + 1, :]
                gam = jnp.where(causal, jnp.exp(jnp.where(causal, gc_col - gc_row, 0.0)), 0.0)
                pbuf[0, 2 * g + d] = jnp.where(strict, -(beta * kk) * gam, 0.0).astype(BF16)
                xbuf[2 * g + d] = jnp.concatenate([v * beta, (k * beta) * jnp.exp(gc_col)], axis=1)
                atb[2 * g + d] = (qk * gam).astype(BF16)

        for j in range(1, n_apply):
            for ch in range(2 * grp):
                pb = pbuf[j - 1, ch]
                pbuf[j, ch] = _dot(pb, pb).astype(BF16)
        for j in reversed(range(n_apply)):
            for ch in range(2 * grp):
                xv = xbuf[ch]
                xbuf[ch] = xv + _dot(pbuf[j, ch], xv.astype(BF16))

        for g in range(grp):
            ci, lr0, r0, cg = chunk_ids(g)
            q = qs[pl.ds(r0, c), :]
            k = ks[pl.ds(r0, c), :]
            oloc = None
            for d in range(2):
                gc_col, _ = gates(col_ref, lr0, d)
                g_last = gc_col[c - 1:c, :] if d == 0 else gc_col[0:1, :]
                kdt = (k * jnp.exp(g_last - gc_col)).T.astype(BF16)
                xv = xbuf[2 * g + d]
                ub = xv[:, 0:LANES].astype(BF16)
                wb = xv[:, LANES:2 * LANES].astype(BF16)
                at = atb[2 * g + d]
                abuf[d, cg] = _dot(kdt, wb).astype(BF16)
                bbuf[d, cg] = _dot(kdt, ub).astype(BF16)
                qpbuf[d, pl.ds(r0, c), :] = (q * jnp.exp(gc_col) - _dot(at, wb)).astype(BF16)
                part = _dot(at, ub)
                oloc = part if oloc is None else oloc + part
                egl[d, pl.ds(cg, 1), :] = jnp.broadcast_to(jnp.exp(g_last), (1, LANES))
            oacc[pl.ds(r0, c), :] = oloc

    for off, n_rows, col_ref, row_ref in segments:
        nch = n_rows // c
        grp = min(group, nch)

        def p1_body(it, carry, off=off, col_ref=col_ref, row_ref=row_ref, grp=grp):
            phase1_group(it, grp, off, col_ref, row_ref)
            return carry
        lax.fori_loop(0, nch // grp, p1_body, 0)

    def phase2(d, ci, state, off):
        r0 = pl.multiple_of(off + ci * c, c)
        cg = off // c + ci
        sb = state.astype(BF16)
        oacc[pl.ds(r0, c), :] += _dot(qpbuf[d, pl.ds(r0, c), :], sb)
        return state * egl[d, pl.ds(cg, 1), :] - _dot(abuf[d, cg], sb) + bbuf[d, cg].astype(F32)

    states = (jnp.zeros((DN_DK, LANES), F32), jnp.zeros((DN_DK, LANES), F32))
    for off, n_rows, _, _ in segments:
        nch = n_rows // c

        def p2_body(i, st, off=off, nch=nch):
            return (phase2(0, i, st[0], off), phase2(1, nch - 1 - i, st[1], off))
        states = lax.fori_loop(0, nch, p2_body, states)

    ng = ng_ref[...]
    for off, n_rows, g_ref, o_ref in ((0, n_ctx, gc_ref, oc_ref), (n_ctx, n_lat, gl_ref, ol_ref)):
        t = min(tbs, n_rows)
        for r0 in range(0, n_rows, t):
            o = oacc[off + r0:off + r0 + t, :]
            y = o * lax.rsqrt(jnp.mean(o * o, axis=-1, keepdims=True) + EPS) * ng
            o_ref[0, r0:r0 + t, :] = (y * _silu(g_ref[0, r0:r0 + t, :].astype(F32))).astype(BF16)


def _dn_call(main_l, main_c, col_l, row_l, col_c, row_c, conv_w, norm_g, col0):
    b, l, _ = main_l.shape
    lc = main_c.shape[1]
    heads = col_l.shape[1]
    lt = l + lc
    nch = lt // DN_CHUNK
    cwt = conv_w.T
    group = 8

    def blk(n_rows, which):
        return pl.BlockSpec((1, n_rows, LANES), lambda bi, h, which=which: (bi, 0, col0 + which * heads + h))

    def cw(which):
        return pl.BlockSpec((SHORT_CONV, LANES), lambda bi, h, which=which: (0, which * heads + h))

    in_specs = ([blk(l, w) for w in range(4)] + [blk(lc, w) for w in range(4)] + [cw(0), cw(1), cw(2)] +
                [pl.BlockSpec((1, 1, l, 4), lambda bi, h: (bi, h, 0, 0)),
                 pl.BlockSpec((1, 1, l // DN_CHUNK, 4, DN_CHUNK), lambda bi, h: (bi, h, 0, 0, 0)),
                 pl.BlockSpec((1, 1, lc, 4), lambda bi, h: (bi, h, 0, 0)),
                 pl.BlockSpec((1, 1, lc // DN_CHUNK, 4, DN_CHUNK), lambda bi, h: (bi, h, 0, 0, 0)),
                 pl.BlockSpec((1, LANES), lambda bi, h: (0, 0))])
    return pl.pallas_call(
        functools.partial(_dn_kernel, group=group),
        grid=(b, heads),
        in_specs=in_specs,
        out_specs=[pl.BlockSpec((1, l, LANES), lambda bi, h: (bi, 0, h)),
                   pl.BlockSpec((1, lc, LANES), lambda bi, h: (bi, 0, h))],
        out_shape=[jax.ShapeDtypeStruct((b, l, heads * LANES), BF16),
                   jax.ShapeDtypeStruct((b, lc, heads * LANES), BF16)],
        scratch_shapes=[pltpu.VMEM((l + 2 * SUBLANES, LANES), F32),
                        pltpu.VMEM((lt, LANES), F32),
                        pltpu.VMEM((lt, LANES), F32),
                        pltpu.VMEM((lt, LANES), F32),
                        pltpu.VMEM((lt, LANES), F32),
                        pltpu.VMEM((6, 2 * group, DN_CHUNK, DN_CHUNK), BF16),
                        pltpu.VMEM((2 * group, DN_CHUNK, 2 * LANES), F32),
                        pltpu.VMEM((2 * group, DN_CHUNK, DN_CHUNK), BF16),
                        pltpu.VMEM((2, nch, DN_DK, LANES), BF16),
                        pltpu.VMEM((2, nch, DN_DK, LANES), BF16),
                        pltpu.VMEM((2, lt, LANES), BF16),
                        pltpu.VMEM((2, nch, LANES), F32)],
        compiler_params=_params("parallel", "parallel"),
        name="gated_deltanet",
    )(main_l, main_l, main_l, main_l, main_c, main_c, main_c, main_c, cwt, cwt, cwt,
      col_l, row_l, col_c, row_c, norm_g.reshape(1, LANES))


def _ret_kernel(th_ref, ql_ref, kl_ref, vl_ref, qc_ref, kc_ref, vc_ref, cos_ref, sin_ref, o_ref,
                qs, kts, s_ref, dec_ref, xz_ref):
    n_lat, n_ctx = ql_ref.shape[1], qc_ref.shape[1]
    c = RET_CHUNK
    half = RET_DK // 2
    ncc, nlc = n_ctx // c, n_lat // c
    scale = RET_DK ** -0.5

    icol = lax.broadcasted_iota(jnp.int32, (c, RET_DV), 0)
    ii = lax.broadcasted_iota(jnp.int32, (c, c), 0)
    jj = lax.broadcasted_iota(jnp.int32, (c, c), 1)
    gch = []
    for d in range(2):
        lg = -jnp.exp(th_ref[d, 0])
        lg1 = lg[:, 0:1]
        fidx = (icol if d == 0 else c - 1 - icol).astype(F32)
        rel = (ii - jj) if d == 0 else (jj - ii)
        mask = rel >= 0
        dec_ref[d] = jnp.where(mask, jnp.exp(jnp.where(mask, rel, 0).astype(F32) * lg), 0.0)
        xz_ref[d, 0] = jnp.exp((fidx + 1.0) * lg1)
        xz_ref[d, 1] = jnp.exp((c - 1.0 - fidx) * lg1)
        gch.append(jnp.exp(c * lg1))
    s_ref[...] = jnp.zeros_like(s_ref)

    def prep_ctx(ci, carry):
        r0 = pl.multiple_of(ci * c, c)
        qs[pl.ds(r0, c), :] = qc_ref[0, pl.ds(r0, c), :].astype(BF16)
        k = kc_ref[0, pl.ds(r0, c), :].astype(F32) * scale
        kts[ci] = k.T.astype(BF16)
        return carry

    def prep_lat(ci, carry):
        r0 = pl.multiple_of(ci * c, c)
        cos = cos_ref[pl.ds(r0, c), :]
        sin = sin_ref[pl.ds(r0, c), :]

        def rope(t):
            t1, t2 = t[:, 0:half], t[:, half:2 * half]
            return jnp.concatenate([t1 * cos - t2 * sin, t1 * sin + t2 * cos], axis=-1)

        q = rope(ql_ref[0, pl.ds(r0, c), :].astype(F32))
        k = rope(kl_ref[0, pl.ds(r0, c), :].astype(F32)) * scale
        qs[pl.ds(pl.multiple_of(n_ctx + ci * c, c), c), :] = q.astype(BF16)
        kts[ncc + ci] = k.T.astype(BF16)
        return carry

    lax.fori_loop(0, ncc, prep_ctx, 0)
    lax.fori_loop(0, nlc, prep_lat, 0)

    def one(d, ci, row_off, ch_off, v_ref, write, first):
        r0 = pl.multiple_of(ci * c, c)
        qb = qs[pl.ds(pl.multiple_of(row_off + ci * c, c), c), :]
        kt = kts[ch_off + ci]
        v = v_ref[0, pl.ds(r0, c), :].astype(BF16)
        state = s_ref[d]
        if write:
            inner = _dot(qb, kt) * dec_ref[d]
            o = _dot(inner.astype(BF16), v) + xz_ref[d, 0] * _dot(qb, state.astype(BF16))
            if first:
                o_ref[0, pl.ds(r0, c), :] = o
            else:
                o_ref[0, pl.ds(r0, c), :] += o
        vz = (v.astype(F32) * xz_ref[d, 1]).astype(BF16)
        s_ref[d] = state * gch[d] + _dot(kt, vz)

    def ctx_body(i, carry):
        one(0, i, 0, 0, vc_ref, False, False)
        one(1, ncc - 1 - i, 0, 0, vc_ref, False, False)
        return carry

    lax.fori_loop(0, ncc, ctx_body, 0)
    for lo, hi, first in ((0, nlc // 2, True), (nlc // 2, nlc, False)):
        def lat_body(i, carry, first=first):
            one(0, i, n_ctx, ncc, vl_ref, True, first)
            one(1, nlc - 1 - i, n_ctx, ncc, vl_ref, True, first)
            return carry
        lax.fori_loop(lo, hi, lat_body, 0)


def _ret_call(main_l, main_c, theta, cos, sin, q_col0):
    b, l, _ = main_l.shape
    lc = main_c.shape[1]
    heads = theta.shape[1]
    c = RET_CHUNK
    qb0 = q_col0 // RET_DK
    kb0 = qb0 + heads
    vb0 = (q_col0 + 2 * heads * RET_DK) // RET_DV
    th = jnp.broadcast_to(theta.reshape(2, heads, 1, 1), (2, heads, 1, LANES))
    return pl.pallas_call(
        _ret_kernel,
        grid=(b, heads),
        in_specs=[pl.BlockSpec((2, 1, 1, LANES), lambda bi, h: (0, h, 0, 0)),
                  pl.BlockSpec((1, l, RET_DK), lambda bi, h: (bi, 0, qb0 + h)),
                  pl.BlockSpec((1, l, RET_DK), lambda bi, h: (bi, 0, kb0 + h)),
                  pl.BlockSpec((1, l, RET_DV), lambda bi, h: (bi, 0, vb0 + h)),
                  pl.BlockSpec((1, lc, RET_DK), lambda bi, h: (bi, 0, qb0 + h)),
                  pl.BlockSpec((1, lc, RET_DK), lambda bi, h: (bi, 0, kb0 + h)),
                  pl.BlockSpec((1, lc, RET_DV), lambda bi, h: (bi, 0, vb0 + h)),
                  pl.BlockSpec((l, RET_DK // 2), lambda bi, h: (0, 0)),
                  pl.BlockSpec((l, RET_DK // 2), lambda bi, h: (0, 0))],
        out_specs=pl.BlockSpec((1, l, RET_DV), lambda bi, h: (bi, 0, h)),
        out_shape=jax.ShapeDtypeStruct((b, l, heads * RET_DV), F32),
        scratch_shapes=[pltpu.VMEM((lc + l, RET_DK), BF16),
                        pltpu.VMEM(((lc + l) // c, RET_DK, c), BF16),
                        pltpu.VMEM((2, RET_DK, RET_DV), F32),
                        pltpu.VMEM((2, c, c), F32),
                        pltpu.VMEM((2, 2, c, RET_DV), F32)],
        compiler_params=_params("parallel", "parallel"),
        name="retention",
    )(th, main_l, main_l, main_l, main_c, main_c, main_c, cos, sin)


def _ret_finish_kernel(o_ref, r_ref, y_ref):
    o = o_ref[0]
    y = o * lax.rsqrt(jnp.mean(o * o, axis=-1, keepdims=True) + EPS)
    y_ref[0] = (y * _silu(r_ref[0].astype(F32))).astype(BF16)


def _ret_finish_call(o, main_l, r_col0):
    b, l, w = o.shape
    heads = w // RET_DV
    tm = min(l, 1024)
    rb0 = r_col0 // RET_DV
    return pl.pallas_call(
        _ret_finish_kernel,
        grid=(b, l // tm, heads),
        in_specs=[pl.BlockSpec((1, tm, RET_DV), lambda bi, i, h: (bi, i, h)),
                  pl.BlockSpec((1, tm, RET_DV), lambda bi, i, h: (bi, i, rb0 + h))],
        out_specs=pl.BlockSpec((1, tm, RET_DV), lambda bi, i, h: (bi, i, h)),
        out_shape=jax.ShapeDtypeStruct((b, l, w), BF16),
        compiler_params=_params("parallel", "parallel", "parallel"),
        name="retention_finish",
    )(o, main_l)


def _rope_tables(n_tokens):
    rows = n_tokens // GRID_W
    r, col = jnp.meshgrid(jnp.arange(rows), jnp.arange(GRID_W), indexing='ij')
    n_freq = RET_DK // 4
    inv = ROPE_BASE ** (-jnp.arange(n_freq, dtype=F32) / n_freq)
    ang = jnp.concatenate([r.reshape(-1, 1) * inv, col.reshape(-1, 1) * inv], axis=-1)
    return jnp.cos(ang), jnp.sin(ang)


def _even_layer(xl, xc, mods_l, mods_c, norm1_g, norm2_g, w_in, w_out, s5p, glu_w, glu_b,
                dn_conv_w, dn_a_log, dn_dt_bias, dn_norm_g, ffn, final_g, final_norm):
    s5_width = glu_w.shape[0]
    dn_width = dn_conv_w.shape[0] // 3
    heads = dn_width // LANES
    n_main = s5_width + 4 * dn_width
    w_main = w_in[:, :n_main].astype(BF16)
    w_small = jnp.pad(w_in[:, n_main:], ((0, 0), (0, LANES - (w_in.shape[1] - n_main)))).astype(BF16)

    main_l, ab_l = _inproj_call(xl, norm1_g, mods_l[0], mods_l[1], w_main, w_small)
    main_c, ab_c = _inproj_call(xc, norm1_g, mods_c[0], mods_c[1], w_main, w_small)

    y_l, y_c = _s5_scan_call(main_l, main_c, *s5p)
    glu_wb = glu_w.astype(BF16)
    s5_l = _s5_glu_call(y_l, glu_wb, glu_b)
    s5_c = _s5_glu_call(y_c, glu_wb, glu_b)

    col_l, row_l = _dn_gate_layouts(_dn_gate_call(ab_l, dn_a_log, dn_dt_bias), heads)
    col_c, row_c = _dn_gate_layouts(_dn_gate_call(ab_c, dn_a_log, dn_dt_bias), heads)
    dn_l, dn_c = _dn_call(main_l, main_c, col_l, row_l, col_c, row_c, dn_conv_w, dn_norm_g,
                          s5_width // LANES)

    w_o1 = w_out[:s5_width].astype(BF16)
    w_o2 = w_out[s5_width:].astype(BF16)
    xl = _outproj_call(xl, mods_l[2], s5_l, dn_l, w_o1, w_o2)
    xc = _outproj_call(xc, mods_c[2], s5_c, dn_c, w_o1, w_o2)
    w1, w3, w2 = ffn
    xl = _ffn_call(xl, norm2_g, mods_l[3], mods_l[4], mods_l[5], w1, w3, w2, final_g, final_norm)
    xc = _ffn_call(xc, norm2_g, mods_c[3], mods_c[4], mods_c[5], w1, w3, w2, final_g, False)
    return xl, xc


def _odd_layer(xl, xc, mods_l, mods_c, norm1_g, norm2_g, w_in, w_out, conv_w, conv_b, wa, ba, wx, bx, lam,
               theta, cos, sin, ffn, final_g, final_norm):
    lru_width = conv_b.shape[0]
    heads = theta.shape[1]
    w_inb = w_in.astype(BF16)
    main_l = _inproj_call(xl, norm1_g, mods_l[0], mods_l[1], w_inb)
    main_c = _inproj_call(xc, norm1_g, mods_c[0], mods_c[1], w_inb)

    lru_l = _lru_call(main_l, main_c, conv_w, conv_b, wa.astype(BF16), ba, wx.astype(BF16), bx, lam)
    q_col0 = 2 * lru_width
    o = _ret_call(main_l, main_c, theta, cos, sin, q_col0)
    ret_l = _ret_finish_call(o, main_l, q_col0 + 2 * heads * RET_DK + heads * RET_DV)

    w_o1 = w_out[:lru_width].astype(BF16)
    w_o2 = w_out[lru_width:].astype(BF16)
    xl = _outproj_call(xl, mods_l[2], lru_l, ret_l, w_o1, w_o2)
    w1, w3, w2 = ffn
    return _ffn_call(xl, norm2_g, mods_l[3], mods_l[4], mods_l[5], w1, w3, w2, final_g, final_norm)


def kernel(x, c, ctx, c_ctx, mod_w, mod_b, norm1_g, norm2_g, ffn_w1, ffn_w3, ffn_w2, final_g, even_w_in, even_w_out, s5_lam_re, s5_lam_im, s5_log_step, s5_b_re, s5_b_im, s5_c_re, s5_c_im, s5_d, s5_glu_w, s5_glu_b, dn_conv_w, dn_a_log, dn_dt_bias, dn_norm_g, odd_w_in, odd_w_out, lru_conv_w, lru_conv_b, lru_wa, lru_ba, lru_wx, lru_bx, lru_lam, ret_theta):
    bsz, n_tok, d = x.shape
    depth = mod_w.shape[0]
    assert depth == 2 and bsz + 1 <= SUBLANES
    cos, sin = _rope_tables(n_tok)

    rows = jnp.concatenate([c, c_ctx[None, :], jnp.zeros((SUBLANES - bsz - 1, d), F32)], axis=0)
    mods = _mod_call(rows, mod_w, mod_b)

    def split_mods(i):
        m = mods[i].reshape(SUBLANES, 6, d)
        ml = [m[:bsz, k][:, None, :] for k in range(6)]
        mc = [jnp.broadcast_to(m[bsz, k][None, None, :], (bsz, 1, d)) for k in range(6)]
        return ml, mc

    xl, xc = x, ctx
    ml, mc = split_mods(0)
    s5p = _s5_weights(s5_lam_re[0], s5_lam_im[0], s5_log_step[0], s5_b_re[0], s5_b_im[0],
                      s5_c_re[0], s5_c_im[0]) + (s5_d[0],)
    ffn0 = (ffn_w1[0].astype(BF16), ffn_w3[0].astype(BF16), ffn_w2[0].astype(BF16))
    xl, xc = _even_layer(xl, xc, ml, mc, norm1_g[0], norm2_g[0], even_w_in[0], even_w_out[0],
                         s5p, s5_glu_w[0], s5_glu_b[0],
                         dn_conv_w[0], dn_a_log[0], dn_dt_bias[0], dn_norm_g[0], ffn0, final_g, False)
    ml, mc = split_mods(1)
    ffn1 = (ffn_w1[1].astype(BF16), ffn_w3[1].astype(BF16), ffn_w2[1].astype(BF16))
    return _odd_layer(xl, xc, ml, mc, norm1_g[1], norm2_g[1], odd_w_in[0], odd_w_out[0],
                      lru_conv_w[0], lru_conv_b[0], lru_wa[0], lru_ba[0], lru_wx[0], lru_bx[0], lru_lam[0],
                      ret_theta[0], cos, sin, ffn1, final_g, True)
```

```python
import functools
import math

import jax
import jax.numpy as jnp
from jax import lax
from jax.experimental import pallas as pl
from jax.experimental.pallas import tpu as pltpu

F32 = jnp.float32
BF16 = jnp.bfloat16

EPS = 1e-6
GRID_W = 64
ROPE_BASE = 10000.0

S5_GROUP = 16
S5_STATE = 64
S5_BLOCK = 8

DN_DK = 128
DN_CHUNK = 64
SHORT_CONV = 4

LRU_BLOCK = 128
LRU_C = 8.0

RET_DK = 256
RET_DV = 512
RET_CHUNK = 128

LANES = 128
SUBLANES = 8
VMEM_LIMIT = 56 * 1024 * 1024


def _params(*sem):
    return pltpu.CompilerParams(dimension_semantics=sem, vmem_limit_bytes=VMEM_LIMIT)


def _silu(x):
    return x * jax.nn.sigmoid(x)


def _gelu(x):
    return 0.5 * x * (1.0 + jnp.tanh(0.7978845608028654 * (x + 0.044715 * (x * x * x))))


def _softplus(x):
    return jnp.maximum(x, 0.0) + jnp.log1p(jnp.exp(-jnp.abs(x)))


def _sigmoid(x):
    return 0.5 * (1.0 + jnp.tanh(0.5 * x))


def _expm1_2x(x):
    t = jnp.tanh(x)
    return 2.0 * t / (1.0 - t)


def _adaln(x, g, shift, scale):
    y = x * lax.rsqrt(jnp.mean(x * x, axis=-1, keepdims=True) + EPS)
    return (y * g) * (1.0 + scale) + shift


def _dot(a, b):
    return jnp.dot(a, b, preferred_element_type=F32)


def _dot_nt(a, b):
    return lax.dot_general(a, b, (((1,), (1,)), ((), ())), preferred_element_type=F32)


def _mod_kernel(s_ref, w_ref, b_ref, o_ref):
    s = _silu(s_ref[...])
    o_ref[0] = _dot(s.astype(BF16), w_ref[0].astype(BF16)) + b_ref[0]


def _mod_call(rows, mod_w, mod_b):
    depth, d, n = mod_w.shape
    tn = 1024
    return pl.pallas_call(
        _mod_kernel,
        grid=(depth, n // tn),
        in_specs=[pl.BlockSpec((SUBLANES, d), lambda i, j: (0, 0)),
                  pl.BlockSpec((1, d, tn), lambda i, j: (i, 0, j)),
                  pl.BlockSpec((1, 1, tn), lambda i, j: (i, 0, j))],
        out_specs=pl.BlockSpec((1, SUBLANES, tn), lambda i, j: (i, 0, j)),
        out_shape=jax.ShapeDtypeStruct((depth, SUBLANES, n), F32),
        compiler_params=_params("parallel", "parallel"),
        name="mod_proj",
    )(rows, mod_w, mod_b.reshape(depth, 1, n))


def _inproj_kernel(x_ref, g_ref, sh_ref, sc_ref, w_ref, *rest, has_small):
    if has_small:
        ws_ref, o_ref, os_ref, h_ref = rest
    else:
        o_ref, h_ref = rest

    @pl.when(pl.program_id(2) == 0)
    def _():
        h = _adaln(x_ref[0], g_ref[...], sh_ref[0], sc_ref[0]).astype(BF16)
        h_ref[...] = h
        if has_small:
            os_ref[0] = _dot(h, ws_ref[...])

    o_ref[0] = _dot(h_ref[...], w_ref[...]).astype(o_ref.dtype)


def _inproj_call(x, g, shift, scale, w, w_small=None):
    b, l, d = x.shape
    n = w.shape[1]
    tm = min(l, 1024)
    tn = 512
    has_small = w_small is not None
    in_specs = [pl.BlockSpec((1, tm, d), lambda bi, i, j: (bi, i, 0)),
                pl.BlockSpec((1, d), lambda bi, i, j: (0, 0)),
                pl.BlockSpec((1, 1, d), lambda bi, i, j: (bi, 0, 0)),
                pl.BlockSpec((1, 1, d), lambda bi, i, j: (bi, 0, 0)),
                pl.BlockSpec((d, tn), lambda bi, i, j: (0, j))]
    out_specs = [pl.BlockSpec((1, tm, tn), lambda bi, i, j: (bi, i, j))]
    out_shape = [jax.ShapeDtypeStruct((b, l, n), BF16)]
    args = [x, g.reshape(1, d), shift, scale, w]
    if has_small:
        in_specs.append(pl.BlockSpec((d, LANES), lambda bi, i, j: (0, 0)))
        out_specs.append(pl.BlockSpec((1, tm, LANES), lambda bi, i, j: (bi, i, 0)))
        out_shape.append(jax.ShapeDtypeStruct((b, l, LANES), F32))
        args.append(w_small)
    outs = pl.pallas_call(
        functools.partial(_inproj_kernel, has_small=has_small),
        grid=(b, l // tm, n // tn),
        in_specs=in_specs, out_specs=out_specs, out_shape=out_shape,
        scratch_shapes=[pltpu.VMEM((tm, d), BF16)],
        compiler_params=_params("parallel", "parallel", "arbitrary"),
        name="adaln_inproj",
    )(*args)
    return outs if has_small else outs[0]


def _outproj_kernel(x_ref, gate_ref, a1_ref, a2_ref, w1_ref, w2_ref, o_ref):
    y = _dot(a1_ref[0], w1_ref[...]) + _dot(a2_ref[0], w2_ref[...])
    o_ref[0] = x_ref[0] + gate_ref[0] * y


def _outproj_call(x, gate, a1, a2, w1, w2):
    b, l, d = x.shape
    k1, k2 = a1.shape[2], a2.shape[2]
    tm = min(l, 1024)
    tn = 512
    return pl.pallas_call(
        _outproj_kernel,
        grid=(b, l // tm, d // tn),
        in_specs=[pl.BlockSpec((1, tm, tn), lambda bi, i, j: (bi, i, j)),
                  pl.BlockSpec((1, 1, tn), lambda bi, i, j: (bi, 0, j)),
                  pl.BlockSpec((1, tm, k1), lambda bi, i, j: (bi, i, 0)),
                  pl.BlockSpec((1, tm, k2), lambda bi, i, j: (bi, i, 0)),
                  pl.BlockSpec((k1, tn), lambda bi, i, j: (0, j)),
                  pl.BlockSpec((k2, tn), lambda bi, i, j: (0, j))],
        out_specs=pl.BlockSpec((1, tm, tn), lambda bi, i, j: (bi, i, j)),
        out_shape=jax.ShapeDtypeStruct((b, l, d), F32),
        compiler_params=_params("parallel", "parallel", "arbitrary"),
        name="outproj_residual",
    )(x, gate, a1, a2, w1, w2)


def _ffn_kernel(x_ref, g_ref, sh_ref, sc_ref, gate_ref, w1_ref, w3_ref, w2_ref, fg_ref, o_ref,
                h_ref, acc_ref, *, final_norm):
    f = pl.program_id(2)

    @pl.when(f == 0)
    def _():
        h_ref[...] = _adaln(x_ref[0], g_ref[...], sh_ref[0], sc_ref[0]).astype(BF16)
        acc_ref[...] = jnp.zeros_like(acc_ref)

    h = h_ref[...]
    a = _dot(h, w1_ref[...])
    b = _dot(h, w3_ref[...])
    acc_ref[...] += _dot((_silu(a) * b).astype(BF16), w2_ref[...])

    @pl.when(f == pl.num_programs(2) - 1)
    def _():
        y = x_ref[0] + gate_ref[0] * acc_ref[...]
        if final_norm:
            y = y * lax.rsqrt(jnp.mean(y * y, axis=-1, keepdims=True) + EPS) * fg_ref[...]
        o_ref[0] = y


def _ffn_call(x, g, shift, scale, gate, w1, w3, w2, final_g, final_norm):
    b, l, d = x.shape
    dff = w1.shape[1]
    tm = min(l, 1024)
    tf = 512
    vec = pl.BlockSpec((1, 1, d), lambda bi, i, f: (bi, 0, 0))
    return pl.pallas_call(
        functools.partial(_ffn_kernel, final_norm=final_norm),
        grid=(b, l // tm, dff // tf),
        in_specs=[pl.BlockSpec((1, tm, d), lambda bi, i, f: (bi, i, 0), pipeline_mode=pl.Buffered(1)),
                  pl.BlockSpec((1, d), lambda bi, i, f: (0, 0)),
                  vec, vec, vec,
                  pl.BlockSpec((d, tf), lambda bi, i, f: (0, f)),
                  pl.BlockSpec((d, tf), lambda bi, i, f: (0, f)),
                  pl.BlockSpec((tf, d), lambda bi, i, f: (f, 0)),
                  pl.BlockSpec((1, d), lambda bi, i, f: (0, 0))],
        out_specs=pl.BlockSpec((1, tm, d), lambda bi, i, f: (bi, i, 0), pipeline_mode=pl.Buffered(1)),
        out_shape=jax.ShapeDtypeStruct((b, l, d), F32),
        scratch_shapes=[pltpu.VMEM((tm, d), BF16), pltpu.VMEM((tm, d), F32)],
        compiler_params=_params("parallel", "parallel", "arbitrary"),
        name="ffn_swiglu",
    )(x, g.reshape(1, d), shift, scale, gate, w1, w3, w2, final_g.reshape(1, d))


def _s5_disc_kernel(lre_ref, lim_ref, ls_ref, bre_ref, bim_ref, are_ref, aim_ref, bbre_ref, bbim_ref):
    lre = jnp.minimum(lre_ref[...], -1e-4)
    lim = lim_ref[...]
    dt = jnp.exp(ls_ref[...])
    mag = jnp.exp(lre * dt)
    ar = mag * jnp.cos(lim * dt)
    ai = mag * jnp.sin(lim * dt)
    nr, ni = ar - 1.0, ai
    den = lre * lre + lim * lim
    cr = (nr * lre + ni * lim) / den
    ci = (ni * lre - nr * lim) / den
    bre, bim = bre_ref[...], bim_ref[...]
    are_ref[...] = ar
    aim_ref[...] = ai
    bbre_ref[...] = cr * bre - ci * bim
    bbim_ref[...] = cr * bim + ci * bre


def _s5_taps_kernel(ar_ref, ai_ref, br_ref, bi_ref, cr_ref, ci_ref, w1_ref, w2_ref, apw_ref, ccr, cci, ktb):
    t_blk = S5_BLOCK
    rows, p = ar_ref.shape[2], ar_ref.shape[3]
    s = S5_GROUP
    gs = rows // s
    tl = t_blk * rows
    hi = lax.Precision.HIGHEST
    nt = (((1,), (1,)), ((), ()))
    iota = lambda shape, ax: lax.broadcasted_iota(jnp.int32, shape, ax)
    same = (iota((rows, rows), 0) // s) == (iota((rows, rows), 1) // s)
    dmask = (iota((rows, gs * p), 0) // s) == (iota((rows, gs * p), 1) // p)
    gmask = (iota((gs * p, rows), 0) // p) == (iota((gs * p, rows), 1) // s)

    for d in range(2):
        ar, ai = ar_ref[d, 0], ai_ref[d, 0]
        br, bi = br_ref[d, 0], bi_ref[d, 0]
        cr, ci = cr_ref[d, 0], ci_ref[d, 0]
        pw = [(jnp.ones_like(ar), jnp.zeros_like(ar))]
        for _ in range(t_blk):
            pr, pi = pw[-1]
            pw.append((pr * ar - pi * ai, pr * ai + pi * ar))
        for t in range(t_blk):
            sl = slice(t * rows, (t + 1) * rows)
            pr, pi = pw[t]
            ccr[sl, :] = cr * pr - ci * pi
            cci[sl, :] = cr * pi + ci * pr
            dpr, dpi = pw[t_blk - 1 - t] if d == 0 else pw[t]
            for c, piece in enumerate((br * dpr - bi * dpi, br * dpi + bi * dpr)):
                wide = jnp.concatenate([piece] * gs, axis=1)
                c0 = tl + (2 * d + c) * gs * p
                w1_ref[0, sl, c0:c0 + gs * p] = jnp.where(dmask, wide, 0.0).astype(BF16)
            gpr, gpi = pw[t + 1] if d == 0 else pw[t_blk - t]
            for c, piece in enumerate((cr * gpr - ci * gpi, -(cr * gpi + ci * gpr))):
                tall = jnp.concatenate([piece.T] * gs, axis=0)
                r0 = (2 * d + c) * gs * p
                w2_ref[0, r0:r0 + gs * p, sl] = jnp.where(gmask, tall, 0.0).astype(BF16)
        ktb[d] = (lax.dot_general(br, ccr[...], nt, precision=hi, preferred_element_type=F32)
                  - lax.dot_general(bi, cci[...], nt, precision=hi, preferred_element_type=F32))
        qr, qi = pw[t_blk]
        er, ei = jnp.ones_like(qr), jnp.zeros_like(qr)
        for kk in range(9):
            apw_ref[0, d, 3 * kk] = er
            apw_ref[0, d, 3 * kk + 1] = ei
            apw_ref[0, d, 3 * kk + 2] = -ei
            er, ei = er * qr - ei * qi, er * qi + ei * qr

    for t_in in range(t_blk):
        for t_out in range(t_blk):
            if t_out >= t_in:
                lag = t_out - t_in
                piece = ktb[0, :, lag * rows:(lag + 1) * rows]
            if t_out <= t_in:
                lag = t_in - t_out
                back = ktb[1, :, lag * rows:(lag + 1) * rows]
                piece = back if t_out < t_in else piece + back
            w1_ref[0, t_in * rows:(t_in + 1) * rows, t_out * rows:(t_out + 1) * rows] = (
                jnp.where(same, piece, 0.0).astype(BF16))


def _s5_weights(lam_re, lam_im, log_step, b_re, b_im, c_re, c_im):
    nd, g, p = lam_re.shape
    s = b_re.shape[-1]
    t_blk = S5_BLOCK
    rows = nd * g * s
    rep = lambda t: jnp.repeat(t.reshape(nd * g, p), s, axis=0)
    ls = jnp.broadcast_to(log_step.reshape(nd * g, 1), (nd * g, p))
    tb = lambda t: jnp.transpose(t, (0, 1, 3, 2)).reshape(rows, p)
    shp = jax.ShapeDtypeStruct((rows, p), F32)
    are, aim, bbre, bbim = pl.pallas_call(
        _s5_disc_kernel, out_shape=[shp, shp, shp, shp], name="s5_discretise",
    )(rep(lam_re), rep(lam_im), rep(ls), tb(b_re), tb(b_im))
    gs = LANES // s
    nslab = g // gs
    tl = t_blk * LANES
    hw = 2 * gs * p
    mat = pl.BlockSpec((nd, 1, LANES, p), lambda i: (0, i, 0, 0))
    slab = lambda t: t.reshape(nd, nslab, LANES, p)
    w1, w2, apw = pl.pallas_call(
        _s5_taps_kernel,
        grid=(nslab,),
        in_specs=[mat] * 6,
        out_specs=[pl.BlockSpec((1, tl, tl + nd * hw), lambda i: (i, 0, 0)),
                   pl.BlockSpec((1, nd * hw, tl), lambda i: (i, 0, 0)),
                   pl.BlockSpec((1, nd, 27, LANES, p), lambda i: (i, 0, 0, 0, 0))],
        out_shape=[jax.ShapeDtypeStruct((nslab, tl, tl + nd * hw), BF16),
                   jax.ShapeDtypeStruct((nslab, nd * hw, tl), BF16),
                   jax.ShapeDtypeStruct((nslab, nd, 27, LANES, p), F32)],
        scratch_shapes=[pltpu.VMEM((tl, p), F32), pltpu.VMEM((tl, p), F32), pltpu.VMEM((nd, LANES, tl), F32)],
        compiler_params=_params("parallel"),
        name="s5_block_taps",
    )(slab(are), slab(aim), slab(bbre), slab(bbim), slab(c_re), slab(c_im))
    ap = apw.reshape(nslab, nd, 9, 3, gs, s, p)[:, :, :, :, :, 0, :].reshape(nslab, nd, 9, 3, gs * p)
    p1 = jnp.concatenate([ap[:, :, :, 0], ap[:, :, :, 0]], axis=-1)
    p2 = jnp.concatenate([ap[:, :, :, 2], ap[:, :, :, 1]], axis=-1)
    return w1, w2, jnp.stack([p1, p2], axis=3).reshape(nslab, nd, 18, hw)


def _s5_kernel(ul_ref, uc_ref, w1_ref, w2_ref, apw_ref, dsk_ref, yl_ref, yc_ref,
               uf, ubuf, zbuf, hbuf, sbuf):
    t_blk = S5_BLOCK
    n_lat, n_ctx = ul_ref.shape[1], uc_ref.shape[1]
    nb_ctx, nb_lat = n_ctx // t_blk, n_lat // t_blk
    nb = nb_ctx + nb_lat
    ts = t_blk * LANES
    hw = apw_ref.shape[3]
    rows = 512

    for src, off, n in ((uc_ref, 0, n_ctx), (ul_ref, n_ctx, n_lat)):
        for r0 in range(0, n, min(rows, n)):
            r1 = min(r0 + rows, n)
            uf[off + r0:off + r1, :] = src[0, r0:r1, :].astype(F32)
    for t in range(t_blk):
        ubuf[:, t * LANES:(t + 1) * LANES] = uf[pl.ds(t, nb, stride=t_blk), :].astype(BF16)
    zbuf[...] = _dot(ubuf[...], w1_ref[0])

    row = lax.broadcasted_iota(jnp.int32, (SUBLANES, hw), 0)

    def cmul(x, p1, p2):
        return x * p1 + pltpu.roll(x, hw // 2, 1) * p2

    for d in range(2):
        col = ts + d * hw
        pw = lambda k, d=d: (apw_ref[0, d, 2 * k:2 * k + 1, :], apw_ref[0, d, 2 * k + 1:2 * k + 2, :])

        def local(i, carry, d=d, col=col, pw=pw):
            r0 = pl.multiple_of(i * SUBLANES, SUBLANES)
            x = zbuf[pl.ds(r0, SUBLANES), col:col + hw]
            for k in (1, 2, 4):
                keep = (row >= k) if d == 0 else (row <= SUBLANES - 1 - k)
                sh = jnp.where(keep, pltpu.roll(x, k if d == 0 else SUBLANES - k, 0), 0.0)
                p1, p2 = pw(k)
                x = x + cmul(sh, p1, p2)
            sbuf[pl.ds(r0, SUBLANES), :] = x
            return carry
        lax.fori_loop(0, nb // SUBLANES, local, 0)

        pex1 = jnp.zeros((SUBLANES, hw), F32)
        pex2 = jnp.zeros((SUBLANES, hw), F32)
        for r in range(SUBLANES):
            p1, p2 = pw(r if d == 0 else SUBLANES - 1 - r)
            pex1 = jnp.where(row == r, p1, pex1)
            pex2 = jnp.where(row == r, p2, pex2)
        p81, p82 = pw(SUBLANES)
        keep1 = (row >= 1) if d == 0 else (row <= SUBLANES - 2)
        last = SUBLANES - 1 if d == 0 else 0
        carry = jnp.zeros((SUBLANES, hw), F32)
        for seg0, ntile in ((0, nb_ctx // SUBLANES), (nb_ctx // SUBLANES, nb_lat // SUBLANES)):
            def chain(i, carry, seg0=seg0, ntile=ntile, d=d):
                ti = seg0 + (i if d == 0 else ntile - 1 - i)
                r0 = pl.multiple_of(ti * SUBLANES, SUBLANES)
                s_t = sbuf[pl.ds(r0, SUBLANES), :]
                excl = jnp.where(keep1, pltpu.roll(s_t, 1 if d == 0 else SUBLANES - 1, 0), 0.0)
                hbuf[pl.ds(r0, SUBLANES), d * hw:(d + 1) * hw] = excl + cmul(carry, pex1, pex2)
                return jnp.broadcast_to(s_t[last:last + 1], (SUBLANES, hw)) + cmul(carry, p81, p82)
            carry = lax.fori_loop(0, ntile, chain, carry)

    zbuf[:, 0:ts] += _dot(hbuf[...].astype(BF16), w2_ref[0])
    dsk = dsk_ref[...]
    for t in range(t_blk):
        y_t = zbuf[:, t * LANES:(t + 1) * LANES] + dsk * uf[pl.ds(t, nb, stride=t_blk), :]
        yc_ref[0, pl.ds(t, nb_ctx, stride=t_blk), :] = y_t[0:nb_ctx]
        yl_ref[0, pl.ds(t, nb_lat, stride=t_blk), :] = y_t[nb_ctx:nb]


def _s5_scan_call(main_l, main_c, w1, w2, apw, d_skip):
    b, l, _ = main_l.shape
    lc = main_c.shape[1]
    nslab, ts, n1 = w1.shape
    hw = apw.shape[3]
    width = d_skip.shape[0]
    nb = (l + lc) // S5_BLOCK
    return pl.pallas_call(
        _s5_kernel,
        grid=(nslab, b),
        in_specs=[pl.BlockSpec((1, l, LANES), lambda s, bi: (bi, 0, s)),
                  pl.BlockSpec((1, lc, LANES), lambda s, bi: (bi, 0, s)),
                  pl.BlockSpec((1, ts, n1), lambda s, bi: (s, 0, 0)),
                  pl.BlockSpec((1, 2 * hw, ts), lambda s, bi: (s, 0, 0)),
                  pl.BlockSpec((1, 2, 18, hw), lambda s, bi: (s, 0, 0, 0)),
                  pl.BlockSpec((1, LANES), lambda s, bi: (0, s))],
        out_specs=[pl.BlockSpec((1, l, LANES), lambda s, bi: (bi, 0, s)),
                   pl.BlockSpec((1, lc, LANES), lambda s, bi: (bi, 0, s))],
        out_shape=[jax.ShapeDtypeStruct((b, l, width), F32),
                   jax.ShapeDtypeStruct((b, lc, width), F32)],
        scratch_shapes=[pltpu.VMEM((l + lc, LANES), F32),
                        pltpu.VMEM((nb, ts), BF16),
                        pltpu.VMEM((nb, n1), F32),
                        pltpu.VMEM((nb, 2 * hw), F32),
                        pltpu.VMEM((nb, hw), F32)],
        compiler_params=_params("parallel", "parallel"),
        name="s5_scan",
    )(main_l, main_c, w1, w2, apw, d_skip.reshape(1, width))


def _s5_glu_kernel(y_ref, w_ref, b_ref, o_ref):
    g = _gelu(y_ref[0])
    o_ref[0] = (g * jax.nn.sigmoid(_dot(g.astype(BF16), w_ref[...]) + b_ref[...])).astype(BF16)


def _s5_glu_call(y, glu_w, glu_b):
    b, l, w = y.shape
    tm = min(l, 1024)
    return pl.pallas_call(
        _s5_glu_kernel,
        grid=(b, l // tm),
        in_specs=[pl.BlockSpec((1, tm, w), lambda bi, i: (bi, i, 0)),
                  pl.BlockSpec((w, w), lambda bi, i: (0, 0)),
                  pl.BlockSpec((1, w), lambda bi, i: (0, 0))],
        out_specs=pl.BlockSpec((1, tm, w), lambda bi, i: (bi, i, 0)),
        out_shape=jax.ShapeDtypeStruct((b, l, w), BF16),
        compiler_params=_params("parallel", "parallel"),
        name="s5_glu",
    )(y, glu_w, glu_b.reshape(1, w))


def _conv_rows(src_ref, n_rows, xpad, cw, bias, emit, tbs):
    zeros = jnp.zeros((SUBLANES, LANES), F32)
    xpad[0:SUBLANES, :] = zeros
    xpad[SUBLANES + n_rows:2 * SUBLANES + n_rows, :] = zeros
    tbs = min(tbs, n_rows)
    for r0 in range(0, n_rows, tbs):
        xpad[SUBLANES + r0:SUBLANES + r0 + tbs, :] = src_ref[0, r0:r0 + tbs, :].astype(F32)
    for r0 in range(0, n_rows, tbs):
        acc = None
        for j in range(SHORT_CONV):
            off = SUBLANES + r0 + j - SHORT_CONV // 2
            term = cw[j:j + 1, :] * xpad[off:off + tbs, :]
            acc = term if acc is None else acc + term
        if bias is not None:
            acc = acc + bias
        emit(r0, acc)


def _lru_kernel(xl_ref, xc_ref, gl_ref, cw_ref, cb_ref, wa_ref, wx_ref, ba_ref, bx_ref, lam_ref, o_ref,
                xpad, xconv, abuf, bbuf, hsum, *, tb):
    n_lat, n_ctx = xl_ref.shape[1], xc_ref.shape[1]
    cw = cw_ref[...]
    cb = cb_ref[...]

    def put(off):
        def emit(r0, y):
            xconv[off + r0:off + r0 + y.shape[0], :] = y
        return emit

    _conv_rows(xc_ref, n_ctx, xpad, cw, cb, put(0), tb)
    _conv_rows(xl_ref, n_lat, xpad, cw, cb, put(n_ctx), tb)

    row = lax.broadcasted_iota(jnp.int32, (SUBLANES, LANES), 0)

    def tile_body(d, ntiles, t, carry):
        ti = t if d == 0 else ntiles - 1 - t
        r0 = pl.multiple_of(ti * SUBLANES, SUBLANES)
        h = bbuf[pl.ds(r0, SUBLANES), :] + abuf[pl.ds(r0, SUBLANES), :] * carry
        bbuf[pl.ds(r0, SUBLANES), :] = h
        last = SUBLANES - 1 if d == 0 else 0
        return jnp.broadcast_to(h[last:last + 1], (SUBLANES, LANES))

    def local_scans(d, a, b):
        nt = a.shape[0] // SUBLANES
        a3 = a.reshape(nt, SUBLANES, LANES)
        b3 = b.reshape(nt, SUBLANES, LANES)
        row3 = lax.broadcasted_iota(jnp.int32, a3.shape, 1)
        for k in (1, 2, 4):
            sh = k if d == 0 else SUBLANES - k
            keep = (row3 >= k) if d == 0 else (row3 <= SUBLANES - 1 - k)
            sa = jnp.where(keep, pltpu.roll(a3, sh, 1), 1.0)
            sb = jnp.where(keep, pltpu.roll(b3, sh, 1), 0.0)
            b3 = b3 + a3 * sb
            a3 = a3 * sa
        return a3.reshape(a.shape), b3.reshape(b.shape)

    for d in range(2):
        sp = _softplus(-lam_ref[d])
        wa, wx = wa_ref[d, 0], wx_ref[d, 0]
        ba, bx = ba_ref[d], bx_ref[d]

        def run_segment(off, n_rows, is_lat, carry, d=d, sp=sp, wa=wa, wx=wx, ba=ba, bx=bx):
            tbs = min(tb, n_rows)
            nblk = n_rows // tbs

            def blk_body(i, carry):
                bi = i if d == 0 else nblk - 1 - i
                lr0 = pl.multiple_of(bi * tbs, tbs)
                xc = xconv[pl.ds(off + lr0, tbs), :]
                xb = xc.astype(BF16)
                r = _sigmoid(_dot(xb, wa) + ba)
                ig = _sigmoid(_dot(xb, wx) + bx)
                log_a = -LRU_C * r * sp
                gain = jnp.sqrt(jnp.maximum(-_expm1_2x(log_a), 0.0))
                acum, hloc = local_scans(d, jnp.exp(log_a), gain * (ig * xc))
                abuf[0:tbs, :] = acum
                bbuf[0:tbs, :] = hloc
                carry = lax.fori_loop(0, tbs // SUBLANES,
                                      functools.partial(tile_body, d, tbs // SUBLANES), carry)
                if is_lat:
                    if d == 0:
                        hsum[pl.ds(lr0, tbs), :] = bbuf[0:tbs, :]
                    else:
                        y = (hsum[pl.ds(lr0, tbs), :] + bbuf[0:tbs, :]) * _gelu(gl_ref[0, pl.ds(lr0, tbs), :].astype(F32))
                        o_ref[0, pl.ds(lr0, tbs), :] = y.astype(BF16)
                return carry

            return lax.fori_loop(0, nblk, blk_body, carry)

        carry = run_segment(0, n_ctx, False, jnp.zeros((SUBLANES, LANES), F32))
        run_segment(n_ctx, n_lat, True, carry)


def _lru_call(main_l, main_c, conv_w, conv_b, wa, ba, wx, bx, lam):
    b, l, _ = main_l.shape
    lc = main_c.shape[1]
    width = conv_b.shape[0]
    nb = width // LRU_BLOCK
    tb = 256
    vec = pl.BlockSpec((2, 1, LANES), lambda bi, n: (0, 0, n))
    wspec = pl.BlockSpec((2, 1, LRU_BLOCK, LRU_BLOCK), lambda bi, n: (0, n, 0, 0))
    return pl.pallas_call(
        functools.partial(_lru_kernel, tb=tb),
        grid=(b, nb),
        in_specs=[pl.BlockSpec((1, l, LANES), lambda bi, n: (bi, 0, n)),
                  pl.BlockSpec((1, lc, LANES), lambda bi, n: (bi, 0, n)),
                  pl.BlockSpec((1, l, LANES), lambda bi, n, nb=nb: (bi, 0, nb + n)),
                  pl.BlockSpec((SHORT_CONV, LANES), lambda bi, n: (0, n)),
                  pl.BlockSpec((1, LANES), lambda bi, n: (0, n)),
                  wspec, wspec, vec, vec, vec],
        out_specs=pl.BlockSpec((1, l, LANES), lambda bi, n: (bi, 0, n)),
        out_shape=jax.ShapeDtypeStruct((b, l, width), BF16),
        scratch_shapes=[pltpu.VMEM((l + 2 * SUBLANES, LANES), F32),
                        pltpu.VMEM((lc + l, LANES), F32),
                        pltpu.VMEM((tb, LANES), F32),
                        pltpu.VMEM((tb, LANES), F32),
                        pltpu.VMEM((l, LANES), F32)],
        compiler_params=_params("parallel", "parallel"),
        name="rglru",
    )(main_l, main_c, main_l, conv_w.T, conv_b.reshape(1, width), wa, wx,
      ba.reshape(2, 1, width), bx.reshape(2, 1, width), lam.reshape(2, 1, width))


def _dn_gate_kernel(ab_ref, alog_ref, dtb_ref, o_ref, *, heads):
    x = ab_ref[0]
    tm = x.shape[0]
    lane = lax.broadcasted_iota(jnp.int32, x.shape, 1)
    g = -jnp.exp(alog_ref[...]) * _softplus(x + dtb_ref[...])
    ii = lax.broadcasted_iota(jnp.int32, (tm, tm), 0)
    jj = lax.broadcasted_iota(jnp.int32, (tm, tm), 1)
    same = (ii // DN_CHUNK) == (jj // DN_CHUNK)
    lower = jnp.where(same & (jj <= ii), 1.0, 0.0).astype(F32)
    upper = jnp.where(same & (jj >= ii), 1.0, 0.0).astype(F32)
    pre = jnp.dot(lower, g, preferred_element_type=F32, precision=lax.Precision.HIGHEST)
    suf = jnp.dot(upper, g, preferred_element_type=F32, precision=lax.Precision.HIGHEST)
    gc = jnp.where(lane < heads, pre, suf)
    o_ref[0] = jnp.where(lane < 2 * heads, gc, jax.nn.sigmoid(x))


def _dn_gate_call(ab, a_log, dt_bias):
    b, l, _ = ab.shape
    tm = min(l, 256)
    pad = lambda t: jnp.pad(t.reshape(1, -1), ((0, 0), (0, LANES - t.size)))
    return pl.pallas_call(
        functools.partial(_dn_gate_kernel, heads=a_log.shape[1]),
        grid=(b, l // tm),
        in_specs=[pl.BlockSpec((1, tm, LANES), lambda bi, i: (bi, i, 0)),
                  pl.BlockSpec((1, LANES), lambda bi, i: (0, 0)),
                  pl.BlockSpec((1, LANES), lambda bi, i: (0, 0))],
        out_specs=pl.BlockSpec((1, tm, LANES), lambda bi, i: (bi, i, 0)),
        out_shape=jax.ShapeDtypeStruct((b, l, LANES), F32),
        compiler_params=_params("parallel", "parallel"),
        name="deltanet_gates",
    )(ab, pad(a_log), pad(dt_bias))


def _dn_gate_layouts(gates, heads):
    b, l, _ = gates.shape
    t = gates[:, :, :4 * heads].reshape(b, l, 2, 2, heads)
    col = jnp.transpose(t, (0, 4, 1, 2, 3)).reshape(b, heads, l, 4)
    row = jnp.transpose(col.reshape(b, heads, l // DN_CHUNK, DN_CHUNK, 4), (0, 1, 2, 4, 3))
    return col, row


def _dn_kernel(ql_ref, kl_ref, vl_ref, gl_ref, qc_ref, kc_ref, vc_ref, gc_ref,
               cwq_ref, cwk_ref, cwv_ref, coll_ref, rowl_ref, colc_ref, rowc_ref, ng_ref,
               ol_ref, oc_ref,
               xpad, qs, ks, vs, oacc, pbuf, xbuf, atb, abuf, bbuf, qpbuf, egl, *, group):
    n_lat, n_ctx = ql_ref.shape[1], qc_ref.shape[1]
    c = DN_CHUNK
    tbs = 256

    def prep(src_ref, n_rows, off, cw_ref, dst, mode):
        def emit(r0, y):
            y = _silu(y)
            if mode != "v":
                y = y * lax.rsqrt(jnp.sum(y * y, axis=-1, keepdims=True) + EPS)
            if mode == "q":
                y = y * (DN_DK ** -0.5)
            dst[off + r0:off + r0 + y.shape[0], :] = y
        _conv_rows(src_ref, n_rows, xpad, cw_ref[...], None, emit, tbs)

    for src_c, src_l, cw_ref, dst, mode in ((qc_ref, ql_ref, cwq_ref, qs, "q"),
                                            (kc_ref, kl_ref, cwk_ref, ks, "k"),
                                            (vc_ref, vl_ref, cwv_ref, vs, "v")):
        prep(src_c, n_ctx, 0, cw_ref, dst, mode)
        prep(src_l, n_lat, n_ctx, cw_ref, dst, mode)

    ii = lax.broadcasted_iota(jnp.int32, (c, c), 0)
    jj = lax.broadcasted_iota(jnp.int32, (c, c), 1)
    n_apply = int(math.log2(c))
    segments = ((0, n_ctx, colc_ref, rowc_ref), (n_ctx, n_lat, coll_ref, rowl_ref))

    def phase1_group(it, grp, off, col_ref, row_ref):
        def chunk_ids(g):
            ci = it * grp + g
            return ci, pl.multiple_of(ci * c, c), pl.multiple_of(off + ci * c, c), off // c + ci

        def gates(col_ref, lr0, d):
            gcb = col_ref[0, 0, pl.ds(lr0, c), :]
            return gcb[:, d:d + 1], gcb[:, 2 + d:3 + d]

        for g in range(grp):
            ci, lr0, r0, _ = chunk_ids(g)
            q = qs[pl.ds(r0, c), :]
            k = ks[pl.ds(r0, c), :]
            v = vs[pl.ds(r0, c), :]
            kbf = k.astype(BF16)
            kk = _dot_nt(kbf, kbf)
            qk = _dot_nt(q.astype(BF16), kbf)
            rows4 = row_ref[0, 0, ci]
            for d in range(2):
                causal = (ii >= jj) if d == 0 else (ii <= jj)
                strict = (ii > jj) if d == 0 else (ii < jj)
                gc_col, beta = gates(col_ref, lr0, d)
                gc_row = rows4[d:d + 1, :]
                gam = jnp.where(causal, jnp.exp(jnp.where(causal, gc_col - gc_row, 0.0)), 0.0)
                pbuf[0, 2 * g + d] = jnp.where(strict, -(beta * kk) * gam, 0.0).astype(BF16)
                xbuf[2 * g + d] = jnp.concatenate([v * beta, (k * beta) * jnp.exp(gc_col)], axis=1)
                atb[2 * g + d] = (qk * gam).astype(BF16)

        for j in range(1, n_apply):
            for ch in range(2 * grp):
                pb = pbuf[j - 1, ch]
                pbuf[j, ch] = _dot(pb, pb).astype(BF16)
        for j in reversed(range(n_apply)):
            for ch in range(2 * grp):
                xv = xbuf[ch]
                xbuf[ch] = xv + _dot(pbuf[j, ch], xv.astype(BF16))

        for g in range(grp):
            ci, lr0, r0, cg = chunk_ids(g)
            q = qs[pl.ds(r0, c), :]
            k = ks[pl.ds(r0, c), :]
            oloc = None
            for d in range(2):
                gc_col, _ = gates(col_ref, lr0, d)
                g_last = gc_col[c - 1:c, :] if d == 0 else gc_col[0:1, :]
                kdt = (k * jnp.exp(g_last - gc_col)).T.astype(BF16)
                xv = xbuf[2 * g + d]
                ub = xv[:, 0:LANES].astype(BF16)
                wb = xv[:, LANES:2 * LANES].astype(BF16)
                at = atb[2 * g + d]
                abuf[d, cg] = _dot(kdt, wb).astype(BF16)
                bbuf[d, cg] = _dot(kdt, ub).astype(BF16)
                qpbuf[d, pl.ds(r0, c), :] = (q * jnp.exp(gc_col) - _dot(at, wb)).astype(BF16)
                part = _dot(at, ub)
                oloc = part if oloc is None else oloc + part
                egl[d, pl.ds(cg, 1), :] = jnp.broadcast_to(jnp.exp(g_last), (1, LANES))
            oacc[pl.ds(r0, c), :] = oloc

    for off, n_rows, col_ref, row_ref in segments:
        nch = n_rows // c
        grp = min(group, nch)

        def p1_body(it, carry, off=off, col_ref=col_ref, row_ref=row_ref, grp=grp):
            phase1_group(it, grp, off, col_ref, row_ref)
            return carry
        lax.fori_loop(0, nch // grp, p1_body, 0)

    def phase2(d, ci, state, off):
        r0 = pl.multiple_of(off + ci * c, c)
        cg = off // c + ci
        sb = state.astype(BF16)
        oacc[pl.ds(r0, c), :] += _dot(qpbuf[d, pl.ds(r0, c), :], sb)
        return state * egl[d, pl.ds(cg, 1), :] - _dot(abuf[d, cg], sb) + bbuf[d, cg].astype(F32)

    states = (jnp.zeros((DN_DK, LANES), F32), jnp.zeros((DN_DK, LANES), F32))
    for off, n_rows, _, _ in segments:
        nch = n_rows // c

        def p2_body(i, st, off=off, nch=nch):
            return (phase2(0, i, st[0], off), phase2(1, nch - 1 - i, st[1], off))
        states = lax.fori_loop(0, nch, p2_body, states)

    ng = ng_ref[...]
    for off, n_rows, g_ref, o_ref in ((0, n_ctx, gc_ref, oc_ref), (n_ctx, n_lat, gl_ref, ol_ref)):
        t = min(tbs, n_rows)
        for r0 in range(0, n_rows, t):
            o = oacc[off + r0:off + r0 + t, :]
            y = o * lax.rsqrt(jnp.mean(o * o, axis=-1, keepdims=True) + EPS) * ng
            o_ref[0, r0:r0 + t, :] = (y * _silu(g_ref[0, r0:r0 + t, :].astype(F32))).astype(BF16)


def _dn_call(main_l, main_c, col_l, row_l, col_c, row_c, conv_w, norm_g, col0):
    b, l, _ = main_l.shape
    lc = main_c.shape[1]
    heads = col_l.shape[1]
    lt = l + lc
    nch = lt // DN_CHUNK
    cwt = conv_w.T
    group = 8

    def blk(n_rows, which):
        return pl.BlockSpec((1, n_rows, LANES), lambda bi, h, which=which: (bi, 0, col0 + which * heads + h))

    def cw(which):
        return pl.BlockSpec((SHORT_CONV, LANES), lambda bi, h, which=which: (0, which * heads + h))

    in_specs = ([blk(l, w) for w in range(4)] + [blk(lc, w) for w in range(4)] + [cw(0), cw(1), cw(2)] +
                [pl.BlockSpec((1, 1, l, 4), lambda bi, h: (bi, h, 0, 0)),
                 pl.BlockSpec((1, 1, l // DN_CHUNK, 4, DN_CHUNK), lambda bi, h: (bi, h, 0, 0, 0)),
                 pl.BlockSpec((1, 1, lc, 4), lambda bi, h: (bi, h, 0, 0)),
                 pl.BlockSpec((1, 1, lc // DN_CHUNK, 4, DN_CHUNK), lambda bi, h: (bi, h, 0, 0, 0)),
                 pl.BlockSpec((1, LANES), lambda bi, h: (0, 0))])
    return pl.pallas_call(
        functools.partial(_dn_kernel, group=group),
        grid=(b, heads),
        in_specs=in_specs,
        out_specs=[pl.BlockSpec((1, l, LANES), lambda bi, h: (bi, 0, h)),
                   pl.BlockSpec((1, lc, LANES), lambda bi, h: (bi, 0, h))],
        out_shape=[jax.ShapeDtypeStruct((b, l, heads * LANES), BF16),
                   jax.ShapeDtypeStruct((b, lc, heads * LANES), BF16)],
        scratch_shapes=[pltpu.VMEM((l + 2 * SUBLANES, LANES), F32),
                        pltpu.VMEM((lt, LANES), F32),
                        pltpu.VMEM((lt, LANES), F32),
                        pltpu.VMEM((lt, LANES), F32),
                        pltpu.VMEM((lt, LANES), F32),
                        pltpu.VMEM((6, 2 * group, DN_CHUNK, DN_CHUNK), BF16),
                        pltpu.VMEM((2 * group, DN_CHUNK, 2 * LANES), F32),
                        pltpu.VMEM((2 * group, DN_CHUNK, DN_CHUNK), BF16),
                        pltpu.VMEM((2, nch, DN_DK, LANES), BF16),
                        pltpu.VMEM((2, nch, DN_DK, LANES), BF16),
                        pltpu.VMEM((2, lt, LANES), BF16),
                        pltpu.VMEM((2, nch, LANES), F32)],
        compiler_params=_params("parallel", "parallel"),
        name="gated_deltanet",
    )(main_l, main_l, main_l, main_l, main_c, main_c, main_c, main_c, cwt, cwt, cwt,
      col_l, row_l, col_c, row_c, norm_g.reshape(1, LANES))


def _ret_kernel(th_ref, ql_ref, kl_ref, vl_ref, qc_ref, kc_ref, vc_ref, cos_ref, sin_ref, o_ref,
                qs, kts, s_ref, dec_ref, xz_ref):
    n_lat, n_ctx = ql_ref.shape[1], qc_ref.shape[1]
    c = RET_CHUNK
    half = RET_DK // 2
    ncc, nlc = n_ctx // c, n_lat // c
    scale = RET_DK ** -0.5

    icol = lax.broadcasted_iota(jnp.int32, (c, RET_DV), 0)
    ii = lax.broadcasted_iota(jnp.int32, (c, c), 0)
    jj = lax.broadcasted_iota(jnp.int32, (c, c), 1)
    gch = []
    for d in range(2):
        lg = -jnp.exp(th_ref[d, 0])
        lg1 = lg[:, 0:1]
        fidx = (icol if d == 0 else c - 1 - icol).astype(F32)
        rel = (ii - jj) if d == 0 else (jj - ii)
        mask = rel >= 0
        dec_ref[d] = jnp.where(mask, jnp.exp(jnp.where(mask, rel, 0).astype(F32) * lg), 0.0)
        xz_ref[d, 0] = jnp.exp((fidx + 1.0) * lg1)
        xz_ref[d, 1] = jnp.exp((c - 1.0 - fidx) * lg1)
        gch.append(jnp.exp(c * lg1))
    s_ref[...] = jnp.zeros_like(s_ref)

    def prep_ctx(ci, carry):
        r0 = pl.multiple_of(ci * c, c)
        qs[pl.ds(r0, c), :] = qc_ref[0, pl.ds(r0, c), :].astype(BF16)
        k = kc_ref[0, pl.ds(r0, c), :].astype(F32) * scale
        kts[ci] = k.T.astype(BF16)
        return carry

    def prep_lat(ci, carry):
        r0 = pl.multiple_of(ci * c, c)
        cos = cos_ref[pl.ds(r0, c), :]
        sin = sin_ref[pl.ds(r0, c), :]

        def rope(t):
            t1, t2 = t[:, 0:half], t[:, half:2 * half]
            return jnp.concatenate([t1 * cos - t2 * sin, t1 * sin + t2 * cos], axis=-1)

        q = rope(ql_ref[0, pl.ds(r0, c), :].astype(F32))
        k = rope(kl_ref[0, pl.ds(r0, c), :].astype(F32)) * scale
        qs[pl.ds(pl.multiple_of(n_ctx + ci * c, c), c), :] = q.astype(BF16)
        kts[ncc + ci] = k.T.astype(BF16)
        return carry

    lax.fori_loop(0, ncc, prep_ctx, 0)
    lax.fori_loop(0, nlc, prep_lat, 0)

    def one(d, ci, row_off, ch_off, v_ref, write, first):
        r0 = pl.multiple_of(ci * c, c)
        qb = qs[pl.ds(pl.multiple_of(row_off + ci * c, c), c), :]
        kt = kts[ch_off + ci]
        v = v_ref[0, pl.ds(r0, c), :].astype(BF16)
        state = s_ref[d]
        if write:
            inner = _dot(qb, kt) * dec_ref[d]
            o = _dot(inner.astype(BF16), v) + xz_ref[d, 0] * _dot(qb, state.astype(BF16))
            if first:
                o_ref[0, pl.ds(r0, c), :] = o
            else:
                o_ref[0, pl.ds(r0, c), :] += o
        vz = (v.astype(F32) * xz_ref[d, 1]).astype(BF16)
        s_ref[d] = state * gch[d] + _dot(kt, vz)

    def ctx_body(i, carry):
        one(0, i, 0, 0, vc_ref, False, False)
        one(1, ncc - 1 - i, 0, 0, vc_ref, False, False)
        return carry

    lax.fori_loop(0, ncc, ctx_body, 0)
    for lo, hi, first in ((0, nlc // 2, True), (nlc // 2, nlc, False)):
        def lat_body(i, carry, first=first):
            one(0, i, n_ctx, ncc, vl_ref, True, first)
            one(1, nlc - 1 - i, n_ctx, ncc, vl_ref, True, first)
            return carry
        lax.fori_loop(lo, hi, lat_body, 0)


def _ret_call(main_l, main_c, theta, cos, sin, q_col0):
    b, l, _ = main_l.shape
    lc = main_c.shape[1]
    heads = theta.shape[1]
    c = RET_CHUNK
    qb0 = q_col0 // RET_DK
    kb0 = qb0 + heads
    vb0 = (q_col0 + 2 * heads * RET_DK) // RET_DV
    th = jnp.broadcast_to(theta.reshape(2, heads, 1, 1), (2, heads, 1, LANES))
    return pl.pallas_call(
        _ret_kernel,
        grid=(b, heads),
        in_specs=[pl.BlockSpec((2, 1, 1, LANES), lambda bi, h: (0, h, 0, 0)),
                  pl.BlockSpec((1, l, RET_DK), lambda bi, h: (bi, 0, qb0 + h)),
                  pl.BlockSpec((1, l, RET_DK), lambda bi, h: (bi, 0, kb0 + h)),
                  pl.BlockSpec((1, l, RET_DV), lambda bi, h: (bi, 0, vb0 + h)),
                  pl.BlockSpec((1, lc, RET_DK), lambda bi, h: (bi, 0, qb0 + h)),
                  pl.BlockSpec((1, lc, RET_DK), lambda bi, h: (bi, 0, kb0 + h)),
                  pl.BlockSpec((1, lc, RET_DV), lambda bi, h: (bi, 0, vb0 + h)),
                  pl.BlockSpec((l, RET_DK // 2), lambda bi, h: (0, 0)),
                  pl.BlockSpec((l, RET_DK // 2), lambda bi, h: (0, 0))],
        out_specs=pl.BlockSpec((1, l, RET_DV), lambda bi, h: (bi, 0, h)),
        out_shape=jax.ShapeDtypeStruct((b, l, heads * RET_DV), F32),
        scratch_shapes=[pltpu.VMEM((lc + l, RET_DK), BF16),
                        pltpu.VMEM(((lc + l) // c, RET_DK, c), BF16),
                        pltpu.VMEM((2, RET_DK, RET_DV), F32),
                        pltpu.VMEM((2, c, c), F32),
                        pltpu.VMEM((2, 2, c, RET_DV), F32)],
        compiler_params=_params("parallel", "parallel"),
        name="retention",
    )(th, main_l, main_l, main_l, main_c, main_c, main_c, cos, sin)


def _ret_finish_kernel(o_ref, r_ref, y_ref):
    o = o_ref[0]
    y = o * lax.rsqrt(jnp.mean(o * o, axis=-1, keepdims=True) + EPS)
    y_ref[0] = (y * _silu(r_ref[0].astype(F32))).astype(BF16)


def _ret_finish_call(o, main_l, r_col0):
    b, l, w = o.shape
    heads = w // RET_DV
    tm = min(l, 1024)
    rb0 = r_col0 // RET_DV
    return pl.pallas_call(
        _ret_finish_kernel,
        grid=(b, l // tm, heads),
        in_specs=[pl.BlockSpec((1, tm, RET_DV), lambda bi, i, h: (bi, i, h)),
                  pl.BlockSpec((1, tm, RET_DV), lambda bi, i, h: (bi, i, rb0 + h))],
        out_specs=pl.BlockSpec((1, tm, RET_DV), lambda bi, i, h: (bi, i, h)),
        out_shape=jax.ShapeDtypeStruct((b, l, w), BF16),
        compiler_params=_params("parallel", "parallel", "parallel"),
        name="retention_finish",
    )(o, main_l)


def _rope_tables(n_tokens):
    rows = n_tokens // GRID_W
    r, col = jnp.meshgrid(jnp.arange(rows), jnp.arange(GRID_W), indexing='ij')
    n_freq = RET_DK // 4
    inv = ROPE_BASE ** (-jnp.arange(n_freq, dtype=F32) / n_freq)
    ang = jnp.concatenate([r.reshape(-1, 1) * inv, col.reshape(-1, 1) * inv], axis=-1)
    return jnp.cos(ang), jnp.sin(ang)


def _even_layer(xl, xc, mods_l, mods_c, norm1_g, norm2_g, w_in, w_out, s5p, glu_w, glu_b,
                dn_conv_w, dn_a_log, dn_dt_bias, dn_norm_g, ffn, final_g, final_norm):
    s5_width = glu_w.shape[0]
    dn_width = dn_conv_w.shape[0] // 3
    heads = dn_width // LANES
    n_main = s5_width + 4 * dn_width
    w_main = w_in[:, :n_main].astype(BF16)
    w_small = jnp.pad(w_in[:, n_main:], ((0, 0), (0, LANES - (w_in.shape[1] - n_main)))).astype(BF16)

    main_l, ab_l = _inproj_call(xl, norm1_g, mods_l[0], mods_l[1], w_main, w_small)
    main_c, ab_c = _inproj_call(xc, norm1_g, mods_c[0], mods_c[1], w_main, w_small)

    y_l, y_c = _s5_scan_call(main_l, main_c, *s5p)
    glu_wb = glu_w.astype(BF16)
    s5_l = _s5_glu_call(y_l, glu_wb, glu_b)
    s5_c = _s5_glu_call(y_c, glu_wb, glu_b)

    col_l, row_l = _dn_gate_layouts(_dn_gate_call(ab_l, dn_a_log, dn_dt_bias), heads)
    col_c, row_c = _dn_gate_layouts(_dn_gate_call(ab_c, dn_a_log, dn_dt_bias), heads)
    dn_l, dn_c = _dn_call(main_l, main_c, col_l, row_l, col_c, row_c, dn_conv_w, dn_norm_g,
                          s5_width // LANES)

    w_o1 = w_out[:s5_width].astype(BF16)
    w_o2 = w_out[s5_width:].astype(BF16)
    xl = _outproj_call(xl, mods_l[2], s5_l, dn_l, w_o1, w_o2)
    xc = _outproj_call(xc, mods_c[2], s5_c, dn_c, w_o1, w_o2)
    w1, w3, w2 = ffn
    xl = _ffn_call(xl, norm2_g, mods_l[3], mods_l[4], mods_l[5], w1, w3, w2, final_g, final_norm)
    xc = _ffn_call(xc, norm2_g, mods_c[3], mods_c[4], mods_c[5], w1, w3, w2, final_g, False)
    return xl, xc


def _odd_layer(xl, xc, mods_l, mods_c, norm1_g, norm2_g, w_in, w_out, conv_w, conv_b, wa, ba, wx, bx, lam,
               theta, cos, sin, ffn, final_g, final_norm):
    lru_width = conv_b.shape[0]
    heads = theta.shape[1]
    w_inb = w_in.astype(BF16)
    main_l = _inproj_call(xl, norm1_g, mods_l[0], mods_l[1], w_inb)
    main_c = _inproj_call(xc, norm1_g, mods_c[0], mods_c[1], w_inb)

    lru_l = _lru_call(main_l, main_c, conv_w, conv_b, wa.astype(BF16), ba, wx.astype(BF16), bx, lam)
    q_col0 = 2 * lru_width
    o = _ret_call(main_l, main_c, theta, cos, sin, q_col0)
    ret_l = _ret_finish_call(o, main_l, q_col0 + 2 * heads * RET_DK + heads * RET_DV)

    w_o1 = w_out[:lru_width].astype(BF16)
    w_o2 = w_out[lru_width:].astype(BF16)
    xl = _outproj_call(xl, mods_l[2], lru_l, ret_l, w_o1, w_o2)
    w1, w3, w2 = ffn
    return _ffn_call(xl, norm2_g, mods_l[3], mods_l[4], mods_l[5], w1, w3, w2, final_g, final_norm)


def kernel(x, c, ctx, c_ctx, mod_w, mod_b, norm1_g, norm2_g, ffn_w1, ffn_w3, ffn_w2, final_g, even_w_in, even_w_out, s5_lam_re, s5_lam_im, s5_log_step, s5_b_re, s5_b_im, s5_c_re, s5_c_im, s5_d, s5_glu_w, s5_glu_b, dn_conv_w, dn_a_log, dn_dt_bias, dn_norm_g, odd_w_in, odd_w_out, lru_conv_w, lru_conv_b, lru_wa, lru_ba, lru_wx, lru_bx, lru_lam, ret_theta):
    bsz, n_tok, d = x.shape
    depth = mod_w.shape[0]
    assert depth == 2 and bsz + 1 <= SUBLANES
    cos, sin = _rope_tables(n_tok)

    rows = jnp.concatenate([c, c_ctx[None, :], jnp.zeros((SUBLANES - bsz - 1, d), F32)], axis=0)
    mods = _mod_call(rows, mod_w, mod_b)

    def split_mods(i):
        m = mods[i].reshape(SUBLANES, 6, d)
        ml = [m[:bsz, k][:, None, :] for k in range(6)]
        mc = [jnp.broadcast_to(m[bsz, k][None, None, :], (bsz, 1, d)) for k in range(6)]
        return ml, mc

    xl, xc = x, ctx
    ml, mc = split_mods(0)
    s5p = _s5_weights(s5_lam_re[0], s5_lam_im[0], s5_log_step[0], s5_b_re[0], s5_b_im[0],
                      s5_c_re[0], s5_c_im[0]) + (s5_d[0],)
    ffn0 = (ffn_w1[0].astype(BF16), ffn_w3[0].astype(BF16), ffn_w2[0].astype(BF16))
    xl, xc = _even_layer(xl, xc, ml, mc, norm1_g[0], norm2_g[0], even_w_in[0], even_w_out[0],
                         s5p, s5_glu_w[0], s5_glu_b[0],
                         dn_conv_w[0], dn_a_log[0], dn_dt_bias[0], dn_norm_g[0], ffn0, final_g, False)
    ml, mc = split_mods(1)
    ffn1 = (ffn_w1[1].astype(BF16), ffn_w3[1].astype(BF16), ffn_w2[1].astype(BF16))
    return _odd_layer(xl, xc, ml, mc, norm1_g[1], norm2_g[1], odd_w_in[0], odd_w_out[0],
                      lru_conv_w[0], lru_conv_b[0], lru_wa[0], lru_ba[0], lru_wx[0], lru_bx[0], lru_lam[0],
                      ret_theta[0], cos, sin, ffn1, final_g, True)
```

```python
import functools
import math

import jax
import jax.numpy as jnp
from jax import lax
from jax.experimental import pallas as pl
from jax.experimental.pallas import tpu as pltpu

F32 = jnp.float32
BF16 = jnp.bfloat16

EPS = 1e-6
GRID_W = 64
ROPE_BASE = 10000.0

S5_GROUP = 16
S5_STATE = 64
S5_BLOCK = 8

DN_DK = 128
DN_CHUNK = 64
SHORT_CONV = 4

LRU_BLOCK = 128
LRU_C = 8.0

RET_DK = 256
RET_DV = 512
RET_CHUNK = 128

LANES = 128
SUBLANES = 8
VMEM_LIMIT = 56 * 1024 * 1024


def _params(*sem):
    return pltpu.CompilerParams(dimension_semantics=sem, vmem_limit_bytes=VMEM_LIMIT)


def _silu(x):
    return x * jax.nn.sigmoid(x)


def _gelu(x):
    return 0.5 * x * (1.0 + jnp.tanh(0.7978845608028654 * (x + 0.044715 * (x * x * x))))


def _softplus(x):
    return jnp.maximum(x, 0.0) + jnp.log1p(jnp.exp(-jnp.abs(x)))


def _sigmoid(x):
    return 0.5 * (1.0 + jnp.tanh(0.5 * x))


def _expm1_2x(x):
    t = jnp.tanh(x)
    return 2.0 * t / (1.0 - t)


def _adaln(x, g, shift, scale):
    y = x * lax.rsqrt(jnp.mean(x * x, axis=-1, keepdims=True) + EPS)
    return (y * g) * (1.0 + scale) + shift


def _dot(a, b):
    return jnp.dot(a, b, preferred_element_type=F32)


def _dot_nt(a, b):
    return lax.dot_general(a, b, (((1,), (1,)), ((), ())), preferred_element_type=F32)


def _mod_kernel(s_ref, w_ref, b_ref, o_ref):
    s = _silu(s_ref[...])
    o_ref[0] = _dot(s.astype(BF16), w_ref[0].astype(BF16)) + b_ref[0]


def _mod_call(rows, mod_w, mod_b):
    depth, d, n = mod_w.shape
    tn = 1024
    return pl.pallas_call(
        _mod_kernel,
        grid=(depth, n // tn),
        in_specs=[pl.BlockSpec((SUBLANES, d), lambda i, j: (0, 0)),
                  pl.BlockSpec((1, d, tn), lambda i, j: (i, 0, j)),
                  pl.BlockSpec((1, 1, tn), lambda i, j: (i, 0, j))],
        out_specs=pl.BlockSpec((1, SUBLANES, tn), lambda i, j: (i, 0, j)),
        out_shape=jax.ShapeDtypeStruct((depth, SUBLANES, n), F32),
        compiler_params=_params("parallel", "parallel"),
        name="mod_proj",
    )(rows, mod_w, mod_b.reshape(depth, 1, n))


def _inproj_kernel(x_ref, g_ref, sh_ref, sc_ref, w_ref, *rest, has_small):
    if has_small:
        ws_ref, o_ref, os_ref, h_ref = rest
    else:
        o_ref, h_ref = rest

    @pl.when(pl.program_id(2) == 0)
    def _():
        h = _adaln(x_ref[0], g_ref[...], sh_ref[0], sc_ref[0]).astype(BF16)
        h_ref[...] = h
        if has_small:
            os_ref[0] = _dot(h, ws_ref[...])

    o_ref[0] = _dot(h_ref[...], w_ref[...]).astype(o_ref.dtype)


def _inproj_call(x, g, shift, scale, w, w_small=None):
    b, l, d = x.shape
    n = w.shape[1]
    tm = min(l, 1024)
    tn = 512
    has_small = w_small is not None
    in_specs = [pl.BlockSpec((1, tm, d), lambda bi, i, j: (bi, i, 0)),
                pl.BlockSpec((1, d), lambda bi, i, j: (0, 0)),
                pl.BlockSpec((1, 1, d), lambda bi, i, j: (bi, 0, 0)),
                pl.BlockSpec((1, 1, d), lambda bi, i, j: (bi, 0, 0)),
                pl.BlockSpec((d, tn), lambda bi, i, j: (0, j))]
    out_specs = [pl.BlockSpec((1, tm, tn), lambda bi, i, j: (bi, i, j))]
    out_shape = [jax.ShapeDtypeStruct((b, l, n), BF16)]
    args = [x, g.reshape(1, d), shift, scale, w]
    if has_small:
        in_specs.append(pl.BlockSpec((d, LANES), lambda bi, i, j: (0, 0)))
        out_specs.append(pl.BlockSpec((1, tm, LANES), lambda bi, i, j: (bi, i, 0)))
        out_shape.append(jax.ShapeDtypeStruct((b, l, LANES), F32))
        args.append(w_small)
    outs = pl.pallas_call(
        functools.partial(_inproj_kernel, has_small=has_small),
        grid=(b, l // tm, n // tn),
        in_specs=in_specs, out_specs=out_specs, out_shape=out_shape,
        scratch_shapes=[pltpu.VMEM((tm, d), BF16)],
        compiler_params=_params("parallel", "parallel", "arbitrary"),
        name="adaln_inproj",
    )(*args)
    return outs if has_small else outs[0]


def _outproj_kernel(x_ref, gate_ref, a1_ref, a2_ref, w1_ref, w2_ref, o_ref):
    y = _dot(a1_ref[0], w1_ref[...]) + _dot(a2_ref[0], w2_ref[...])
    o_ref[0] = x_ref[0] + gate_ref[0] * y


def _outproj_call(x, gate, a1, a2, w1, w2):
    b, l, d = x.shape
    k1, k2 = a1.shape[2], a2.shape[2]
    tm = min(l, 1024)
    tn = 512
    return pl.pallas_call(
        _outproj_kernel,
        grid=(b, l // tm, d // tn),
        in_specs=[pl.BlockSpec((1, tm, tn), lambda bi, i, j: (bi, i, j)),
                  pl.BlockSpec((1, 1, tn), lambda bi, i, j: (bi, 0, j)),
                  pl.BlockSpec((1, tm, k1), lambda bi, i, j: (bi, i, 0)),
                  pl.BlockSpec((1, tm, k2), lambda bi, i, j: (bi, i, 0)),
                  pl.BlockSpec((k1, tn), lambda bi, i, j: (0, j)),
                  pl.BlockSpec((k2, tn), lambda bi, i, j: (0, j))],
        out_specs=pl.BlockSpec((1, tm, tn), lambda bi, i, j: (bi, i, j)),
        out_shape=jax.ShapeDtypeStruct((b, l, d), F32),
        compiler_params=_params("parallel", "parallel", "arbitrary"),
        name="outproj_residual",
    )(x, gate, a1, a2, w1, w2)


def _ffn_kernel(x_ref, g_ref, sh_ref, sc_ref, gate_ref, w1_ref, w3_ref, w2_ref, fg_ref, o_ref,
                h_ref, acc_ref, *, final_norm):
    f = pl.program_id(2)

    @pl.when(f == 0)
    def _():
        h_ref[...] = _adaln(x_ref[0], g_ref[...], sh_ref[0], sc_ref[0]).astype(BF16)
        acc_ref[...] = jnp.zeros_like(acc_ref)

    h = h_ref[...]
    a = _dot(h, w1_ref[...])
    b = _dot(h, w3_ref[...])
    acc_ref[...] += _dot((_silu(a) * b).astype(BF16), w2_ref[...])

    @pl.when(f == pl.num_programs(2) - 1)
    def _():
        y = x_ref[0] + gate_ref[0] * acc_ref[...]
        if final_norm:
            y = y * lax.rsqrt(jnp.mean(y * y, axis=-1, keepdims=True) + EPS) * fg_ref[...]
        o_ref[0] = y


def _ffn_call(x, g, shift, scale, gate, w1, w3, w2, final_g, final_norm):
    b, l, d = x.shape
    dff = w1.shape[1]
    tm = min(l, 512)
    tf = 512
    vec = pl.BlockSpec((1, 1, d), lambda bi, i, f: (bi, 0, 0))
    return pl.pallas_call(
        functools.partial(_ffn_kernel, final_norm=final_norm),
        grid=(b, l // tm, dff // tf),
        in_specs=[pl.BlockSpec((1, tm, d), lambda bi, i, f: (bi, i, 0)),
                  pl.BlockSpec((1, d), lambda bi, i, f: (0, 0)),
                  vec, vec, vec,
                  pl.BlockSpec((d, tf), lambda bi, i, f: (0, f)),
                  pl.BlockSpec((d, tf), lambda bi, i, f: (0, f)),
                  pl.BlockSpec((tf, d), lambda bi, i, f: (f, 0)),
                  pl.BlockSpec((1, d), lambda bi, i, f: (0, 0))],
        out_specs=pl.BlockSpec((1, tm, d), lambda bi, i, f: (bi, i, 0)),
        out_shape=jax.ShapeDtypeStruct((b, l, d), F32),
        scratch_shapes=[pltpu.VMEM((tm, d), BF16), pltpu.VMEM((tm, d), F32)],
        compiler_params=_params("parallel", "parallel", "arbitrary"),
        name="ffn_swiglu",
    )(x, g.reshape(1, d), shift, scale, gate, w1, w3, w2, final_g.reshape(1, d))


def _s5_disc_kernel(lre_ref, lim_ref, ls_ref, bre_ref, bim_ref, are_ref, aim_ref, bbre_ref, bbim_ref):
    lre = jnp.minimum(lre_ref[...], -1e-4)
    lim = lim_ref[...]
    dt = jnp.exp(ls_ref[...])
    mag = jnp.exp(lre * dt)
    ar = mag * jnp.cos(lim * dt)
    ai = mag * jnp.sin(lim * dt)
    nr, ni = ar - 1.0, ai
    den = lre * lre + lim * lim
    cr = (nr * lre + ni * lim) / den
    ci = (ni * lre - nr * lim) / den
    bre, bim = bre_ref[...], bim_ref[...]
    are_ref[...] = ar
    aim_ref[...] = ai
    bbre_ref[...] = cr * bre - ci * bim
    bbim_ref[...] = cr * bim + ci * bre


def _s5_taps_kernel(ar_ref, ai_ref, br_ref, bi_ref, cr_ref, ci_ref, w1_ref, w2_ref, apw_ref, ccr, cci, ktb):
    t_blk = S5_BLOCK
    rows, p = ar_ref.shape[2], ar_ref.shape[3]
    s = S5_GROUP
    gs = rows // s
    tl = t_blk * rows
    hi = lax.Precision.HIGHEST
    nt = (((1,), (1,)), ((), ()))
    iota = lambda shape, ax: lax.broadcasted_iota(jnp.int32, shape, ax)
    same = (iota((rows, rows), 0) // s) == (iota((rows, rows), 1) // s)
    dmask = (iota((rows, gs * p), 0) // s) == (iota((rows, gs * p), 1) // p)
    gmask = (iota((gs * p, rows), 0) // p) == (iota((gs * p, rows), 1) // s)

    for d in range(2):
        ar, ai = ar_ref[d, 0], ai_ref[d, 0]
        br, bi = br_ref[d, 0], bi_ref[d, 0]
        cr, ci = cr_ref[d, 0], ci_ref[d, 0]
        pw = [(jnp.ones_like(ar), jnp.zeros_like(ar))]
        for _ in range(t_blk):
            pr, pi = pw[-1]
            pw.append((pr * ar - pi * ai, pr * ai + pi * ar))
        for t in range(t_blk):
            sl = slice(t * rows, (t + 1) * rows)
            pr, pi = pw[t]
            ccr[sl, :] = cr * pr - ci * pi
            cci[sl, :] = cr * pi + ci * pr
            dpr, dpi = pw[t_blk - 1 - t] if d == 0 else pw[t]
            for c, piece in enumerate((br * dpr - bi * dpi, br * dpi + bi * dpr)):
                wide = jnp.concatenate([piece] * gs, axis=1)
                c0 = tl + (2 * d + c) * gs * p
                w1_ref[0, sl, c0:c0 + gs * p] = jnp.where(dmask, wide, 0.0).astype(BF16)
            gpr, gpi = pw[t + 1] if d == 0 else pw[t_blk - t]
            for c, piece in enumerate((cr * gpr - ci * gpi, -(cr * gpi + ci * gpr))):
                tall = jnp.concatenate([piece.T] * gs, axis=0)
                r0 = (2 * d + c) * gs * p
                w2_ref[0, r0:r0 + gs * p, sl] = jnp.where(gmask, tall, 0.0).astype(BF16)
        ktb[d] = (lax.dot_general(br, ccr[...], nt, precision=hi, preferred_element_type=F32)
                  - lax.dot_general(bi, cci[...], nt, precision=hi, preferred_element_type=F32))
        qr, qi = pw[t_blk]
        er, ei = jnp.ones_like(qr), jnp.zeros_like(qr)
        for kk in range(9):
            apw_ref[0, d, 3 * kk] = er
            apw_ref[0, d, 3 * kk + 1] = ei
            apw_ref[0, d, 3 * kk + 2] = -ei
            er, ei = er * qr - ei * qi, er * qi + ei * qr

    for t_in in range(t_blk):
        for t_out in range(t_blk):
            if t_out >= t_in:
                lag = t_out - t_in
                piece = ktb[0, :, lag * rows:(lag + 1) * rows]
            if t_out <= t_in:
                lag = t_in - t_out
                back = ktb[1, :, lag * rows:(lag + 1) * rows]
                piece = back if t_out < t_in else piece + back
            w1_ref[0, t_in * rows:(t_in + 1) * rows, t_out * rows:(t_out + 1) * rows] = (
                jnp.where(same, piece, 0.0).astype(BF16))


def _s5_weights(lam_re, lam_im, log_step, b_re, b_im, c_re, c_im):
    nd, g, p = lam_re.shape
    s = b_re.shape[-1]
    t_blk = S5_BLOCK
    rows = nd * g * s
    rep = lambda t: jnp.repeat(t.reshape(nd * g, p), s, axis=0)
    ls = jnp.broadcast_to(log_step.reshape(nd * g, 1), (nd * g, p))
    tb = lambda t: jnp.transpose(t, (0, 1, 3, 2)).reshape(rows, p)
    shp = jax.ShapeDtypeStruct((rows, p), F32)
    are, aim, bbre, bbim = pl.pallas_call(
        _s5_disc_kernel, out_shape=[shp, shp, shp, shp], name="s5_discretise",
    )(rep(lam_re), rep(lam_im), rep(ls), tb(b_re), tb(b_im))
    gs = LANES // s
    nslab = g // gs
    tl = t_blk * LANES
    hw = 2 * gs * p
    mat = pl.BlockSpec((nd, 1, LANES, p), lambda i: (0, i, 0, 0))
    slab = lambda t: t.reshape(nd, nslab, LANES, p)
    w1, w2, apw = pl.pallas_call(
        _s5_taps_kernel,
        grid=(nslab,),
        in_specs=[mat] * 6,
        out_specs=[pl.BlockSpec((1, tl, tl + nd * hw), lambda i: (i, 0, 0)),
                   pl.BlockSpec((1, nd * hw, tl), lambda i: (i, 0, 0)),
                   pl.BlockSpec((1, nd, 27, LANES, p), lambda i: (i, 0, 0, 0, 0))],
        out_shape=[jax.ShapeDtypeStruct((nslab, tl, tl + nd * hw), BF16),
                   jax.ShapeDtypeStruct((nslab, nd * hw, tl), BF16),
                   jax.ShapeDtypeStruct((nslab, nd, 27, LANES, p), F32)],
        scratch_shapes=[pltpu.VMEM((tl, p), F32), pltpu.VMEM((tl, p), F32), pltpu.VMEM((nd, LANES, tl), F32)],
        compiler_params=_params("parallel"),
        name="s5_block_taps",
    )(slab(are), slab(aim), slab(bbre), slab(bbim), slab(c_re), slab(c_im))
    ap = apw.reshape(nslab, nd, 9, 3, gs, s, p)[:, :, :, :, :, 0, :].reshape(nslab, nd, 9, 3, gs * p)
    p1 = jnp.concatenate([ap[:, :, :, 0], ap[:, :, :, 0]], axis=-1)
    p2 = jnp.concatenate([ap[:, :, :, 2], ap[:, :, :, 1]], axis=-1)
    return w1, w2, jnp.stack([p1, p2], axis=3).reshape(nslab, nd, 18, hw)


def _s5_kernel(ul_ref, uc_ref, w1_ref, w2_ref, apw_ref, dsk_ref, yl_ref, yc_ref,
               uf, ubuf, zbuf, hbuf, sbuf):
    t_blk = S5_BLOCK
    n_lat, n_ctx = ul_ref.shape[1], uc_ref.shape[1]
    nb_ctx, nb_lat = n_ctx // t_blk, n_lat // t_blk
    nb = nb_ctx + nb_lat
    ts = t_blk * LANES
    hw = apw_ref.shape[3]
    rows = 512

    for src, off, n in ((uc_ref, 0, n_ctx), (ul_ref, n_ctx, n_lat)):
        for r0 in range(0, n, min(rows, n)):
            r1 = min(r0 + rows, n)
            uf[off + r0:off + r1, :] = src[0, r0:r1, :].astype(F32)
    for t in range(t_blk):
        ubuf[:, t * LANES:(t + 1) * LANES] = uf[pl.ds(t, nb, stride=t_blk), :].astype(BF16)
    zbuf[...] = _dot(ubuf[...], w1_ref[0])

    row = lax.broadcasted_iota(jnp.int32, (SUBLANES, hw), 0)

    def cmul(x, p1, p2):
        return x * p1 + pltpu.roll(x, hw // 2, 1) * p2

    for d in range(2):
        col = ts + d * hw
        pw = lambda k, d=d: (apw_ref[0, d, 2 * k:2 * k + 1, :], apw_ref[0, d, 2 * k + 1:2 * k + 2, :])

        def local(i, carry, d=d, col=col, pw=pw):
            r0 = pl.multiple_of(i * SUBLANES, SUBLANES)
            x = zbuf[pl.ds(r0, SUBLANES), col:col + hw]
            for k in (1, 2, 4):
                keep = (row >= k) if d == 0 else (row <= SUBLANES - 1 - k)
                sh = jnp.where(keep, pltpu.roll(x, k if d == 0 else SUBLANES - k, 0), 0.0)
                p1, p2 = pw(k)
                x = x + cmul(sh, p1, p2)
            sbuf[pl.ds(r0, SUBLANES), :] = x
            return carry
        lax.fori_loop(0, nb // SUBLANES, local, 0)

        pex1 = jnp.zeros((SUBLANES, hw), F32)
        pex2 = jnp.zeros((SUBLANES, hw), F32)
        for r in range(SUBLANES):
            p1, p2 = pw(r if d == 0 else SUBLANES - 1 - r)
            pex1 = jnp.where(row == r, p1, pex1)
            pex2 = jnp.where(row == r, p2, pex2)
        p81, p82 = pw(SUBLANES)
        keep1 = (row >= 1) if d == 0 else (row <= SUBLANES - 2)
        last = SUBLANES - 1 if d == 0 else 0
        carry = jnp.zeros((SUBLANES, hw), F32)
        for seg0, ntile in ((0, nb_ctx // SUBLANES), (nb_ctx // SUBLANES, nb_lat // SUBLANES)):
            def chain(i, carry, seg0=seg0, ntile=ntile, d=d):
                ti = seg0 + (i if d == 0 else ntile - 1 - i)
                r0 = pl.multiple_of(ti * SUBLANES, SUBLANES)
                s_t = sbuf[pl.ds(r0, SUBLANES), :]
                excl = jnp.where(keep1, pltpu.roll(s_t, 1 if d == 0 else SUBLANES - 1, 0), 0.0)
                hbuf[pl.ds(r0, SUBLANES), d * hw:(d + 1) * hw] = excl + cmul(carry, pex1, pex2)
                return jnp.broadcast_to(s_t[last:last + 1], (SUBLANES, hw)) + cmul(carry, p81, p82)
            carry = lax.fori_loop(0, ntile, chain, carry)

    zbuf[:, 0:ts] += _dot(hbuf[...].astype(BF16), w2_ref[0])
    dsk = dsk_ref[...]
    for t in range(t_blk):
        y_t = zbuf[:, t * LANES:(t + 1) * LANES] + dsk * uf[pl.ds(t, nb, stride=t_blk), :]
        yc_ref[0, pl.ds(t, nb_ctx, stride=t_blk), :] = y_t[0:nb_ctx]
        yl_ref[0, pl.ds(t, nb_lat, stride=t_blk), :] = y_t[nb_ctx:nb]


def _s5_scan_call(main_l, main_c, w1, w2, apw, d_skip):
    b, l, _ = main_l.shape
    lc = main_c.shape[1]
    nslab, ts, n1 = w1.shape
    hw = apw.shape[3]
    width = d_skip.shape[0]
    nb = (l + lc) // S5_BLOCK
    return pl.pallas_call(
        _s5_kernel,
        grid=(nslab, b),
        in_specs=[pl.BlockSpec((1, l, LANES), lambda s, bi: (bi, 0, s)),
                  pl.BlockSpec((1, lc, LANES), lambda s, bi: (bi, 0, s)),
                  pl.BlockSpec((1, ts, n1), lambda s, bi: (s, 0, 0)),
                  pl.BlockSpec((1, 2 * hw, ts), lambda s, bi: (s, 0, 0)),
                  pl.BlockSpec((1, 2, 18, hw), lambda s, bi: (s, 0, 0, 0)),
                  pl.BlockSpec((1, LANES), lambda s, bi: (0, s))],
        out_specs=[pl.BlockSpec((1, l, LANES), lambda s, bi: (bi, 0, s)),
                   pl.BlockSpec((1, lc, LANES), lambda s, bi: (bi, 0, s))],
        out_shape=[jax.ShapeDtypeStruct((b, l, width), F32),
                   jax.ShapeDtypeStruct((b, lc, width), F32)],
        scratch_shapes=[pltpu.VMEM((l + lc, LANES), F32),
                        pltpu.VMEM((nb, ts), BF16),
                        pltpu.VMEM((nb, n1), F32),
                        pltpu.VMEM((nb, 2 * hw), F32),
                        pltpu.VMEM((nb, hw), F32)],
        compiler_params=_params("parallel", "parallel"),
        name="s5_scan",
    )(main_l, main_c, w1, w2, apw, d_skip.reshape(1, width))


def _s5_glu_kernel(y_ref, w_ref, b_ref, o_ref):
    g = _gelu(y_ref[0])
    o_ref[0] = (g * jax.nn.sigmoid(_dot(g.astype(BF16), w_ref[...]) + b_ref[...])).astype(BF16)


def _s5_glu_call(y, glu_w, glu_b):
    b, l, w = y.shape
    tm = min(l, 1024)
    return pl.pallas_call(
        _s5_glu_kernel,
        grid=(b, l // tm),
        in_specs=[pl.BlockSpec((1, tm, w), lambda bi, i: (bi, i, 0)),
                  pl.BlockSpec((w, w), lambda bi, i: (0, 0)),
                  pl.BlockSpec((1, w), lambda bi, i: (0, 0))],
        out_specs=pl.BlockSpec((1, tm, w), lambda bi, i: (bi, i, 0)),
        out_shape=jax.ShapeDtypeStruct((b, l, w), BF16),
        compiler_params=_params("parallel", "parallel"),
        name="s5_glu",
    )(y, glu_w, glu_b.reshape(1, w))


def _conv_rows(src_ref, n_rows, xpad, cw, bias, emit, tbs):
    zeros = jnp.zeros((SUBLANES, LANES), F32)
    xpad[0:SUBLANES, :] = zeros
    xpad[SUBLANES + n_rows:2 * SUBLANES + n_rows, :] = zeros
    tbs = min(tbs, n_rows)
    for r0 in range(0, n_rows, tbs):
        xpad[SUBLANES + r0:SUBLANES + r0 + tbs, :] = src_ref[0, r0:r0 + tbs, :].astype(F32)
    for r0 in range(0, n_rows, tbs):
        acc = None
        for j in range(SHORT_CONV):
            off = SUBLANES + r0 + j - SHORT_CONV // 2
            term = cw[j:j + 1, :] * xpad[off:off + tbs, :]
            acc = term if acc is None else acc + term
        if bias is not None:
            acc = acc + bias
        emit(r0, acc)


def _lru_kernel(xl_ref, xc_ref, gl_ref, cw_ref, cb_ref, wa_ref, wx_ref, ba_ref, bx_ref, lam_ref, o_ref,
                xpad, xconv, abuf, bbuf, hsum, *, tb):
    n_lat, n_ctx = xl_ref.shape[1], xc_ref.shape[1]
    cw = cw_ref[...]
    cb = cb_ref[...]

    def put(off):
        def emit(r0, y):
            xconv[off + r0:off + r0 + y.shape[0], :] = y
        return emit

    _conv_rows(xc_ref, n_ctx, xpad, cw, cb, put(0), tb)
    _conv_rows(xl_ref, n_lat, xpad, cw, cb, put(n_ctx), tb)

    row = lax.broadcasted_iota(jnp.int32, (SUBLANES, LANES), 0)

    def tile_body(d, ntiles, t, carry):
        ti = t if d == 0 else ntiles - 1 - t
        r0 = pl.multiple_of(ti * SUBLANES, SUBLANES)
        h = bbuf[pl.ds(r0, SUBLANES), :] + abuf[pl.ds(r0, SUBLANES), :] * carry
        bbuf[pl.ds(r0, SUBLANES), :] = h
        last = SUBLANES - 1 if d == 0 else 0
        return jnp.broadcast_to(h[last:last + 1], (SUBLANES, LANES))

    def local_scans(d, a, b):
        nt = a.shape[0] // SUBLANES
        a3 = a.reshape(nt, SUBLANES, LANES)
        b3 = b.reshape(nt, SUBLANES, LANES)
        row3 = lax.broadcasted_iota(jnp.int32, a3.shape, 1)
        for k in (1, 2, 4):
            sh = k if d == 0 else SUBLANES - k
            keep = (row3 >= k) if d == 0 else (row3 <= SUBLANES - 1 - k)
            sa = jnp.where(keep, pltpu.roll(a3, sh, 1), 1.0)
            sb = jnp.where(keep, pltpu.roll(b3, sh, 1), 0.0)
            b3 = b3 + a3 * sb
            a3 = a3 * sa
        return a3.reshape(a.shape), b3.reshape(b.shape)

    for d in range(2):
        sp = _softplus(-lam_ref[d])
        wa, wx = wa_ref[d, 0], wx_ref[d, 0]
        ba, bx = ba_ref[d], bx_ref[d]

        def run_segment(off, n_rows, is_lat, carry, d=d, sp=sp, wa=wa, wx=wx, ba=ba, bx=bx):
            tbs = min(tb, n_rows)
            nblk = n_rows // tbs

            def blk_body(i, carry):
                bi = i if d == 0 else nblk - 1 - i
                lr0 = pl.multiple_of(bi * tbs, tbs)
                xc = xconv[pl.ds(off + lr0, tbs), :]
                xb = xc.astype(BF16)
                r = _sigmoid(_dot(xb, wa) + ba)
                ig = _sigmoid(_dot(xb, wx) + bx)
                log_a = -LRU_C * r * sp
                gain = jnp.sqrt(jnp.maximum(-_expm1_2x(log_a), 0.0))
                acum, hloc = local_scans(d, jnp.exp(log_a), gain * (ig * xc))
                abuf[0:tbs, :] = acum
                bbuf[0:tbs, :] = hloc
                carry = lax.fori_loop(0, tbs // SUBLANES,
                                      functools.partial(tile_body, d, tbs // SUBLANES), carry)
                if is_lat:
                    if d == 0:
                        hsum[pl.ds(lr0, tbs), :] = bbuf[0:tbs, :]
                    else:
                        y = (hsum[pl.ds(lr0, tbs), :] + bbuf[0:tbs, :]) * _gelu(gl_ref[0, pl.ds(lr0, tbs), :].astype(F32))
                        o_ref[0, pl.ds(lr0, tbs), :] = y.astype(BF16)
                return carry

            return lax.fori_loop(0, nblk, blk_body, carry)

        carry = run_segment(0, n_ctx, False, jnp.zeros((SUBLANES, LANES), F32))
        run_segment(n_ctx, n_lat, True, carry)


def _lru_call(main_l, main_c, conv_w, conv_b, wa, ba, wx, bx, lam):
    b, l, _ = main_l.shape
    lc = main_c.shape[1]
    width = conv_b.shape[0]
    nb = width // LRU_BLOCK
    tb = 256
    vec = pl.BlockSpec((2, 1, LANES), lambda bi, n: (0, 0, n))
    wspec = pl.BlockSpec((2, 1, LRU_BLOCK, LRU_BLOCK), lambda bi, n: (0, n, 0, 0))
    return pl.pallas_call(
        functools.partial(_lru_kernel, tb=tb),
        grid=(b, nb),
        in_specs=[pl.BlockSpec((1, l, LANES), lambda bi, n: (bi, 0, n)),
                  pl.BlockSpec((1, lc, LANES), lambda bi, n: (bi, 0, n)),
                  pl.BlockSpec((1, l, LANES), lambda bi, n, nb=nb: (bi, 0, nb + n)),
                  pl.BlockSpec((SHORT_CONV, LANES), lambda bi, n: (0, n)),
                  pl.BlockSpec((1, LANES), lambda bi, n: (0, n)),
                  wspec, wspec, vec, vec, vec],
        out_specs=pl.BlockSpec((1, l, LANES), lambda bi, n: (bi, 0, n)),
        out_shape=jax.ShapeDtypeStruct((b, l, width), BF16),
        scratch_shapes=[pltpu.VMEM((l + 2 * SUBLANES, LANES), F32),
                        pltpu.VMEM((lc + l, LANES), F32),
                        pltpu.VMEM((tb, LANES), F32),
                        pltpu.VMEM((tb, LANES), F32),
                        pltpu.VMEM((l, LANES), F32)],
        compiler_params=_params("parallel", "parallel"),
        name="rglru",
    )(main_l, main_c, main_l, conv_w.T, conv_b.reshape(1, width), wa, wx,
      ba.reshape(2, 1, width), bx.reshape(2, 1, width), lam.reshape(2, 1, width))


def _dn_gate_kernel(ab_ref, alog_ref, dtb_ref, o_ref, *, heads):
    x = ab_ref[0]
    tm = x.shape[0]
    lane = lax.broadcasted_iota(jnp.int32, x.shape, 1)
    g = -jnp.exp(alog_ref[...]) * _softplus(x + dtb_ref[...])
    ii = lax.broadcasted_iota(jnp.int32, (tm, tm), 0)
    jj = lax.broadcasted_iota(jnp.int32, (tm, tm), 1)
    same = (ii // DN_CHUNK) == (jj // DN_CHUNK)
    lower = jnp.where(same & (jj <= ii), 1.0, 0.0).astype(F32)
    upper = jnp.where(same & (jj >= ii), 1.0, 0.0).astype(F32)
    pre = jnp.dot(lower, g, preferred_element_type=F32, precision=lax.Precision.HIGHEST)
    suf = jnp.dot(upper, g, preferred_element_type=F32, precision=lax.Precision.HIGHEST)
    gc = jnp.where(lane < heads, pre, suf)
    o_ref[0] = jnp.where(lane < 2 * heads, gc, jax.nn.sigmoid(x))


def _dn_gate_call(ab, a_log, dt_bias):
    b, l, _ = ab.shape
    tm = min(l, 256)
    pad = lambda t: jnp.pad(t.reshape(1, -1), ((0, 0), (0, LANES - t.size)))
    return pl.pallas_call(
        functools.partial(_dn_gate_kernel, heads=a_log.shape[1]),
        grid=(b, l // tm),
        in_specs=[pl.BlockSpec((1, tm, LANES), lambda bi, i: (bi, i, 0)),
                  pl.BlockSpec((1, LANES), lambda bi, i: (0, 0)),
                  pl.BlockSpec((1, LANES), lambda bi, i: (0, 0))],
        out_specs=pl.BlockSpec((1, tm, LANES), lambda bi, i: (bi, i, 0)),
        out_shape=jax.ShapeDtypeStruct((b, l, LANES), F32),
        compiler_params=_params("parallel", "parallel"),
        name="deltanet_gates",
    )(ab, pad(a_log), pad(dt_bias))


def _dn_gate_layouts(gates, heads):
    b, l, _ = gates.shape
    t = gates[:, :, :4 * heads].reshape(b, l, 2, 2, heads)
    col = jnp.transpose(t, (0, 4, 1, 2, 3)).reshape(b, heads, l, 4)
    row = jnp.transpose(col.reshape(b, heads, l // DN_CHUNK, DN_CHUNK, 4), (0, 1, 2, 4, 3))
    return col, row


def _dn_kernel(ql_ref, kl_ref, vl_ref, gl_ref, qc_ref, kc_ref, vc_ref, gc_ref,
               cwq_ref, cwk_ref, cwv_ref, coll_ref, rowl_ref, colc_ref, rowc_ref, ng_ref,
               ol_ref, oc_ref,
               xpad, qs, ks, vs, oacc, pbuf, xbuf, atb, abuf, bbuf, qpbuf, egl, *, group):
    n_lat, n_ctx = ql_ref.shape[1], qc_ref.shape[1]
    c = DN_CHUNK
    tbs = 256

    def prep(src_ref, n_rows, off, cw_ref, dst, mode):
        def emit(r0, y):
            y = _silu(y)
            if mode != "v":
                y = y * lax.rsqrt(jnp.sum(y * y, axis=-1, keepdims=True) + EPS)
            if mode == "q":
                y = y * (DN_DK ** -0.5)
            dst[off + r0:off + r0 + y.shape[0], :] = y
        _conv_rows(src_ref, n_rows, xpad, cw_ref[...], None, emit, tbs)

    for src_c, src_l, cw_ref, dst, mode in ((qc_ref, ql_ref, cwq_ref, qs, "q"),
                                            (kc_ref, kl_ref, cwk_ref, ks, "k"),
                                            (vc_ref, vl_ref, cwv_ref, vs, "v")):
        prep(src_c, n_ctx, 0, cw_ref, dst, mode)
        prep(src_l, n_lat, n_ctx, cw_ref, dst, mode)

    ii = lax.broadcasted_iota(jnp.int32, (c, c), 0)
    jj = lax.broadcasted_iota(jnp.int32, (c, c), 1)
    n_apply = int(math.log2(c))
    segments = ((0, n_ctx, colc_ref, rowc_ref), (n_ctx, n_lat, coll_ref, rowl_ref))

    def phase1_group(it, grp, off, col_ref, row_ref):
        def chunk_ids(g):
            ci = it * grp + g
            return ci, pl.multiple_of(ci * c, c), pl.multiple_of(off + ci * c, c), off // c + ci

        def gates(col_ref, lr0, d):
            gcb = col_ref[0, 0, pl.ds(lr0, c), :]
            return gcb[:, d:d + 1], gcb[:, 2 + d:3 + d]

        for g in range(grp):
            ci, lr0, r0, _ = chunk_ids(g)
            q = qs[pl.ds(r0, c), :]
            k = ks[pl.ds(r0, c), :]
            v = vs[pl.ds(r0, c), :]
            kbf = k.astype(BF16)
            kk = _dot_nt(kbf, kbf)
            qk = _dot_nt(q.astype(BF16), kbf)
            rows4 = row_ref[0, 0, ci]
            for d in range(2):
                causal = (ii >= jj) if d == 0 else (ii <= jj)
                strict = (ii > jj) if d == 0 else (ii < jj)
                gc_col, beta = gates(col_ref, lr0, d)
                gc_row = rows4[d:d + 1, :]
                gam = jnp.where(causal, jnp.exp(jnp.where(causal, gc_col - gc_row, 0.0)), 0.0)
                pbuf[0, 2 * g + d] = jnp.where(strict, -(beta * kk) * gam, 0.0).astype(BF16)
                xbuf[2 * g + d] = jnp.concatenate([v * beta, (k * beta) * jnp.exp(gc_col)], axis=1)
                atb[2 * g + d] = (qk * gam).astype(BF16)

        for j in range(1, n_apply):
            for ch in range(2 * grp):
                pb = pbuf[j - 1, ch]
                pbuf[j, ch] = _dot(pb, pb).astype(BF16)
        for j in reversed(range(n_apply)):
            for ch in range(2 * grp):
                xv = xbuf[ch]
                xbuf[ch] = xv + _dot(pbuf[j, ch], xv.astype(BF16))

        for g in range(grp):
            ci, lr0, r0, cg = chunk_ids(g)
            q = qs[pl.ds(r0, c), :]
            k = ks[pl.ds(r0, c), :]
            oloc = None
            for d in range(2):
                gc_col, _ = gates(col_ref, lr0, d)
                g_last = gc_col[c - 1:c, :] if d == 0 else gc_col[0:1, :]
                kdt = (k * jnp.exp(g_last - gc_col)).T.astype(BF16)
                xv = xbuf[2 * g + d]
                ub = xv[:, 0:LANES].astype(BF16)
                wb = xv[:, LANES:2 * LANES].astype(BF16)
                at = atb[2 * g + d]
                abuf[d, cg] = _dot(kdt, wb).astype(BF16)
                bbuf[d, cg] = _dot(kdt, ub).astype(BF16)
                qpbuf[d, pl.ds(r0, c), :] = (q * jnp.exp(gc_col) - _dot(at, wb)).astype(BF16)
                part = _dot(at, ub)
                oloc = part if oloc is None else oloc + part
                egl[d, pl.ds(cg, 1), :] = jnp.broadcast_to(jnp.exp(g_last), (1, LANES))
            oacc[pl.ds(r0, c), :] = oloc

    for off, n_rows, col_ref, row_ref in segments:
        nch = n_rows // c
        grp = min(group, nch)

        def p1_body(it, carry, off=off, col_ref=col_ref, row_ref=row_ref, grp=grp):
            phase1_group(it, grp, off, col_ref, row_ref)
            return carry
        lax.fori_loop(0, nch // grp, p1_body, 0)

    def phase2(d, ci, state, off):
        r0 = pl.multiple_of(off + ci * c, c)
        cg = off // c + ci
        sb = state.astype(BF16)
        oacc[pl.ds(r0, c), :] += _dot(qpbuf[d, pl.ds(r0, c), :], sb)
        return state * egl[d, pl.ds(cg, 1), :] - _dot(abuf[d, cg], sb) + bbuf[d, cg].astype(F32)

    states = (jnp.zeros((DN_DK, LANES), F32), jnp.zeros((DN_DK, LANES), F32))
    for off, n_rows, _, _ in segments:
        nch = n_rows // c

        def p2_body(i, st, off=off, nch=nch):
            return (phase2(0, i, st[0], off), phase2(1, nch - 1 - i, st[1], off))
        states = lax.fori_loop(0, nch, p2_body, states)

    ng = ng_ref[...]
    for off, n_rows, g_ref, o_ref in ((0, n_ctx, gc_ref, oc_ref), (n_ctx, n_lat, gl_ref, ol_ref)):
        t = min(tbs, n_rows)
        for r0 in range(0, n_rows, t):
            o = oacc[off + r0:off + r0 + t, :]
            y = o * lax.rsqrt(jnp.mean(o * o, axis=-1, keepdims=True) + EPS) * ng
            o_ref[0, r0:r0 + t, :] = (y * _silu(g_ref[0, r0:r0 + t, :].astype(F32))).astype(BF16)


def _dn_call(main_l, main_c, col_l, row_l, col_c, row_c, conv_w, norm_g, col0):
    b, l, _ = main_l.shape
    lc = main_c.shape[1]
    heads = col_l.shape[1]
    lt = l + lc
    nch = lt // DN_CHUNK
    cwt = conv_w.T
    group = 16

    def blk(n_rows, which):
        return pl.BlockSpec((1, n_rows, LANES), lambda bi, h, which=which: (bi, 0, col0 + which * heads + h))

    def cw(which):
        return pl.BlockSpec((SHORT_CONV, LANES), lambda bi, h, which=which: (0, which * heads + h))

    in_specs = ([blk(l, w) for w in range(4)] + [blk(lc, w) for w in range(4)] + [cw(0), cw(1), cw(2)] +
                [pl.BlockSpec((1, 1, l, 4), lambda bi, h: (bi, h, 0, 0)),
                 pl.BlockSpec((1, 1, l // DN_CHUNK, 4, DN_CHUNK), lambda bi, h: (bi, h, 0, 0, 0)),
                 pl.BlockSpec((1, 1, lc, 4), lambda bi, h: (bi, h, 0, 0)),
                 pl.BlockSpec((1, 1, lc // DN_CHUNK, 4, DN_CHUNK), lambda bi, h: (bi, h, 0, 0, 0)),
                 pl.BlockSpec((1, LANES), lambda bi, h: (0, 0))])
    return pl.pallas_call(
        functools.partial(_dn_kernel, group=group),
        grid=(b, heads),
        in_specs=in_specs,
        out_specs=[pl.BlockSpec((1, l, LANES), lambda bi, h: (bi, 0, h)),
                   pl.BlockSpec((1, lc, LANES), lambda bi, h: (bi, 0, h))],
        out_shape=[jax.ShapeDtypeStruct((b, l, heads * LANES), BF16),
                   jax.ShapeDtypeStruct((b, lc, heads * LANES), BF16)],
        scratch_shapes=[pltpu.VMEM((l + 2 * SUBLANES, LANES), F32),
                        pltpu.VMEM((lt, LANES), F32),
                        pltpu.VMEM((lt, LANES), F32),
                        pltpu.VMEM((lt, LANES), F32),
                        pltpu.VMEM((lt, LANES), F32),
                        pltpu.VMEM((6, 2 * group, DN_CHUNK, DN_CHUNK), BF16),
                        pltpu.VMEM((2 * group, DN_CHUNK, 2 * LANES), F32),
                        pltpu.VMEM((2 * group, DN_CHUNK, DN_CHUNK), BF16),
                        pltpu.VMEM((2, nch, DN_DK, LANES), BF16),
                        pltpu.VMEM((2, nch, DN_DK, LANES), BF16),
                        pltpu.VMEM((2, lt, LANES), BF16),
                        pltpu.VMEM((2, nch, LANES), F32)],
        compiler_params=_params("parallel", "parallel"),
        name="gated_deltanet",
    )(main_l, main_l, main_l, main_l, main_c, main_c, main_c, main_c, cwt, cwt, cwt,
      col_l, row_l, col_c, row_c, norm_g.reshape(1, LANES))


def _ret_kernel(th_ref, ql_ref, kl_ref, vl_ref, qc_ref, kc_ref, vc_ref, cos_ref, sin_ref, o_ref,
                qs, kts, s_ref, dec_ref, xz_ref):
    n_lat, n_ctx = ql_ref.shape[1], qc_ref.shape[1]
    c = RET_CHUNK
    half = RET_DK // 2
    ncc, nlc = n_ctx // c, n_lat // c
    scale = RET_DK ** -0.5

    icol = lax.broadcasted_iota(jnp.int32, (c, RET_DV), 0)
    ii = lax.broadcasted_iota(jnp.int32, (c, c), 0)
    jj = lax.broadcasted_iota(jnp.int32, (c, c), 1)
    gch = []
    for d in range(2):
        lg = -jnp.exp(th_ref[d, 0])
        lg1 = lg[:, 0:1]
        fidx = (icol if d == 0 else c - 1 - icol).astype(F32)
        rel = (ii - jj) if d == 0 else (jj - ii)
        mask = rel >= 0
        dec_ref[d] = jnp.where(mask, jnp.exp(jnp.where(mask, rel, 0).astype(F32) * lg), 0.0)
        xz_ref[d, 0] = jnp.exp((fidx + 1.0) * lg1)
        xz_ref[d, 1] = jnp.exp((c - 1.0 - fidx) * lg1)
        gch.append(jnp.exp(c * lg1))
    s_ref[...] = jnp.zeros_like(s_ref)

    def prep_ctx(ci, carry):
        r0 = pl.multiple_of(ci * c, c)
        qs[pl.ds(r0, c), :] = qc_ref[0, pl.ds(r0, c), :].astype(BF16)
        k = kc_ref[0, pl.ds(r0, c), :].astype(F32) * scale
        kts[ci] = k.T.astype(BF16)
        return carry

    def prep_lat(ci, carry):
        r0 = pl.multiple_of(ci * c, c)
        cos = cos_ref[pl.ds(r0, c), :]
        sin = sin_ref[pl.ds(r0, c), :]

        def rope(t):
            t1, t2 = t[:, 0:half], t[:, half:2 * half]
            return jnp.concatenate([t1 * cos - t2 * sin, t1 * sin + t2 * cos], axis=-1)

        q = rope(ql_ref[0, pl.ds(r0, c), :].astype(F32))
        k = rope(kl_ref[0, pl.ds(r0, c), :].astype(F32)) * scale
        qs[pl.ds(pl.multiple_of(n_ctx + ci * c, c), c), :] = q.astype(BF16)
        kts[ncc + ci] = k.T.astype(BF16)
        return carry

    lax.fori_loop(0, ncc, prep_ctx, 0)
    lax.fori_loop(0, nlc, prep_lat, 0)

    def one(d, ci, row_off, ch_off, v_ref, write, first):
        r0 = pl.multiple_of(ci * c, c)
        qb = qs[pl.ds(pl.multiple_of(row_off + ci * c, c), c), :]
        kt = kts[ch_off + ci]
        v = v_ref[0, pl.ds(r0, c), :].astype(BF16)
        state = s_ref[d]
        if write:
            inner = _dot(qb, kt) * dec_ref[d]
            o = _dot(inner.astype(BF16), v) + xz_ref[d, 0] * _dot(qb, state.astype(BF16))
            if first:
                o_ref[0, pl.ds(r0, c), :] = o
            else:
                o_ref[0, pl.ds(r0, c), :] += o
        vz = (v.astype(F32) * xz_ref[d, 1]).astype(BF16)
        s_ref[d] = state * gch[d] + _dot(kt, vz)

    def ctx_body(i, carry):
        one(0, i, 0, 0, vc_ref, False, False)
        one(1, ncc - 1 - i, 0, 0, vc_ref, False, False)
        return carry

    lax.fori_loop(0, ncc, ctx_body, 0)
    for lo, hi, first in ((0, nlc // 2, True), (nlc // 2, nlc, False)):
        def lat_body(i, carry, first=first):
            one(0, i, n_ctx, ncc, vl_ref, True, first)
            one(1, nlc - 1 - i, n_ctx, ncc, vl_ref, True, first)
            return carry
        lax.fori_loop(lo, hi, lat_body, 0)


def _ret_call(main_l, main_c, theta, cos, sin, q_col0):
    b, l, _ = main_l.shape
    lc = main_c.shape[1]
    heads = theta.shape[1]
    c = RET_CHUNK
    qb0 = q_col0 // RET_DK
    kb0 = qb0 + heads
    vb0 = (q_col0 + 2 * heads * RET_DK) // RET_DV
    th = jnp.broadcast_to(theta.reshape(2, heads, 1, 1), (2, heads, 1, LANES))
    return pl.pallas_call(
        _ret_kernel,
        grid=(b, heads),
        in_specs=[pl.BlockSpec((2, 1, 1, LANES), lambda bi, h: (0, h, 0, 0)),
                  pl.BlockSpec((1, l, RET_DK), lambda bi, h: (bi, 0, qb0 + h)),
                  pl.BlockSpec((1, l, RET_DK), lambda bi, h: (bi, 0, kb0 + h)),
                  pl.BlockSpec((1, l, RET_DV), lambda bi, h: (bi, 0, vb0 + h)),
                  pl.BlockSpec((1, lc, RET_DK), lambda bi, h: (bi, 0, qb0 + h)),
                  pl.BlockSpec((1, lc, RET_DK), lambda bi, h: (bi, 0, kb0 + h)),
                  pl.BlockSpec((1, lc, RET_DV), lambda bi, h: (bi, 0, vb0 + h)),
                  pl.BlockSpec((l, RET_DK // 2), lambda bi, h: (0, 0)),
                  pl.BlockSpec((l, RET_DK // 2), lambda bi, h: (0, 0))],
        out_specs=pl.BlockSpec((1, l, RET_DV), lambda bi, h: (bi, 0, h)),
        out_shape=jax.ShapeDtypeStruct((b, l, heads * RET_DV), F32),
        scratch_shapes=[pltpu.VMEM((lc + l, RET_DK), BF16),
                        pltpu.VMEM(((lc + l) // c, RET_DK, c), BF16),
                        pltpu.VMEM((2, RET_DK, RET_DV), F32),
                        pltpu.VMEM((2, c, c), F32),
                        pltpu.VMEM((2, 2, c, RET_DV), F32)],
        compiler_params=_params("parallel", "parallel"),
        name="retention",
    )(th, main_l, main_l, main_l, main_c, main_c, main_c, cos, sin)


def _ret_finish_kernel(o_ref, r_ref, y_ref):
    o = o_ref[0]
    y = o * lax.rsqrt(jnp.mean(o * o, axis=-1, keepdims=True) + EPS)
    y_ref[0] = (y * _silu(r_ref[0].astype(F32))).astype(BF16)


def _ret_finish_call(o, main_l, r_col0):
    b, l, w = o.shape
    heads = w // RET_DV
    tm = min(l, 1024)
    rb0 = r_col0 // RET_DV
    return pl.pallas_call(
        _ret_finish_kernel,
        grid=(b, l // tm, heads),
        in_specs=[pl.BlockSpec((1, tm, RET_DV), lambda bi, i, h: (bi, i, h)),
                  pl.BlockSpec((1, tm, RET_DV), lambda bi, i, h: (bi, i, rb0 + h))],
        out_specs=pl.BlockSpec((1, tm, RET_DV), lambda bi, i, h: (bi, i, h)),
        out_shape=jax.ShapeDtypeStruct((b, l, w), BF16),
        compiler_params=_params("parallel", "parallel", "parallel"),
        name="retention_finish",
    )(o, main_l)


def _rope_tables(n_tokens):
    rows = n_tokens // GRID_W
    r, col = jnp.meshgrid(jnp.arange(rows), jnp.arange(GRID_W), indexing='ij')
    n_freq = RET_DK // 4
    inv = ROPE_BASE ** (-jnp.arange(n_freq, dtype=F32) / n_freq)
    ang = jnp.concatenate([r.reshape(-1, 1) * inv, col.reshape(-1, 1) * inv], axis=-1)
    return jnp.cos(ang), jnp.sin(ang)


def _even_layer(xl, xc, mods_l, mods_c, norm1_g, norm2_g, w_in, w_out, s5p, glu_w, glu_b,
                dn_conv_w, dn_a_log, dn_dt_bias, dn_norm_g, ffn, final_g, final_norm):
    s5_width = glu_w.shape[0]
    dn_width = dn_conv_w.shape[0] // 3
    heads = dn_width // LANES
    n_main = s5_width + 4 * dn_width
    w_main = w_in[:, :n_main].astype(BF16)
    w_small = jnp.pad(w_in[:, n_main:], ((0, 0), (0, LANES - (w_in.shape[1] - n_main)))).astype(BF16)

    main_l, ab_l = _inproj_call(xl, norm1_g, mods_l[0], mods_l[1], w_main, w_small)
    main_c, ab_c = _inproj_call(xc, norm1_g, mods_c[0], mods_c[1], w_main, w_small)

    y_l, y_c = _s5_scan_call(main_l, main_c, *s5p)
    glu_wb = glu_w.astype(BF16)
    s5_l = _s5_glu_call(y_l, glu_wb, glu_b)
    s5_c = _s5_glu_call(y_c, glu_wb, glu_b)

    col_l, row_l = _dn_gate_layouts(_dn_gate_call(ab_l, dn_a_log, dn_dt_bias), heads)
    col_c, row_c = _dn_gate_layouts(_dn_gate_call(ab_c, dn_a_log, dn_dt_bias), heads)
    dn_l, dn_c = _dn_call(main_l, main_c, col_l, row_l, col_c, row_c, dn_conv_w, dn_norm_g,
                          s5_width // LANES)

    w_o1 = w_out[:s5_width].astype(BF16)
    w_o2 = w_out[s5_width:].astype(BF16)
    xl = _outproj_call(xl, mods_l[2], s5_l, dn_l, w_o1, w_o2)
    xc = _outproj_call(xc, mods_c[2], s5_c, dn_c, w_o1, w_o2)
    w1, w3, w2 = ffn
    xl = _ffn_call(xl, norm2_g, mods_l[3], mods_l[4], mods_l[5], w1, w3, w2, final_g, final_norm)
    xc = _ffn_call(xc, norm2_g, mods_c[3], mods_c[4], mods_c[5], w1, w3, w2, final_g, False)
    return xl, xc


def _odd_layer(xl, xc, mods_l, mods_c, norm1_g, norm2_g, w_in, w_out, conv_w, conv_b, wa, ba, wx, bx, lam,
               theta, cos, sin, ffn, final_g, final_norm):
    lru_width = conv_b.shape[0]
    heads = theta.shape[1]
    w_inb = w_in.astype(BF16)
    main_l = _inproj_call(xl, norm1_g, mods_l[0], mods_l[1], w_inb)
    main_c = _inproj_call(xc, norm1_g, mods_c[0], mods_c[1], w_inb)

    lru_l = _lru_call(main_l, main_c, conv_w, conv_b, wa.astype(BF16), ba, wx.astype(BF16), bx, lam)
    q_col0 = 2 * lru_width
    o = _ret_call(main_l, main_c, theta, cos, sin, q_col0)
    ret_l = _ret_finish_call(o, main_l, q_col0 + 2 * heads * RET_DK + heads * RET_DV)

    w_o1 = w_out[:lru_width].astype(BF16)
    w_o2 = w_out[lru_width:].astype(BF16)
    xl = _outproj_call(xl, mods_l[2], lru_l, ret_l, w_o1, w_o2)
    w1, w3, w2 = ffn
    return _ffn_call(xl, norm2_g, mods_l[3], mods_l[4], mods_l[5], w1, w3, w2, final_g, final_norm)


def kernel(x, c, ctx, c_ctx, mod_w, mod_b, norm1_g, norm2_g, ffn_w1, ffn_w3, ffn_w2, final_g, even_w_in, even_w_out, s5_lam_re, s5_lam_im, s5_log_step, s5_b_re, s5_b_im, s5_c_re, s5_c_im, s5_d, s5_glu_w, s5_glu_b, dn_conv_w, dn_a_log, dn_dt_bias, dn_norm_g, odd_w_in, odd_w_out, lru_conv_w, lru_conv_b, lru_wa, lru_ba, lru_wx, lru_bx, lru_lam, ret_theta):
    bsz, n_tok, d = x.shape
    depth = mod_w.shape[0]
    assert depth == 2 and bsz + 1 <= SUBLANES
    cos, sin = _rope_tables(n_tok)

    rows = jnp.concatenate([c, c_ctx[None, :], jnp.zeros((SUBLANES - bsz - 1, d), F32)], axis=0)
    mods = _mod_call(rows, mod_w, mod_b)

    def split_mods(i):
        m = mods[i].reshape(SUBLANES, 6, d)
        ml = [m[:bsz, k][:, None, :] for k in range(6)]
        mc = [jnp.broadcast_to(m[bsz, k][None, None, :], (bsz, 1, d)) for k in range(6)]
        return ml, mc

    xl, xc = x, ctx
    ml, mc = split_mods(0)
    s5p = _s5_weights(s5_lam_re[0], s5_lam_im[0], s5_log_step[0], s5_b_re[0], s5_b_im[0],
                      s5_c_re[0], s5_c_im[0]) + (s5_d[0],)
    ffn0 = (ffn_w1[0].astype(BF16), ffn_w3[0].astype(BF16), ffn_w2[0].astype(BF16))
    xl, xc = _even_layer(xl, xc, ml, mc, norm1_g[0], norm2_g[0], even_w_in[0], even_w_out[0],
                         s5p, s5_glu_w[0], s5_glu_b[0],
                         dn_conv_w[0], dn_a_log[0], dn_dt_bias[0], dn_norm_g[0], ffn0, final_g, False)
    ml, mc = split_mods(1)
    ffn1 = (ffn_w1[1].astype(BF16), ffn_w3[1].astype(BF16), ffn_w2[1].astype(BF16))
    return _odd_layer(xl, xc, ml, mc, norm1_g[1], norm2_g[1], odd_w_in[0], odd_w_out[0],
                      lru_conv_w[0], lru_conv_b[0], lru_wa[0], lru_ba[0], lru_wx[0], lru_bx[0], lru_lam[0],
                      ret_theta[0], cos, sin, ffn1, final_g, True)
```
